```python
import math
import jax, jax.numpy as jnp
from jax import lax
import numpy as np

D_MODEL = 2048
BATCH = 4
SEQ = 4096
DEPTH = 1

GRID_W = 64
CTX_LEN = 256
HA_HEADS = 8
HA_DK = 128
HA_DV = 128
GB_HEADS = 8
GB_DK = 128
GB_DV = 128
CONV_K = 5
CHUNK = 64
N_EXPERTS = 16
CAPACITY_FACTOR = 2
D_EXPERT = 1024
NORM_EPS = 1e-6

HA_QK = HA_HEADS * HA_DK
HA_V = HA_HEADS * HA_DV
GB_QK = GB_HEADS * GB_DK
GB_V = GB_HEADS * GB_DV
GB_QKV = 2 * GB_QK + GB_V
IN_SPLITS = (HA_QK, HA_QK, HA_QK, HA_V, HA_V, GB_QKV, GB_V, 2 * GB_HEADS, 2 * GB_HEADS, 2 * D_MODEL)
N_IN = sum(IN_SPLITS)
SPLIT_POINTS = tuple(sum(IN_SPLITS[:k]) for k in range(1, len(IN_SPLITS)))

kernel_name = 'hybrid_hgrn2_gdn_ec_moe_dit_layer'


def rms_norm(x, gain):
    xf = x.astype(jnp.float32)
    y = xf * lax.rsqrt(jnp.mean(xf * xf, axis=-1, keepdims=True) + NORM_EPS)
    return (y * gain.astype(jnp.float32)).astype(x.dtype)


def head_rms_norm(o, gain, n_heads):
    b, t, w = o.shape
    oh = o.reshape(b, t, n_heads, w // n_heads)
    oh = oh * lax.rsqrt(jnp.mean(oh * oh, axis=-1, keepdims=True) + NORM_EPS)
    return oh.reshape(b, t, w) * gain.astype(jnp.float32)


def l2_normalize(a):
    return a * lax.rsqrt(jnp.sum(a * a, axis=-1, keepdims=True) + NORM_EPS)


def to_heads(a, n_heads):
    b, t, w = a.shape
    return a.reshape(b, t, n_heads, w // n_heads).transpose(0, 2, 1, 3)


def from_heads(a):
    b, h, t, d = a.shape
    return a.transpose(0, 2, 1, 3).reshape(b, t, h * d)


def to_col_major(a, rows):
    b, t, w = a.shape
    return a.reshape(b, rows, GRID_W, w).transpose(0, 2, 1, 3).reshape(b, t, w)


def from_col_major(a, rows):
    b, t, w = a.shape
    return a.reshape(b, GRID_W, rows, w).transpose(0, 2, 1, 3).reshape(b, t, w)


def centred_depthwise_conv(a, w):
    return lax.conv_general_dilated(
        a, w.astype(a.dtype)[:, None, :], window_strides=(1,),
        padding=[(CONV_K // 2, CONV_K // 2)],
        dimension_numbers=('NWC', 'WIO', 'NWC'), feature_group_count=a.shape[-1])


def modulate(h, shift, scale):
    return h * (1 + scale) + shift


def _tril(inclusive):
    i = jnp.arange(CHUNK)
    return (i[:, None] >= i[None, :]) if inclusive else (i[:, None] > i[None, :])


def hgrn2_chunk(state, xs):
    q, k, v, g = xs
    G = jnp.cumsum(g, axis=2)
    mask = _tril(True)[:, :, None]
    diff = G[:, :, :, None, :] - G[:, :, None, :, :]
    decay = jnp.where(mask, jnp.exp(jnp.where(mask, diff, 0.0)), 0.0)
    scores = jnp.einsum('bhik,bhjk,bhijk->bhij', q, k, decay)
    o = (jnp.einsum('bhik,bhkv->bhiv', q * jnp.exp(G), state)
         + jnp.einsum('bhij,bhjv->bhiv', scores, v))
    G_last = G[:, :, -1, :]
    state = (jnp.exp(G_last)[..., None] * state
             + jnp.einsum('bhjk,bhjv->bhkv', k * jnp.exp(G_last[:, :, None, :] - G), v))
    return state, o


def gdn_chunk(state, xs):
    q, k, v, g, beta = xs
    G = jnp.cumsum(g, axis=-1)
    incl = _tril(True)
    strict = _tril(False)
    diff = G[..., :, None] - G[..., None, :]
    gamma = jnp.where(incl, jnp.exp(jnp.where(incl, diff, 0.0)), 0.0)
    kk = jnp.einsum('bhik,bhjk->bhij', k, k)
    m = jnp.eye(CHUNK, dtype=jnp.float32) + jnp.where(strict, beta[..., :, None] * kk * gamma, 0.0)
    rhs = jnp.concatenate([v * beta[..., None], k * (beta * jnp.exp(G))[..., None]], axis=-1)
    sol = lax.linalg.triangular_solve(m, rhs, left_side=True, lower=True, unit_diagonal=True)
    u, w = sol[..., :GB_DV], sol[..., GB_DV:]
    v_new = u - jnp.einsum('bhik,bhkv->bhiv', w, state)
    qk = jnp.einsum('bhik,bhjk->bhij', q, k) * gamma
    o = (jnp.einsum('bhik,bhkv->bhiv', q * jnp.exp(G)[..., None], state)
         + jnp.einsum('bhij,bhjv->bhiv', qk, v_new))
    G_last = G[..., -1]
    state = (jnp.exp(G_last)[..., None, None] * state
             + jnp.einsum('bhjk,bhjv->bhkv', k * jnp.exp(G_last[..., None] - G)[..., None], v_new))
    return state, o


def _scan_chunks(chunk_fn, state, seqs):
    b, h, t = seqs[0].shape[:3]
    n = t // CHUNK
    xs = tuple(jnp.moveaxis(a.reshape((b, h, n, CHUNK) + a.shape[3:]), 2, 0) for a in seqs)
    state, o = lax.scan(chunk_fn, state, xs)
    return state, jnp.moveaxis(o, 0, 2).reshape(b, h, t, o.shape[-1])


def _context_prefixed_scan(chunk_fn, state0, ctx_seqs, lat_seqs, reverse):
    if reverse:
        ctx_seqs = tuple(jnp.flip(a, axis=2) for a in ctx_seqs)
        lat_seqs = tuple(jnp.flip(a, axis=2) for a in lat_seqs)
    s_ctx, o_ctx = _scan_chunks(chunk_fn, state0, ctx_seqs)
    _, o_lat = _scan_chunks(chunk_fn, s_ctx, lat_seqs)
    if reverse:
        o_ctx = jnp.flip(o_ctx, axis=2)
        o_lat = jnp.flip(o_lat, axis=2)
    return o_ctx, o_lat


def hgrn2_branch(ctx_parts, lat_parts, lb_fwd, lb_bwd):
    def prep(q, f_logit, i, lower):
        f_logit = f_logit.astype(jnp.float32)
        lower = lower.astype(jnp.float32)
        f = lower + (1.0 - lower) * jax.nn.sigmoid(f_logit)
        k = (1.0 - lower) * jax.nn.sigmoid(-f_logit)
        return (to_heads(jax.nn.silu(q.astype(jnp.float32)), HA_HEADS), to_heads(k, HA_HEADS),
                to_heads(i.astype(jnp.float32), HA_HEADS), to_heads(jnp.log(f), HA_HEADS))
    qc, fcf, fcb, ic = ctx_parts
    ql, flf, flb, il = lat_parts
    s0 = jnp.zeros((qc.shape[0], HA_HEADS, HA_DK, HA_DV), jnp.float32)
    oc_f, ol_f = _context_prefixed_scan(hgrn2_chunk, s0, prep(qc, fcf, ic, lb_fwd), prep(ql, flf, il, lb_fwd), False)
    oc_b, ol_b = _context_prefixed_scan(hgrn2_chunk, s0, prep(qc, fcb, ic, lb_bwd), prep(ql, flb, il, lb_bwd), True)
    return from_heads(oc_f + oc_b), from_heads(ol_f + ol_b)


def gdn_branch(ctx_parts, lat_parts, conv_w, a_log, dt_bias):
    def prep(qkv, a_logit, b_logit):
        y = jax.nn.silu(centred_depthwise_conv(qkv, conv_w).astype(jnp.float32))
        q, k, v = jnp.split(y, [GB_QK, 2 * GB_QK], axis=-1)
        q = l2_normalize(to_heads(q, GB_HEADS)) * (GB_DK ** -0.5)
        k = l2_normalize(to_heads(k, GB_HEADS))
        v = to_heads(v, GB_HEADS)
        a_f, a_b = jnp.split(a_logit.astype(jnp.float32), 2, axis=-1)
        b_f, b_b = jnp.split(b_logit.astype(jnp.float32), 2, axis=-1)
        def gates(a, bl, d):
            g = -jnp.exp(a_log[d].astype(jnp.float32)) * jax.nn.softplus(a + dt_bias[d].astype(jnp.float32))
            return g.transpose(0, 2, 1), jax.nn.sigmoid(bl).transpose(0, 2, 1)
        g_f, beta_f = gates(a_f, b_f, 0)
        g_b, beta_b = gates(a_b, b_b, 1)
        return (q, k, v, g_f, beta_f), (q, k, v, g_b, beta_b)
    c_fwd, c_bwd = prep(*ctx_parts)
    l_fwd, l_bwd = prep(*lat_parts)
    s0 = jnp.zeros((ctx_parts[0].shape[0], GB_HEADS, GB_DK, GB_DV), jnp.float32)
    oc_f, ol_f = _context_prefixed_scan(gdn_chunk, s0, c_fwd, l_fwd, False)
    oc_b, ol_b = _context_prefixed_scan(gdn_chunk, s0, c_bwd, l_bwd, True)
    return from_heads(oc_f + oc_b), from_heads(ol_f + ol_b)


def token_mixer(h_lat, h_ctx, rows, need_ctx, w_in, conv_w, a_log, dt_bias, lb_fwd, lb_bwd,
                ha_gain, gb_gain, w_ba, w_bb, w_out):
    pl = jnp.split(h_lat @ w_in, SPLIT_POINTS, axis=-1)
    pc = jnp.split(h_ctx @ w_in, SPLIT_POINTS, axis=-1)
    oA_c, oA_l = hgrn2_branch((pc[0], pc[1], pc[2], pc[3]), (pl[0], pl[1], pl[2], pl[3]), lb_fwd, lb_bwd)
    lat_b = tuple(to_col_major(p, rows) for p in (pl[5], pl[7], pl[8]))
    oB_c, oB_l = gdn_branch((pc[5], pc[7], pc[8]), lat_b, conv_w, a_log, dt_bias)
    oB_l = from_col_major(oB_l, rows)

    def merge(oA, oB, parts, dtype):
        yA = head_rms_norm(oA * jax.nn.sigmoid(parts[4].astype(jnp.float32)), ha_gain, HA_HEADS).astype(dtype)
        yB = (head_rms_norm(oB, gb_gain, GB_HEADS) * jax.nn.silu(parts[6].astype(jnp.float32))).astype(dtype)
        gate_a, gate_b = jnp.split(jax.nn.sigmoid(parts[9]), 2, axis=-1)
        return (gate_a * (yA @ w_ba) + gate_b * (yB @ w_bb)) @ w_out

    y_lat = merge(oA_l, oB_l, pl, h_lat.dtype)
    y_ctx = merge(oA_c, oB_c, pc, h_ctx.dtype) if need_ctx else None
    return y_lat, y_ctx


def expert_choice_ffn(h, router_w, w_gate, w_up, w_down):
    b, n, d = h.shape
    cap = max(1, (CAPACITY_FACTOR * n) // N_EXPERTS)
    aff = jax.nn.softmax(jnp.einsum('bnd,de->bne', h, router_w).astype(jnp.float32), axis=-1)
    gate, idx = lax.top_k(jnp.swapaxes(aff, 1, 2), cap)
    xs = jax.vmap(lambda hb, ib: hb[ib])(h, idx)
    a = jnp.einsum('becd,edf->becf', xs, w_gate)
    u = jnp.einsum('becd,edf->becf', xs, w_up)
    y = jnp.einsum('becf,efd->becd', jax.nn.silu(a) * u, w_down) * gate[..., None].astype(h.dtype)
    return jax.vmap(lambda yb, ib: jnp.zeros((n, d), yb.dtype).at[ib.reshape(-1)].add(yb.reshape(-1, d)))(y, idx)


def setup_inputs(seed: int = 0) -> dict:
    key = jax.random.key(seed)
    ks = jax.random.split(key, 24)
    f32 = jnp.float32

    def nrm(k, shape, scale):
        return jax.random.normal(k, shape, f32) * scale

    dt = jnp.exp(jax.random.uniform(ks[12], (DEPTH, 2, GB_HEADS), f32, minval=math.log(1e-3), maxval=math.log(1e-1)))
    return {
        'x': nrm(ks[0], (BATCH, SEQ, D_MODEL), 1.0),
        'c': nrm(ks[1], (BATCH, D_MODEL), 1.0),
        'ctx': nrm(ks[2], (BATCH, CTX_LEN, D_MODEL), 1.0),
        'c_ctx': nrm(ks[3], (D_MODEL,), 1.0),
        'ada_w': nrm(ks[4], (DEPTH, D_MODEL, 6 * D_MODEL), 0.5 * D_MODEL ** -0.5),
        'ada_b': nrm(ks[5], (DEPTH, 6 * D_MODEL), 0.01),
        'norm_mix': 1.0 + nrm(ks[6], (DEPTH, D_MODEL), 0.02),
        'norm_ffn': 1.0 + nrm(ks[7], (DEPTH, D_MODEL), 0.02),
        'w_in': nrm(ks[8], (DEPTH, D_MODEL, N_IN), D_MODEL ** -0.5),
        'gdn_conv': nrm(ks[9], (DEPTH, CONV_K, GB_QKV), CONV_K ** -0.5),
        'gdn_a_log': jnp.log(jax.random.uniform(ks[10], (DEPTH, 2, GB_HEADS), f32, minval=1.0, maxval=16.0)),
        'gdn_dt_bias': dt + jnp.log(-jnp.expm1(-dt)),
        'hgrn_lb': 1.0 + nrm(ks[11], (2, DEPTH + 1, HA_QK), 0.1),
        'hgrn_norm': 1.0 + nrm(ks[13], (DEPTH, HA_V), 0.02),
        'gdn_norm': 1.0 + nrm(ks[14], (DEPTH, GB_V), 0.02),
        'w_branch_a': nrm(ks[15], (DEPTH, HA_V, D_MODEL), HA_V ** -0.5),
        'w_branch_b': nrm(ks[16], (DEPTH, GB_V, D_MODEL), GB_V ** -0.5),
        'w_out': nrm(ks[17], (DEPTH, D_MODEL, D_MODEL), D_MODEL ** -0.5),
        'router_w': nrm(ks[18], (DEPTH, D_MODEL, N_EXPERTS), D_MODEL ** -0.5),
        'w_gate': nrm(ks[19], (DEPTH, N_EXPERTS, D_MODEL, D_EXPERT), D_MODEL ** -0.5),
        'w_up': nrm(ks[20], (DEPTH, N_EXPERTS, D_MODEL, D_EXPERT), D_MODEL ** -0.5),
        'w_down': nrm(ks[21], (DEPTH, N_EXPERTS, D_EXPERT, D_MODEL), D_EXPERT ** -0.5),
        'final_norm': 1.0 + nrm(ks[22], (D_MODEL,), 0.02),
    }


def reference(x, c, ctx, c_ctx, ada_w, ada_b, norm_mix, norm_ffn, w_in, gdn_conv, gdn_a_log, gdn_dt_bias,
              hgrn_lb, hgrn_norm, gdn_norm, w_branch_a, w_branch_b, w_out, router_w, w_gate, w_up, w_down,
              final_norm):
    rows = x.shape[1] // GRID_W
    lower = jnp.cumsum(jax.nn.softmax(hgrn_lb.astype(jnp.float32), axis=1), axis=1)
    x_lat, x_ctx = x, ctx
    for l in range(DEPTH):
        need_ctx = l < DEPTH - 1
        ml = [m[:, None, :] for m in jnp.split(jax.nn.silu(c) @ ada_w[l] + ada_b[l], 6, axis=-1)]
        mc = jnp.split(jax.nn.silu(c_ctx) @ ada_w[l] + ada_b[l], 6, axis=-1)
        h_lat = modulate(rms_norm(x_lat, norm_mix[l]), ml[0], ml[1])
        h_ctx = modulate(rms_norm(x_ctx, norm_mix[l]), mc[0], mc[1])
        y_lat, y_ctx = token_mixer(h_lat, h_ctx, rows, need_ctx, w_in[l], gdn_conv[l], gdn_a_log[l], gdn_dt_bias[l],
                                   lower[0, l], lower[1, l], hgrn_norm[l], gdn_norm[l],
                                   w_branch_a[l], w_branch_b[l], w_out[l])
        x_lat = x_lat + ml[2] * y_lat
        h2 = modulate(rms_norm(x_lat, norm_ffn[l]), ml[3], ml[4])
        x_lat = x_lat + ml[5] * expert_choice_ffn(h2, router_w[l], w_gate[l], w_up[l], w_down[l])
        if need_ctx:
            x_ctx = x_ctx + mc[2] * y_ctx
            h2c = modulate(rms_norm(x_ctx, norm_ffn[l]), mc[3], mc[4])
            x_ctx = x_ctx + mc[5] * expert_choice_ffn(h2c, router_w[l], w_gate[l], w_up[l], w_down[l])
    return rms_norm(x_lat, final_norm)
```

```python
import functools

import jax
import jax.numpy as jnp
from jax import lax
from jax.experimental import pallas as pl
from jax.experimental.pallas import tpu as pltpu

F32 = jnp.float32
BF16 = jnp.bfloat16
HIGHEST = lax.Precision.HIGHEST

NORM_EPS = 1e-6
GRID_W = 64
HEADS = 8
HEAD_DIM = 128
CHUNK = 64
CONV_K = 5
CAPACITY_FACTOR = 2
COL = HEADS * HEAD_DIM
LANES = 128
SUBLANES = 8
VMEM_LIMIT_BYTES = 56 * 1024 * 1024


def _cparams(*sem):
    return pltpu.CompilerParams(dimension_semantics=sem, vmem_limit_bytes=VMEM_LIMIT_BYTES)


def _dot(a, b, precision=None):
    return jnp.dot(a, b, preferred_element_type=F32, precision=precision)


def _dot_nt(a, b):
    return lax.dot_general(a, b, (((1,), (1,)), ((), ())), preferred_element_type=F32)


def _silu(x):
    return x * jax.nn.sigmoid(x)


def _mod_kernel(c_ref, w_ref, b_ref, o_ref):
    o_ref[...] = _dot(_silu(c_ref[...]), w_ref[...]) + b_ref[...]


def _modulation(c_rows, ada_w, ada_b):
    rows, d = c_rows.shape
    n = ada_w.shape[1]
    bn = 1024
    return pl.pallas_call(
        _mod_kernel,
        grid=(n // bn,),
        in_specs=[pl.BlockSpec((rows, d), lambda j: (0, 0)),
                  pl.BlockSpec((d, bn), lambda j: (0, j)),
                  pl.BlockSpec((1, bn), lambda j: (0, j))],
        out_specs=pl.BlockSpec((rows, bn), lambda j: (0, j)),
        out_shape=jax.ShapeDtypeStruct((rows, n), F32),
        compiler_params=_cparams("arbitrary"),
        name="modulation",
    )(c_rows, ada_w, ada_b.reshape(1, n))


def _proj_kernel(x_ref, gain_ref, shift_ref, scale_ref, w_ref, wab_ref, o_ref, oab_ref, h_ref):
    @pl.when(pl.program_id(2) == 0)
    def _():
        x = x_ref[...]
        y = x * lax.rsqrt(jnp.mean(x * x, axis=-1, keepdims=True) + NORM_EPS) * gain_ref[...]
        h = (y * (1.0 + scale_ref[...]) + shift_ref[...]).astype(BF16)
        h_ref[...] = h
        oab_ref[...] = _dot(h, wab_ref[...])

    o_ref[...] = _dot(h_ref[...], w_ref[...])


def _input_projection(x, gain, shift, scale, w_main, w_ab, bm):
    b, t, d = x.shape
    w = w_main.shape[1]
    per_batch = shift.shape[0] > 1
    mod_map = (lambda bi, i, j: (bi, 0, 0)) if per_batch else (lambda bi, i, j: (0, 0, 0))
    return pl.pallas_call(
        _proj_kernel,
        grid=(b, t // bm, w // COL),
        in_specs=[pl.BlockSpec((None, bm, d), lambda bi, i, j: (bi, i, 0)),
                  pl.BlockSpec((1, d), lambda bi, i, j: (0, 0)),
                  pl.BlockSpec((None, 1, d), mod_map),
                  pl.BlockSpec((None, 1, d), mod_map),
                  pl.BlockSpec((d, COL), lambda bi, i, j: (0, j)),
                  pl.BlockSpec((d, LANES), lambda bi, i, j: (0, 0))],
        out_specs=[pl.BlockSpec((None, bm, COL), lambda bi, i, j: (bi, i, j)),
                   pl.BlockSpec((None, bm, LANES), lambda bi, i, j: (bi, i, 0))],
        out_shape=[jax.ShapeDtypeStruct((b, t, w), F32),
                   jax.ShapeDtypeStruct((b, t, LANES), F32)],
        scratch_shapes=[pltpu.VMEM((bm, d), BF16)],
        compiler_params=_cparams("arbitrary", "arbitrary", "arbitrary"),
        name="input_projection",
    )(x, gain.reshape(1, d), shift, scale, w_main, w_ab)


def _tri(rev, strict=False):
    ri = lax.broadcasted_iota(jnp.int32, (CHUNK, CHUNK), 0)
    ci = lax.broadcasted_iota(jnp.int32, (CHUNK, CHUNK), 1)
    if rev:
        return (ci > ri) if strict else (ci >= ri)
    return (ci < ri) if strict else (ci <= ri)


def _boundary_rows(g, bs, rev):
    n, w = g.shape
    off = (bs >> 1) - 1 + (1 if rev else 0)
    if bs >= SUBLANES:
        pieces = [jnp.broadcast_to(g[p * bs + off:p * bs + off + 1], (bs, w)) for p in range(n // bs)]
        return jnp.concatenate(pieces, axis=0)
    sub = lax.broadcasted_iota(jnp.int32, (SUBLANES, w), 0)
    pieces = []
    for r0 in range(0, n, SUBLANES):
        acc = None
        for p in range(SUBLANES // bs):
            r = r0 + p * bs + off
            val = jnp.broadcast_to(g[r:r + 1], (SUBLANES, w))
            acc = val if acc is None else jnp.where(sub >= p * bs, val, acc)
        pieces.append(acc)
    return jnp.concatenate(pieces, axis=0)


def _hgrn_chunk(q_raw, f_raw, v, low, s_ref, rev):
    q = _silu(q_raw)
    one_m = 1.0 - low
    f = low + one_m * jax.nn.sigmoid(f_raw)
    k = one_m * jax.nn.sigmoid(-f_raw)
    g = jnp.log(f)
    gc = _dot(_tri(rev).astype(F32), g, precision=HIGHEST)
    g_last = gc[0:1] if rev else gc[CHUNK - 1:CHUNK]
    q_dec = (q * jnp.exp(gc)).astype(BF16)
    k_dec = (k * jnp.exp(g_last - gc)).astype(BF16)

    row = lax.broadcasted_iota(jnp.int32, (CHUNK, 1), 0)
    ri = lax.broadcasted_iota(jnp.int32, (CHUNK, CHUNK), 0)
    ci = lax.broadcasted_iota(jnp.int32, (CHUNK, CHUNK), 1)
    n_levels = CHUNK.bit_length() - 1
    q_lv, k_lv, same_blk = [], [], []
    for lv in range(1, n_levels + 1):
        gb = _boundary_rows(gc, 1 << lv, rev)
        upper = ((row >> (lv - 1)) & 1) == 1
        q_rows = jnp.logical_not(upper) if rev else upper
        k_rows = jnp.logical_not(q_rows)
        q_lv.append(jnp.where(q_rows, q * jnp.exp(jnp.where(q_rows, gc - gb, 0.0)), 0.0).astype(BF16))
        k_lv.append(jnp.where(k_rows, k * jnp.exp(jnp.where(k_rows, gb - gc, 0.0)), 0.0).astype(BF16))
        same_blk.append((ri >> lv) == (ci >> lv))
    eye = ri == ci
    qk = q * k
    v_b = v.astype(BF16)

    outs = []
    for h in range(HEADS):
        sl = slice(h * HEAD_DIM, (h + 1) * HEAD_DIM)
        a = jnp.where(eye, jnp.sum(qk[:, sl], axis=1, keepdims=True), 0.0)
        for lv in range(n_levels):
            a = a + jnp.where(same_blk[lv], _dot_nt(q_lv[lv][:, sl], k_lv[lv][:, sl]), 0.0)
        s = s_ref[h]
        outs.append(_dot(a.astype(BF16), v_b[:, sl]) + _dot_nt(q_dec[:, sl], s.astype(BF16)))
        s_ref[h] = jnp.exp(g_last[:, sl]) * s + _dot(v[:, sl].T.astype(BF16), k_dec[:, sl])
    return jnp.concatenate(outs, axis=1)


def _hgrn_kernel(lowf_ref, lowb_ref,
                 cqf, cff, cvf, cqb, cfb, cvb,
                 lqf, lff, lvf, lqb, lfb, lvb,
                 of_ref, ob_ref, sf_ref, sb_ref, *, n_ctx):
    t = pl.program_id(1)

    @pl.when(t == 0)
    def _():
        sf_ref[...] = jnp.zeros_like(sf_ref)
        sb_ref[...] = jnp.zeros_like(sb_ref)

    is_ctx = t < n_ctx

    def pick(c_ref, l_ref):
        return jnp.where(is_ctx, c_ref[...], l_ref[...])

    of_ref[...] = _hgrn_chunk(pick(cqf, lqf), pick(cff, lff), pick(cvf, lvf), lowf_ref[...], sf_ref, False)
    ob_ref[...] = _hgrn_chunk(pick(cqb, lqb), pick(cfb, lfb), pick(cvb, lvb), lowb_ref[...], sb_ref, True)


def _hgrn_scans(p_ctx, p_lat, low_f, low_b, base):
    b, tc, _ = p_ctx.shape
    t = p_lat.shape[1]
    n_ctx, n_lat = tc // CHUNK, t // CHUNK

    def cf(ti): return jnp.minimum(ti, n_ctx - 1)
    def cb(ti): return jnp.maximum(n_ctx - 1 - ti, 0)
    def lf(ti): return jnp.maximum(ti - n_ctx, 0)
    def lb(ti): return n_lat - 1 - jnp.maximum(ti - n_ctx, 0)

    def spec(chunk_of, col):
        return pl.BlockSpec((None, CHUNK, COL), lambda bi, ti: (bi, chunk_of(ti), base + col))

    low_spec = pl.BlockSpec((1, COL), lambda bi, ti: (0, 0))
    out_f, out_b = pl.pallas_call(
        functools.partial(_hgrn_kernel, n_ctx=n_ctx),
        grid=(b, n_ctx + n_lat),
        in_specs=[low_spec, low_spec,
                  spec(cf, 0), spec(cf, 1), spec(cf, 3), spec(cb, 0), spec(cb, 2), spec(cb, 3),
                  spec(lf, 0), spec(lf, 1), spec(lf, 3), spec(lb, 0), spec(lb, 2), spec(lb, 3)],
        out_specs=[pl.BlockSpec((None, CHUNK, COL), lambda bi, ti: (bi, lf(ti), 0)),
                   pl.BlockSpec((None, CHUNK, COL), lambda bi, ti: (bi, lb(ti), 0))],
        out_shape=[jax.ShapeDtypeStruct((b, t, COL), F32)] * 2,
        scratch_shapes=[pltpu.VMEM((HEADS, HEAD_DIM, HEAD_DIM), F32)] * 2,
        compiler_params=_cparams("arbitrary", "arbitrary"),
        name="hgrn_scans",
    )(low_f.reshape(1, COL), low_b.reshape(1, COL), *([p_ctx] * 6), *([p_lat] * 6))
    return out_f, out_b


def _gdn_prep_kernel(q_ref, k_ref, v_ref, pq_ref, pk_ref, pv_ref, nq_ref, nk_ref, nv_ref, w_ref, o_ref):
    t = pl.program_id(1)
    first = t == 0
    last = t == pl.num_programs(1) - 1
    row = lax.broadcasted_iota(jnp.int32, (CHUNK, 1), 0)
    for idx, (c_ref, p_ref, n_ref) in enumerate(((q_ref, pq_ref, nq_ref), (k_ref, pk_ref, nk_ref),
                                                 (v_ref, pv_ref, nv_ref))):
        x = c_ref[...]
        prev = jnp.where(first, 0.0, p_ref[...])
        nxt = jnp.where(last, 0.0, n_ref[...])
        w = w_ref[:, idx * COL:(idx + 1) * COL]
        xm1 = jnp.where(row == 0, prev[7:8], pltpu.roll(x, 1, 0))
        xm2 = jnp.where(row == 0, prev[6:7], jnp.where(row == 1, prev[7:8], pltpu.roll(x, 2, 0)))
        xp1 = jnp.where(row == CHUNK - 1, nxt[0:1], pltpu.roll(x, CHUNK - 1, 0))
        xp2 = jnp.where(row == CHUNK - 1, nxt[1:2],
                        jnp.where(row == CHUNK - 2, nxt[0:1], pltpu.roll(x, CHUNK - 2, 0)))
        y = _silu(w[0:1] * xm2 + w[1:2] * xm1 + w[2:3] * x + w[3:4] * xp1 + w[4:5] * xp2)
        if idx < 2:
            scale = HEAD_DIM ** -0.5 if idx == 0 else 1.0
            pieces = []
            for h in range(HEADS):
                yh = y[:, h * HEAD_DIM:(h + 1) * HEAD_DIM]
                inv = lax.rsqrt(jnp.sum(yh * yh, axis=1, keepdims=True) + NORM_EPS)
                pieces.append(yh * (inv * scale))
            y = jnp.concatenate(pieces, axis=1)
        o_ref[:, idx * COL:(idx + 1) * COL] = y


def _gdn_prep(p, conv_w, base, column_major):
    b, t, w = p.shape
    nblk = w // COL
    n_chunks = t // CHUNK
    rb = CHUNK // SUBLANES
    if column_major:
        assert t == CHUNK * GRID_W
        arr = p.reshape(b, CHUNK, GRID_W * w)
        def cur(col): return pl.BlockSpec((None, CHUNK, COL), lambda bi, c: (bi, 0, c * nblk + col))
        def prv(col): return pl.BlockSpec((None, SUBLANES, COL),
                                          lambda bi, c: (bi, rb - 1, jnp.maximum(c - 1, 0) * nblk + col))
        def nxt(col): return pl.BlockSpec((None, SUBLANES, COL),
                                          lambda bi, c: (bi, 0, jnp.minimum(c + 1, n_chunks - 1) * nblk + col))
    else:
        arr = p
        def cur(col): return pl.BlockSpec((None, CHUNK, COL), lambda bi, c: (bi, c, col))
        def prv(col): return pl.BlockSpec((None, SUBLANES, COL),
                                          lambda bi, c: (bi, jnp.maximum(c * rb - 1, 0), col))
        def nxt(col): return pl.BlockSpec((None, SUBLANES, COL),
                                          lambda bi, c: (bi, jnp.minimum((c + 1) * rb, n_chunks * rb - 1), col))
    cols = (base + 5, base + 6, base + 7)
    return pl.pallas_call(
        _gdn_prep_kernel,
        grid=(b, n_chunks),
        in_specs=[cur(c) for c in cols] + [prv(c) for c in cols] + [nxt(c) for c in cols]
                 + [pl.BlockSpec((CONV_K, 3 * COL), lambda bi, c: (0, 0))],
        out_specs=pl.BlockSpec((None, None, CHUNK, 3 * COL), lambda bi, c: (bi, c, 0, 0)),
        out_shape=jax.ShapeDtypeStruct((b, n_chunks, CHUNK, 3 * COL), F32),
        compiler_params=_cparams("arbitrary", "arbitrary"),
        name="gdn_prep",
    )(*([arr] * 9), conv_w)


def _unit_tri_inverse(n):
    ri = lax.broadcasted_iota(jnp.int32, n.shape, 0)
    ci = lax.broadcasted_iota(jnp.int32, n.shape, 1)
    x = jnp.where(ri == ci, 1.0, 0.0) - n
    p = n
    for _ in range(CHUNK.bit_length() - 2):
        p = _dot(p, p, precision=HIGHEST)
        x = x + _dot(x, p, precision=HIGHEST)
    return x


def _gdn_chunk(qkv, ab, alog, dtb, s_ref, rev):
    g_all = -jnp.exp(alog) * jax.nn.softplus(ab + dtb)
    beta_all = jax.nn.sigmoid(ab)
    gc_all = _dot(_tri(rev).astype(F32), g_all, precision=HIGHEST)
    gr_all = gc_all.T
    incl = _tri(rev)
    strict = _tri(rev, strict=True)
    off = HEADS if rev else 0
    outs = []
    for h in range(HEADS):
        gcol = gc_all[:, off + h:off + h + 1]
        grow = gr_all[off + h:off + h + 1, :]
        beta = beta_all[:, 2 * HEADS + off + h:2 * HEADS + off + h + 1]
        g_last = gcol[0:1] if rev else gcol[CHUNK - 1:CHUNK]
        gamma = jnp.where(incl, jnp.exp(jnp.where(incl, gcol - grow, 0.0)), 0.0)
        q = qkv[:, h * HEAD_DIM:(h + 1) * HEAD_DIM]
        k = qkv[:, COL + h * HEAD_DIM:COL + (h + 1) * HEAD_DIM]
        v = qkv[:, 2 * COL + h * HEAD_DIM:2 * COL + (h + 1) * HEAD_DIM]
        k_b = k.astype(BF16)
        n = jnp.where(strict, beta * _dot_nt(k_b, k_b) * gamma, 0.0)
        rhs = jnp.concatenate([v * beta, k * (beta * jnp.exp(gcol))], axis=1)
        sol = _dot(_unit_tri_inverse(n), rhs, precision=HIGHEST)
        u, w = sol[:, :HEAD_DIM], sol[:, HEAD_DIM:]
        s = s_ref[h]
        s_b = s.astype(BF16)
        v_new = u - _dot_nt(w.astype(BF16), s_b)
        qk = _dot_nt(q.astype(BF16), k_b) * gamma
        outs.append(_dot_nt((q * jnp.exp(gcol)).astype(BF16), s_b) + _dot(qk.astype(BF16), v_new.astype(BF16)))
        k_dec = (k * jnp.exp(g_last - gcol)).astype(BF16)
        s_ref[h] = jnp.exp(g_last) * s + _dot(v_new.T.astype(BF16), k_dec)
    return jnp.concatenate(outs, axis=1)


def _gdn_kernel(alog_ref, dtb_ref, cxf, caf, cxb, cab, lxf, laf, lxb, lab,
                of_ref, ob_ref, sf_ref, sb_ref, *, n_ctx):
    t = pl.program_id(1)

    @pl.when(t == 0)
    def _():
        sf_ref[...] = jnp.zeros_like(sf_ref)
        sb_ref[...] = jnp.zeros_like(sb_ref)

    is_ctx = t < n_ctx

    def pick(c_ref, l_ref):
        return jnp.where(is_ctx, c_ref[...], l_ref[...])

    of_ref[...] = _gdn_chunk(pick(cxf, lxf), pick(caf, laf), alog_ref[...], dtb_ref[...], sf_ref, False)
    ob_ref[...] = _gdn_chunk(pick(cxb, lxb), pick(cab, lab), alog_ref[...], dtb_ref[...], sb_ref, True)


def _gdn_scans(x_ctx, x_lat, ab_ctx, ab_lat, alog_row, dtb_row):
    b, n_ctx = x_ctx.shape[:2]
    n_lat = x_lat.shape[1]
    t = n_lat * CHUNK
    ab_lat_cm = ab_lat.reshape(b, CHUNK, GRID_W * LANES)

    def cf(ti): return jnp.minimum(ti, n_ctx - 1)
    def cb(ti): return jnp.maximum(n_ctx - 1 - ti, 0)
    def lf(ti): return jnp.maximum(ti - n_ctx, 0)
    def lb(ti): return n_lat - 1 - jnp.maximum(ti - n_ctx, 0)

    def xspec(chunk_of):
        return pl.BlockSpec((None, None, CHUNK, 3 * COL), lambda bi, ti: (bi, chunk_of(ti), 0, 0))

    def ab_ctx_spec(chunk_of):
        return pl.BlockSpec((None, CHUNK, LANES), lambda bi, ti: (bi, chunk_of(ti), 0))

    def ab_lat_spec(chunk_of):
        return pl.BlockSpec((None, CHUNK, LANES), lambda bi, ti: (bi, 0, chunk_of(ti)))

    row_spec = pl.BlockSpec((1, LANES), lambda bi, ti: (0, 0))
    out_f, out_b = pl.pallas_call(
        functools.partial(_gdn_kernel, n_ctx=n_ctx),
        grid=(b, n_ctx + n_lat),
        in_specs=[row_spec, row_spec,
                  xspec(cf), ab_ctx_spec(cf), xspec(cb), ab_ctx_spec(cb),
                  xspec(lf), ab_lat_spec(lf), xspec(lb), ab_lat_spec(lb)],
        out_specs=[pl.BlockSpec((None, CHUNK, COL), lambda bi, ti: (bi, 0, lf(ti))),
                   pl.BlockSpec((None, CHUNK, COL), lambda bi, ti: (bi, 0, lb(ti)))],
        out_shape=[jax.ShapeDtypeStruct((b, CHUNK, GRID_W * COL), F32)] * 2,
        scratch_shapes=[pltpu.VMEM((HEADS, HEAD_DIM, HEAD_DIM), F32)] * 2,
        compiler_params=_cparams("arbitrary", "arbitrary"),
        name="gdn_scans",
    )(alog_row, dtb_row, x_ctx, ab_ctx, x_ctx, ab_ctx, x_lat, ab_lat_cm, x_lat, ab_lat_cm)
    return out_f.reshape(b, t, COL), out_b.reshape(b, t, COL)


def _head_norm(x):
    pieces = []
    for h in range(HEADS):
        xh = x[:, h * HEAD_DIM:(h + 1) * HEAD_DIM]
        pieces.append(xh * lax.rsqrt(jnp.mean(xh * xh, axis=1, keepdims=True) + NORM_EPS))
    return jnp.concatenate(pieces, axis=1)


def _merge_kernel(oaf, oab, og, obf, obb, z, ga, gb, hag, gbg, wba, wbb, u_ref):
    y_a = _head_norm((oaf[...] + oab[...]) * jax.nn.sigmoid(og[...])) * hag[...]
    y_b = _head_norm(obf[...] + obb[...]) * gbg[...] * _silu(z[...])
    p_a = _dot(y_a.astype(BF16), wba[...])
    p_b = _dot(y_b.astype(BF16), wbb[...])
    u_ref[...] = (jax.nn.sigmoid(ga[...]) * p_a + jax.nn.sigmoid(gb[...]) * p_b).astype(u_ref.dtype)


def _merge(p_lat, oa_f, oa_b, ob_f, ob_b, ha_gain, gb_gain, w_ba, w_bb, base, bm):
    b, t, _ = p_lat.shape
    d = w_ba.shape[1]
    assert d % COL == 0
    def o_spec(): return pl.BlockSpec((None, bm, COL), lambda bi, i: (bi, i, 0))
    def p_spec(col): return pl.BlockSpec((None, bm, COL), lambda bi, i: (bi, i, col))
    def gate_spec(k): return pl.BlockSpec((None, bm, d), lambda bi, i: (bi, i, k))
    def full(shape): return pl.BlockSpec(shape, lambda bi, i: (0,) * len(shape))
    return pl.pallas_call(
        _merge_kernel,
        grid=(b, t // bm),
        in_specs=[o_spec(), o_spec(), p_spec(base + 4), o_spec(), o_spec(), p_spec(base + 8),
                  gate_spec(0), gate_spec(1), full((1, COL)), full((1, COL)),
                  full((COL, d)), full((COL, d))],
        out_specs=pl.BlockSpec((None, bm, d), lambda bi, i: (bi, i, 0)),
        out_shape=jax.ShapeDtypeStruct((b, t, d), BF16),
        compiler_params=_cparams("arbitrary", "arbitrary"),
        name="merge",
    )(oa_f, oa_b, p_lat, ob_f, ob_b, p_lat, p_lat, p_lat,
      ha_gain.reshape(1, COL), gb_gain.reshape(1, COL), w_ba, w_bb)


def _outproj_kernel(u_ref, wout_ref, x_ref, g_ref, nrm_ref, sh_ref, sc_ref, rw_ref,
                    x1_ref, h2_ref, aff_ref, *, n_experts):
    x1 = x_ref[...] + g_ref[...] * _dot(u_ref[...], wout_ref[...])
    x1_ref[...] = x1
    y = x1 * lax.rsqrt(jnp.mean(x1 * x1, axis=-1, keepdims=True) + NORM_EPS) * nrm_ref[...]
    h2 = y * (1.0 + sc_ref[...]) + sh_ref[...]
    h2_ref[...] = h2
    logits = _dot(h2, rw_ref[...], precision=HIGHEST)
    lane = lax.broadcasted_iota(jnp.int32, logits.shape, 1)
    logits = jnp.where(lane < n_experts, logits, -jnp.inf)
    e = jnp.exp(logits - jnp.max(logits, axis=-1, keepdims=True))
    aff_ref[...] = e / jnp.sum(e, axis=-1, keepdims=True)


def _outproj_router(u, w_out, x, gate, norm_gain, shift, scale, router_pad, n_experts, bm):
    b, t, d = x.shape
    def row(): return pl.BlockSpec((None, bm, d), lambda bi, i: (bi, i, 0))
    def mod(): return pl.BlockSpec((None, 1, d), lambda bi, i: (bi, 0, 0))
    return pl.pallas_call(
        functools.partial(_outproj_kernel, n_experts=n_experts),
        grid=(b, t // bm),
        in_specs=[row(), pl.BlockSpec((d, d), lambda bi, i: (0, 0)), row(), mod(),
                  pl.BlockSpec((1, d), lambda bi, i: (0, 0)), mod(), mod(),
                  pl.BlockSpec((d, LANES), lambda bi, i: (0, 0))],
        out_specs=[row(), row(), pl.BlockSpec((None, bm, LANES), lambda bi, i: (bi, i, 0))],
        out_shape=[jax.ShapeDtypeStruct((b, t, d), F32), jax.ShapeDtypeStruct((b, t, d), F32),
                   jax.ShapeDtypeStruct((b, t, LANES), F32)],
        compiler_params=_cparams("arbitrary", "arbitrary"),
        name="outproj_router",
    )(u, w_out, x, gate, norm_gain.reshape(1, d), shift, scale, router_pad)


def _ffn_kernel(idx_ref, gate_ref, h_hbm, acc_in_hbm, wg_ref, wu_ref, wd_ref, acc_hbm,
                x_buf, a_buf, sem, *, cap, half):
    b = pl.program_id(1)

    def x_copy(s):
        return pltpu.make_async_copy(h_hbm.at[b, pl.ds(idx_ref[0, s], 1)], x_buf.at[pl.ds(s, 1)], sem.at[0])

    def acc_copy(s):
        return pltpu.make_async_copy(acc_in_hbm.at[b, pl.ds(idx_ref[0, s], 1)], a_buf.at[pl.ds(s, 1)],
                                     sem.at[1])

    def out_copy(s):
        return pltpu.make_async_copy(a_buf.at[pl.ds(s, 1)], acc_hbm.at[b, pl.ds(idx_ref[0, s], 1)],
                                     sem.at[2])

    def start_gather(s, carry):
        x_copy(s).start()
        acc_copy(s).start()
        return carry

    def wait_gather(s, carry):
        x_copy(s).wait()
        acc_copy(s).wait()
        return carry

    def start_scatter(s, carry):
        out_copy(s).start()
        return carry

    def wait_scatter(s, carry):
        out_copy(s).wait()
        return carry

    lax.fori_loop(0, cap, start_gather, 0)
    lax.fori_loop(0, cap, wait_gather, 0)
    for r0 in range(0, cap, half):
        rows = pl.ds(r0, half)
        x = x_buf[rows, :].astype(BF16)
        a = _dot(x, wg_ref[...])
        u = _dot(x, wu_ref[...])
        y = _dot((_silu(a) * u).astype(BF16), wd_ref[...]) * gate_ref[rows, :]
        a_buf[rows, :] = a_buf[rows, :] + y
    lax.fori_loop(0, cap, start_scatter, 0)
    lax.fori_loop(0, cap, wait_scatter, 0)


def _expert_ffn(h2, idx, gate, w_gate, w_up, w_down):
    b, t, d = h2.shape
    e, _, f = w_gate.shape
    cap = idx.shape[-1]
    half = cap // 2 if cap % 16 == 0 else cap
    acc0 = jnp.zeros((b, t, d), F32)
    return pl.pallas_call(
        functools.partial(_ffn_kernel, cap=cap, half=half),
        grid=(e, b),
        in_specs=[pl.BlockSpec((None, None, 1, cap), lambda ei, bi: (bi, ei, 0, 0), memory_space=pltpu.SMEM),
                  pl.BlockSpec((None, None, cap, 1), lambda ei, bi: (bi, ei, 0, 0)),
                  pl.BlockSpec(memory_space=pl.ANY),
                  pl.BlockSpec(memory_space=pl.ANY),
                  pl.BlockSpec((None, d, f), lambda ei, bi: (ei, 0, 0)),
                  pl.BlockSpec((None, d, f), lambda ei, bi: (ei, 0, 0)),
                  pl.BlockSpec((None, f, d), lambda ei, bi: (ei, 0, 0))],
        out_specs=pl.BlockSpec(memory_space=pl.ANY),
        out_shape=jax.ShapeDtypeStruct((b, t, d), F32),
        scratch_shapes=[pltpu.VMEM((cap, d), F32), pltpu.VMEM((cap, d), F32),
                        pltpu.SemaphoreType.DMA((3,))],
        input_output_aliases={3: 0},
        compiler_params=_cparams("arbitrary", "arbitrary"),
        name="expert_ffn",
    )(idx.reshape(b, e, 1, cap), gate.reshape(b, e, cap, 1), h2, acc0, w_gate, w_up, w_down)


def _final_kernel(x1_ref, acc_ref, g_ref, nrm_ref, o_ref):
    x = x1_ref[...] + g_ref[...] * acc_ref[...]
    o_ref[...] = x * lax.rsqrt(jnp.mean(x * x, axis=-1, keepdims=True) + NORM_EPS) * nrm_ref[...]


def _final(x1, acc, gate, norm_gain, bm):
    b, t, d = x1.shape
    def row(): return pl.BlockSpec((None, bm, d), lambda bi, i: (bi, i, 0))
    return pl.pallas_call(
        _final_kernel,
        grid=(b, t // bm),
        in_specs=[row(), row(), pl.BlockSpec((None, 1, d), lambda bi, i: (bi, 0, 0)),
                  pl.BlockSpec((1, d), lambda bi, i: (0, 0))],
        out_specs=row(),
        out_shape=jax.ShapeDtypeStruct((b, t, d), F32),
        compiler_params=_cparams("arbitrary", "arbitrary"),
        name="final_norm",
    )(x1, acc, gate, norm_gain.reshape(1, d))


def _row_block(t):
    for bm in (512, 256, 128, 64):
        if t % bm == 0:
            return bm
    raise ValueError(f"token count {t} is not a multiple of 64")


def kernel(x, c, ctx, c_ctx, ada_w, ada_b, norm_mix, norm_ffn, w_in, gdn_conv, gdn_a_log, gdn_dt_bias,
           hgrn_lb, hgrn_norm, gdn_norm, w_branch_a, w_branch_b, w_out, router_w, w_gate, w_up, w_down,
           final_norm):
    depth = ada_w.shape[0]
    assert depth == 1, "single-layer stack"
    b, t, d = x.shape
    tc = ctx.shape[1]
    n_experts = router_w.shape[-1]
    assert d % COL == 0 and t == CHUNK * GRID_W and tc % CHUNK == 0
    assert w_in.shape[-1] == 9 * COL + 4 * HEADS + 2 * d and n_experts <= LANES

    lower = jnp.cumsum(jax.nn.softmax(hgrn_lb.astype(F32), axis=1), axis=1)

    c_rows = jnp.zeros((SUBLANES, d), F32).at[:b].set(c).at[b].set(c_ctx)
    mods = _modulation(c_rows, ada_w[0], ada_b[0])
    ml = [m.reshape(b, 1, d) for m in jnp.split(mods[:b], 6, axis=-1)]
    mc = [m.reshape(1, 1, d) for m in jnp.split(mods[b:b + 1], 6, axis=-1)]

    wi = w_in[0]
    s_gates = 9 * COL + 4 * HEADS
    w_main = jnp.concatenate([wi[:, s_gates:], wi[:, :9 * COL]], axis=1).astype(BF16)
    w_ab = jnp.pad(wi[:, 9 * COL:s_gates], ((0, 0), (0, LANES - 4 * HEADS))).astype(BF16)
    base = 2 * d // COL

    p_lat, ab_lat = _input_projection(x, norm_mix[0], ml[0], ml[1], w_main, w_ab, _row_block(t))
    p_ctx, ab_ctx = _input_projection(ctx, norm_mix[0], mc[0], mc[1], w_main, w_ab, _row_block(tc))

    oa_f, oa_b = _hgrn_scans(p_ctx, p_lat, lower[0, 0], lower[1, 0], base)

    x_ctx = _gdn_prep(p_ctx, gdn_conv[0], base, column_major=False)
    x_lat = _gdn_prep(p_lat, gdn_conv[0], base, column_major=True)
    pad = LANES - 2 * HEADS
    alog_row = jnp.pad(gdn_a_log[0].reshape(1, 2 * HEADS).astype(F32), ((0, 0), (0, pad)))
    dtb_row = jnp.pad(gdn_dt_bias[0].reshape(1, 2 * HEADS).astype(F32), ((0, 0), (0, pad)))
    ob_f, ob_b = _gdn_scans(x_ctx, x_lat, ab_ctx, ab_lat, alog_row, dtb_row)

    u = _merge(p_lat, oa_f, oa_b, ob_f, ob_b, hgrn_norm[0], gdn_norm[0],
               w_branch_a[0].astype(BF16), w_branch_b[0].astype(BF16), base, 256)
    router_pad = jnp.pad(router_w[0].astype(F32), ((0, 0), (0, LANES - n_experts)))
    x1, h2, aff = _outproj_router(u, w_out[0].astype(BF16), x, ml[2], norm_ffn[0], ml[3], ml[4],
                                  router_pad, n_experts, 256)

    cap = max(1, (CAPACITY_FACTOR * t) // n_experts)
    gate, idx = lax.top_k(jnp.swapaxes(aff[..., :n_experts], 1, 2), cap)
    acc = _expert_ffn(h2, idx.astype(jnp.int32), gate, w_gate[0].astype(BF16), w_up[0].astype(BF16),
                      w_down[0].astype(BF16))
    return _final(x1, acc, ml[5], final_norm, 256)
```

```python
import functools

import jax
import jax.numpy as jnp
from jax import lax
from jax.experimental import pallas as pl
from jax.experimental.pallas import tpu as pltpu

F32 = jnp.float32
BF16 = jnp.bfloat16

NORM_EPS = 1e-6
GRID_W = 64
HEADS = 8
HEAD_DIM = 128
CHUNK = 64
CONV_K = 5
CAPACITY_FACTOR = 2
COL = HEADS * HEAD_DIM
LANES = 128
SUBLANES = 8
GROUP = SUBLANES
INV_BLOCK = 8
VMEM_LIMIT_BYTES = 56 * 1024 * 1024


def _cparams(*sem):
    return pltpu.CompilerParams(dimension_semantics=sem, vmem_limit_bytes=VMEM_LIMIT_BYTES)


def _dot(a, b):
    return jnp.dot(a, b, preferred_element_type=F32)


def _dot_nt(a, b):
    return lax.dot_general(a, b, (((1,), (1,)), ((), ())), preferred_element_type=F32)


def _silu(x):
    return x * jax.nn.sigmoid(x)


def _split3(x):
    hi = x.astype(BF16)
    r = x - hi.astype(F32)
    mid = r.astype(BF16)
    return hi, mid, (r - mid.astype(F32)).astype(BF16)


def _scan_cumsum(g, rev):
    tri = jnp.where(_tri(rev), 1.0, 0.0).astype(BF16)
    hi, mid, lo = _split3(g)
    return _dot(tri, hi) + _dot(tri, mid) + _dot(tri, lo)


def _resident(shape, index_map):
    return pl.BlockSpec(shape, index_map, pipeline_mode=pl.Buffered(1))


def _mod_kernel(c_ref, w_ref, b_ref, o_ref):
    o_ref[...] = _dot(_silu(c_ref[...]), w_ref[...]) + b_ref[...]


def _modulation(c_rows, ada_w, ada_b):
    rows, d = c_rows.shape
    n = ada_w.shape[1]
    bn = 1024
    return pl.pallas_call(
        _mod_kernel,
        grid=(n // bn,),
        in_specs=[pl.BlockSpec((rows, d), lambda j: (0, 0)),
                  pl.BlockSpec((d, bn), lambda j: (0, j)),
                  pl.BlockSpec((1, bn), lambda j: (0, j))],
        out_specs=pl.BlockSpec((rows, bn), lambda j: (0, j)),
        out_shape=jax.ShapeDtypeStruct((rows, n), F32),
        compiler_params=_cparams("arbitrary"),
        name="modulation",
    )(c_rows, ada_w, ada_b.reshape(1, n))


def _proj_kernel(x_ref, gain_ref, shift_ref, scale_ref, w_ref, *rest, with_ab):
    if with_ab:
        wab_ref, o_ref, oab_ref, h_ref = rest
    else:
        o_ref, h_ref = rest

    @pl.when(pl.program_id(2) == 0)
    def _():
        x = x_ref[...]
        y = x * lax.rsqrt(jnp.mean(x * x, axis=-1, keepdims=True) + NORM_EPS) * gain_ref[...]
        h = (y * (1.0 + scale_ref[...]) + shift_ref[...]).astype(BF16)
        h_ref[...] = h
        if with_ab:
            oab_ref[...] = _dot(h, wab_ref[...])

    o_ref[...] = _dot(h_ref[...], w_ref[...]).astype(o_ref.dtype)


def _input_projection(x, gain, shift, scale, w, w_ab, out_dtype, bm):
    b, t, d = x.shape
    n = w.shape[1]
    per_batch = shift.shape[0] > 1
    mod_map = (lambda bi, i, j: (bi, 0, 0)) if per_batch else (lambda bi, i, j: (0, 0, 0))
    with_ab = w_ab is not None
    in_specs = [pl.BlockSpec((None, bm, d), lambda bi, i, j: (bi, i, 0)),
                pl.BlockSpec((1, d), lambda bi, i, j: (0, 0)),
                pl.BlockSpec((None, 1, d), mod_map),
                pl.BlockSpec((None, 1, d), mod_map),
                pl.BlockSpec((d, COL), lambda bi, i, j: (0, j))]
    out_specs = [pl.BlockSpec((None, bm, COL), lambda bi, i, j: (bi, i, j))]
    out_shape = [jax.ShapeDtypeStruct((b, t, n), out_dtype)]
    args = [x, gain.reshape(1, d), shift, scale, w]
    if with_ab:
        in_specs.append(pl.BlockSpec((d, LANES), lambda bi, i, j: (0, 0)))
        out_specs.append(pl.BlockSpec((None, bm, LANES), lambda bi, i, j: (bi, i, 0)))
        out_shape.append(jax.ShapeDtypeStruct((b, t, LANES), F32))
        args.append(w_ab)
    return pl.pallas_call(
        functools.partial(_proj_kernel, with_ab=with_ab),
        grid=(b, t // bm, n // COL),
        in_specs=in_specs, out_specs=out_specs, out_shape=out_shape,
        scratch_shapes=[pltpu.VMEM((bm, d), BF16)],
        compiler_params=_cparams("arbitrary", "arbitrary", "arbitrary"),
        name="input_projection",
    )(*args)


def _tri(rev, strict=False):
    ri = lax.broadcasted_iota(jnp.int32, (CHUNK, CHUNK), 0)
    ci = lax.broadcasted_iota(jnp.int32, (CHUNK, CHUNK), 1)
    if rev:
        return (ci > ri) if strict else (ci >= ri)
    return (ci < ri) if strict else (ci <= ri)


def _boundary_rows(g, bs, rev):
    n, w = g.shape
    off = (bs >> 1) - 1 + (1 if rev else 0)
    if bs >= SUBLANES:
        pieces = [jnp.broadcast_to(g[p * bs + off:p * bs + off + 1], (bs, w)) for p in range(n // bs)]
        return jnp.concatenate(pieces, axis=0)
    sub = lax.broadcasted_iota(jnp.int32, (SUBLANES, w), 0)
    pieces = []
    for r0 in range(0, n, SUBLANES):
        acc = None
        for p in range(SUBLANES // bs):
            r = r0 + p * bs + off
            val = jnp.broadcast_to(g[r:r + 1], (SUBLANES, w))
            acc = val if acc is None else jnp.where(sub >= p * bs, val, acc)
        pieces.append(acc)
    return jnp.concatenate(pieces, axis=0)


def _hgrn_chunks(scans):
    row = lax.broadcasted_iota(jnp.int32, (CHUNK, 1), 0)
    ri = lax.broadcasted_iota(jnp.int32, (CHUNK, CHUNK), 0)
    ci = lax.broadcasted_iota(jnp.int32, (CHUNK, CHUNK), 1)
    n_levels = CHUNK.bit_length() - 1
    same_blk = [(ri >> lv) == (ci >> lv) for lv in range(1, n_levels + 1)]
    eye = ri == ci

    units = []
    for q_raw, f_raw, v, low, s_ref, rev in scans:
        q = _silu(q_raw.astype(F32))
        f_raw = f_raw.astype(F32)
        one_m = 1.0 - low
        f = low + one_m * jax.nn.sigmoid(f_raw)
        k = one_m * jax.nn.sigmoid(-f_raw)
        gc = _scan_cumsum(jnp.log(f), rev)
        g_last = gc[0:1] if rev else gc[CHUNK - 1:CHUNK]
        q_dec = (q * jnp.exp(gc)).astype(BF16)
        k_dec = (k * jnp.exp(g_last - gc)).astype(BF16)
        q_lv, k_lv = [], []
        for lv in range(1, n_levels + 1):
            gb = _boundary_rows(gc, 1 << lv, rev)
            upper = ((row >> (lv - 1)) & 1) == 1
            q_rows = jnp.logical_not(upper) if rev else upper
            k_rows = jnp.logical_not(q_rows)
            q_lv.append(jnp.where(q_rows, q * jnp.exp(jnp.where(q_rows, gc - gb, 0.0)), 0.0).astype(BF16))
            k_lv.append(jnp.where(k_rows, k * jnp.exp(jnp.where(k_rows, gb - gc, 0.0)), 0.0).astype(BF16))
        qk = q * k
        v_b = v.astype(BF16)
        v_f = v.astype(F32)
        for h in range(HEADS):
            sl = slice(h * HEAD_DIM, (h + 1) * HEAD_DIM)
            units.append(dict(s_ref=s_ref, h=h, diag=jnp.sum(qk[:, sl], axis=1, keepdims=True),
                              q_lv=[x[:, sl] for x in q_lv], k_lv=[x[:, sl] for x in k_lv],
                              q_dec=q_dec[:, sl], k_dec=k_dec[:, sl], v_b=v_b[:, sl], v_f=v_f[:, sl],
                              decay=jnp.exp(g_last[:, sl])))

    scores = [[_dot_nt(u["q_lv"][lv], u["k_lv"][lv]) for lv in range(n_levels)] for u in units]
    states = [u["s_ref"][u["h"]] for u in units]
    carried = [_dot_nt(u["q_dec"], s.astype(BF16)) for u, s in zip(units, states)]
    outs = []
    for u, sc, car in zip(units, scores, carried):
        a = jnp.where(eye, u["diag"], 0.0)
        for lv in range(n_levels):
            a = a + jnp.where(same_blk[lv], sc[lv], 0.0)
        outs.append(_dot(a.astype(BF16), u["v_b"]) + car)
    for u, s in zip(units, states):
        u["s_ref"][u["h"]] = u["decay"] * s + _dot(u["v_f"].T.astype(BF16), u["k_dec"])
    return [jnp.concatenate(outs[i * HEADS:(i + 1) * HEADS], axis=1) for i in range(len(scans))]


def _hgrn_kernel(lowf_ref, lowb_ref,
                 cqf, cff, cvf, cqb, cfb, cvb,
                 lqf, lff, lvf, lqb, lfb, lvb,
                 of_ref, ob_ref, sf_ref, sb_ref, *, n_ctx):
    t = pl.program_id(1)

    @pl.when(t == 0)
    def _():
        sf_ref[...] = jnp.zeros_like(sf_ref)
        sb_ref[...] = jnp.zeros_like(sb_ref)

    is_ctx = t < n_ctx

    def pick(c_ref, l_ref):
        return jnp.where(is_ctx, c_ref[...], l_ref[...])

    o_f, o_b = _hgrn_chunks([(pick(cqf, lqf), pick(cff, lff), pick(cvf, lvf), lowf_ref[...], sf_ref, False),
                             (pick(cqb, lqb), pick(cfb, lfb), pick(cvb, lvb), lowb_ref[...], sb_ref, True)])
    of_ref[...] = o_f
    ob_ref[...] = o_b


def _hgrn_scans(p_ctx, p_lat, low_f, low_b):
    b, tc, _ = p_ctx.shape
    t = p_lat.shape[1]
    n_ctx, n_lat = tc // CHUNK, t // CHUNK

    def cf(ti): return jnp.minimum(ti, n_ctx - 1)
    def cb(ti): return jnp.maximum(n_ctx - 1 - ti, 0)
    def lf(ti): return jnp.maximum(ti - n_ctx, 0)
    def lb(ti): return n_lat - 1 - jnp.maximum(ti - n_ctx, 0)

    def spec(chunk_of, col):
        return pl.BlockSpec((None, CHUNK, COL), lambda bi, ti: (bi, chunk_of(ti), col))

    low_spec = pl.BlockSpec((1, COL), lambda bi, ti: (0, 0))
    out_f, out_b = pl.pallas_call(
        functools.partial(_hgrn_kernel, n_ctx=n_ctx),
        grid=(b, n_ctx + n_lat),
        in_specs=[low_spec, low_spec,
                  spec(cf, 0), spec(cf, 1), spec(cf, 3), spec(cb, 0), spec(cb, 2), spec(cb, 3),
                  spec(lf, 0), spec(lf, 1), spec(lf, 3), spec(lb, 0), spec(lb, 2), spec(lb, 3)],
        out_specs=[pl.BlockSpec((None, CHUNK, COL), lambda bi, ti: (bi, lf(ti), 0)),
                   pl.BlockSpec((None, CHUNK, COL), lambda bi, ti: (bi, lb(ti), 0))],
        out_shape=[jax.ShapeDtypeStruct((b, t, COL), F32)] * 2,
        scratch_shapes=[pltpu.VMEM((HEADS, HEAD_DIM, HEAD_DIM), F32)] * 2,
        compiler_params=_cparams("arbitrary", "arbitrary"),
        name="hgrn_scans",
    )(low_f.reshape(1, COL), low_b.reshape(1, COL), *([p_ctx] * 6), *([p_lat] * 6))
    return out_f, out_b


def _conv_silu_norm(x, prev, nxt, w, idx):
    row = lax.broadcasted_iota(jnp.int32, (CHUNK, 1), 0)
    xm1 = jnp.where(row == 0, prev[7:8], pltpu.roll(x, 1, 0))
    xm2 = jnp.where(row == 0, prev[6:7], jnp.where(row == 1, prev[7:8], pltpu.roll(x, 2, 0)))
    xp1 = jnp.where(row == CHUNK - 1, nxt[0:1], pltpu.roll(x, CHUNK - 1, 0))
    xp2 = jnp.where(row == CHUNK - 1, nxt[1:2],
                    jnp.where(row == CHUNK - 2, nxt[0:1], pltpu.roll(x, CHUNK - 2, 0)))
    y = _silu(w[0:1] * xm2 + w[1:2] * xm1 + w[2:3] * x + w[3:4] * xp1 + w[4:5] * xp2)
    if idx == 2:
        return y
    scale = HEAD_DIM ** -0.5 if idx == 0 else 1.0
    pieces = []
    for h in range(HEADS):
        yh = y[:, h * HEAD_DIM:(h + 1) * HEAD_DIM]
        pieces.append(yh * (lax.rsqrt(jnp.sum(yh * yh, axis=1, keepdims=True) + NORM_EPS) * scale))
    return jnp.concatenate(pieces, axis=1)


def _gdn_prep_ctx_kernel(q_ref, k_ref, v_ref, pq_ref, pk_ref, pv_ref, nq_ref, nk_ref, nv_ref, w_ref, o_ref):
    t = pl.program_id(1)
    first = t == 0
    last = t == pl.num_programs(1) - 1
    for idx, (c_ref, p_ref, n_ref) in enumerate(((q_ref, pq_ref, nq_ref), (k_ref, pk_ref, nk_ref),
                                                 (v_ref, pv_ref, nv_ref))):
        prev = jnp.where(first, 0.0, p_ref[...])
        nxt = jnp.where(last, 0.0, n_ref[...])
        y = _conv_silu_norm(c_ref[...], prev, nxt, w_ref[:, idx * COL:(idx + 1) * COL], idx)
        o_ref[:, idx * COL:(idx + 1) * COL] = y.astype(o_ref.dtype)


def _gdn_prep_ctx(qkv, conv_w):
    b, t, _ = qkv.shape
    n_chunks = t // CHUNK
    rb = CHUNK // SUBLANES
    def cur(col): return pl.BlockSpec((None, CHUNK, COL), lambda bi, c: (bi, c, col))
    def prv(col): return pl.BlockSpec((None, SUBLANES, COL), lambda bi, c: (bi, jnp.maximum(c * rb - 1, 0), col))
    def nxt(col): return pl.BlockSpec((None, SUBLANES, COL),
                                      lambda bi, c: (bi, jnp.minimum((c + 1) * rb, n_chunks * rb - 1), col))
    cols = (0, 1, 2)
    return pl.pallas_call(
        _gdn_prep_ctx_kernel,
        grid=(b, n_chunks),
        in_specs=[cur(c) for c in cols] + [prv(c) for c in cols] + [nxt(c) for c in cols]
                 + [pl.BlockSpec((CONV_K, 3 * COL), lambda bi, c: (0, 0))],
        out_specs=pl.BlockSpec((None, None, CHUNK, 3 * COL), lambda bi, c: (bi, c, 0, 0)),
        out_shape=jax.ShapeDtypeStruct((b, n_chunks, CHUNK, 3 * COL), BF16),
        compiler_params=_cparams("arbitrary", "arbitrary"),
        name="gdn_prep_ctx",
    )(*([qkv] * 9), conv_w)


def _gdn_prep_lat_kernel(q_ref, k_ref, v_ref, pq_ref, pk_ref, pv_ref, nq_ref, nk_ref, nv_ref, ab_ref, w_ref,
                         o_ref, oab_ref):
    g = pl.program_id(1)
    first = g == 0
    last = g == pl.num_programs(1) - 1
    tail = CHUNK - SUBLANES
    for idx, (c_ref, p_ref, n_ref) in enumerate(((q_ref, pq_ref, nq_ref), (k_ref, pk_ref, nk_ref),
                                                 (v_ref, pv_ref, nv_ref))):
        w = w_ref[:, idx * COL:(idx + 1) * COL]
        cols = [c_ref[:, j, :] for j in range(GROUP)]
        before = jnp.where(first, 0.0, p_ref[:, GROUP - 1, :])
        after = jnp.where(last, 0.0, n_ref[:, 0, :])
        for j in range(GROUP):
            prev = cols[j - 1][tail:] if j > 0 else before
            nxt = cols[j + 1][:SUBLANES] if j < GROUP - 1 else after
            o_ref[j, :, idx * COL:(idx + 1) * COL] = _conv_silu_norm(cols[j], prev, nxt, w, idx).astype(o_ref.dtype)
    for j in range(GROUP):
        oab_ref[j] = ab_ref[:, j, :]


def _gdn_prep_lat(qkv, ab, conv_w):
    b, t, _ = qkv.shape
    assert t == CHUNK * GRID_W and GRID_W % GROUP == 0
    n_groups = GRID_W // GROUP
    rb = CHUNK // SUBLANES
    q4 = qkv.reshape(b, CHUNK, GRID_W, 3 * COL)
    ab4 = ab.reshape(b, CHUNK, GRID_W, LANES)
    def cur(col): return pl.BlockSpec((None, CHUNK, GROUP, COL), lambda bi, g: (bi, 0, g, col))
    def prv(col): return pl.BlockSpec((None, SUBLANES, GROUP, COL),
                                      lambda bi, g: (bi, rb - 1, jnp.maximum(g - 1, 0), col))
    def nxt(col): return pl.BlockSpec((None, SUBLANES, GROUP, COL),
                                      lambda bi, g: (bi, 0, jnp.minimum(g + 1, n_groups - 1), col))
    cols = (0, 1, 2)
    return pl.pallas_call(
        _gdn_prep_lat_kernel,
        grid=(b, n_groups),
        in_specs=[cur(c) for c in cols] + [prv(c) for c in cols] + [nxt(c) for c in cols]
                 + [pl.BlockSpec((None, CHUNK, GROUP, LANES), lambda bi, g: (bi, 0, g, 0)),
                    pl.BlockSpec((CONV_K, 3 * COL), lambda bi, g: (0, 0))],
        out_specs=[pl.BlockSpec((None, GROUP, CHUNK, 3 * COL), lambda bi, g: (bi, g, 0, 0)),
                   pl.BlockSpec((None, GROUP, CHUNK, LANES), lambda bi, g: (bi, g, 0, 0))],
        out_shape=[jax.ShapeDtypeStruct((b, GRID_W, CHUNK, 3 * COL), BF16),
                   jax.ShapeDtypeStruct((b, GRID_W, CHUNK, LANES), F32)],
        compiler_params=_cparams("arbitrary", "arbitrary"),
        name="gdn_prep_lat",
    )(*([q4] * 9), ab4, conv_w)


def _mm_bf16(a, b):
    return _dot(a.astype(BF16), b.astype(BF16))


def _unit_tri_inverses(ns):
    ri = lax.broadcasted_iota(jnp.int32, (CHUNK, CHUNK), 0)
    ci = lax.broadcasted_iota(jnp.int32, (CHUNK, CHUNK), 1)
    eye = jnp.where(ri == ci, 1.0, 0.0)
    shift = INV_BLOCK.bit_length() - 1
    on_diag = (ri >> shift) == (ci >> shift)

    def nilpotent_inverses(ms, degree):
        xs = [eye - m for m in ms]
        ps = ms
        for _ in range(degree.bit_length() - 2):
            ps = [_mm_bf16(p, p) for p in ps]
            xs = [x + _mm_bf16(x, p) for x, p in zip(xs, ps)]
        return xs

    d_invs = nilpotent_inverses([jnp.where(on_diag, n, 0.0) for n in ns], INV_BLOCK)
    ms = [_mm_bf16(d, jnp.where(on_diag, 0.0, n)) for d, n in zip(d_invs, ns)]
    ys = nilpotent_inverses(ms, CHUNK // INV_BLOCK)
    return [_mm_bf16(y, d) for y, d in zip(ys, d_invs)]


def _gdn_chunks(scans, alog, dtb):
    units = []
    for x, ab, s_ref, rev in scans:
        g_all = -jnp.exp(alog) * jax.nn.softplus(ab + dtb)
        beta_all = jax.nn.sigmoid(ab)
        gc_all = _scan_cumsum(g_all, rev)
        gr_all = gc_all.T
        incl = _tri(rev)
        strict = _tri(rev, strict=True)
        off = HEADS if rev else 0
        for h in range(HEADS):
            gcol = gc_all[:, off + h:off + h + 1]
            grow = gr_all[off + h:off + h + 1, :]
            units.append(dict(
                s_ref=s_ref, h=h, strict=strict, gcol=gcol,
                beta=beta_all[:, 2 * HEADS + off + h:2 * HEADS + off + h + 1],
                g_last=gcol[0:1] if rev else gcol[CHUNK - 1:CHUNK],
                gamma=jnp.where(incl, jnp.exp(jnp.where(incl, gcol - grow, 0.0)), 0.0),
                q=x[:, h * HEAD_DIM:(h + 1) * HEAD_DIM],
                k=x[:, COL + h * HEAD_DIM:COL + (h + 1) * HEAD_DIM],
                v=x[:, 2 * COL + h * HEAD_DIM:2 * COL + (h + 1) * HEAD_DIM]))

    kks = [_dot_nt(u["k"], u["k"]) for u in units]
    qks = [_dot_nt(u["q"], u["k"]) for u in units]
    ns = [jnp.where(u["strict"], u["beta"] * kk * u["gamma"], 0.0) for u, kk in zip(units, kks)]
    invs = _unit_tri_inverses(ns)
    rhss = [jnp.concatenate([u["v"].astype(F32) * u["beta"],
                             u["k"].astype(F32) * (u["beta"] * jnp.exp(u["gcol"]))], axis=1) for u in units]
    sols = [_mm_bf16(a, r) for a, r in zip(invs, rhss)]
    states = [u["s_ref"][u["h"]] for u in units]
    states_b = [s.astype(BF16) for s in states]
    v_news = [sol[:, :HEAD_DIM] - _dot_nt(sol[:, HEAD_DIM:].astype(BF16), s_b) for sol, s_b in zip(sols, states_b)]
    outs = [_dot_nt((u["q"].astype(F32) * jnp.exp(u["gcol"])).astype(BF16), s_b)
            + _mm_bf16(qk * u["gamma"], v_new)
            for u, qk, s_b, v_new in zip(units, qks, states_b, v_news)]
    for u, s, v_new in zip(units, states, v_news):
        k_dec = (u["k"].astype(F32) * jnp.exp(u["g_last"] - u["gcol"])).astype(BF16)
        u["s_ref"][u["h"]] = jnp.exp(u["g_last"]) * s + _dot(v_new.T.astype(BF16), k_dec)
    return [jnp.concatenate(outs[i * HEADS:(i + 1) * HEADS], axis=1) for i in range(len(scans))]


def _gdn_ctx_kernel(alog_ref, dtb_ref, x_ref, ab_ref, sf_ref, sb_ref, *, n_chunks):
    sf_ref[...] = jnp.zeros_like(sf_ref)
    sb_ref[...] = jnp.zeros_like(sb_ref)

    def body(j, carry):
        jb = n_chunks - 1 - j
        _gdn_chunks([(x_ref[j], ab_ref[j], sf_ref, False), (x_ref[jb], ab_ref[jb], sb_ref, True)],
                    alog_ref[...], dtb_ref[...])
        return carry

    lax.fori_loop(0, n_chunks, body, 0)


def _gdn_lat_kernel(alog_ref, dtb_ref, s0f_ref, s0b_ref, xf_ref, abf_ref, xb_ref, abb_ref,
                    of_ref, ob_ref, sf_ref, sb_ref):
    @pl.when(pl.program_id(1) == 0)
    def _():
        sf_ref[...] = s0f_ref[...]
        sb_ref[...] = s0b_ref[...]

    def body(j, carry):
        jb = GROUP - 1 - j
        o_f, o_b = _gdn_chunks([(xf_ref[j], abf_ref[j], sf_ref, False), (xb_ref[jb], abb_ref[jb], sb_ref, True)],
                               alog_ref[...], dtb_ref[...])
        of_ref[j] = o_f
        ob_ref[jb] = o_b
        return carry

    lax.fori_loop(0, GROUP, body, 0)


def _gdn_scans(x_ctx, x_lat, ab_ctx, ab_lat, alog_row, dtb_row):
    b, n_ctx = x_ctx.shape[:2]
    n_lat = x_lat.shape[1]
    n_groups = n_lat // GROUP
    state_shape = jax.ShapeDtypeStruct((b, HEADS, HEAD_DIM, HEAD_DIM), F32)
    def state_spec(nidx):
        return pl.BlockSpec((None, HEADS, HEAD_DIM, HEAD_DIM), lambda *i: (i[0], 0, 0, 0))
    s_f, s_b = pl.pallas_call(
        functools.partial(_gdn_ctx_kernel, n_chunks=n_ctx),
        grid=(b,),
        in_specs=[pl.BlockSpec((1, LANES), lambda bi: (0, 0)), pl.BlockSpec((1, LANES), lambda bi: (0, 0)),
                  pl.BlockSpec((None, n_ctx, CHUNK, 3 * COL), lambda bi: (bi, 0, 0, 0)),
                  pl.BlockSpec((None, n_ctx, CHUNK, LANES), lambda bi: (bi, 0, 0, 0))],
        out_specs=[state_spec(1), state_spec(1)],
        out_shape=[state_shape, state_shape],
        compiler_params=_cparams("arbitrary"),
        name="gdn_ctx_scans",
    )(alog_row, dtb_row, x_ctx, ab_ctx)

    def gf(ti): return ti
    def gb(ti): return n_groups - 1 - ti
    def xspec(group_of):
        return pl.BlockSpec((None, GROUP, CHUNK, 3 * COL), lambda bi, ti: (bi, group_of(ti), 0, 0))
    def abspec(group_of):
        return pl.BlockSpec((None, GROUP, CHUNK, LANES), lambda bi, ti: (bi, group_of(ti), 0, 0))
    def ospec(group_of):
        return pl.BlockSpec((None, GROUP, CHUNK, COL), lambda bi, ti: (bi, group_of(ti), 0, 0))
    row_spec = pl.BlockSpec((1, LANES), lambda bi, ti: (0, 0))
    return pl.pallas_call(
        _gdn_lat_kernel,
        grid=(b, n_groups),
        in_specs=[row_spec, row_spec, state_spec(2), state_spec(2),
                  xspec(gf), abspec(gf), xspec(gb), abspec(gb)],
        out_specs=[ospec(gf), ospec(gb)],
        out_shape=[jax.ShapeDtypeStruct((b, n_lat, CHUNK, COL), F32)] * 2,
        scratch_shapes=[pltpu.VMEM((HEADS, HEAD_DIM, HEAD_DIM), F32)] * 2,
        compiler_params=_cparams("arbitrary", "arbitrary"),
        name="gdn_lat_scans",
    )(alog_row, dtb_row, s_f, s_b, x_lat, ab_lat, x_lat, ab_lat)


def _head_norm(x):
    pieces = []
    for h in range(HEADS):
        xh = x[:, h * HEAD_DIM:(h + 1) * HEAD_DIM]
        pieces.append(xh * lax.rsqrt(jnp.mean(xh * xh, axis=1, keepdims=True) + NORM_EPS))
    return jnp.concatenate(pieces, axis=1)


def _merge_kernel(oaf, oab, og, obf, obb, z, ga, gb, hag, gbg, wba, wbb, u_ref):
    y_a = _head_norm((oaf[...] + oab[...]) * jax.nn.sigmoid(og[...].astype(F32))) * hag[...]
    o_b = jnp.concatenate([obf[:, r, :] + obb[:, r, :] for r in range(obf.shape[1])], axis=0)
    y_b = _head_norm(o_b) * gbg[...] * _silu(z[...].astype(F32))
    u = jax.nn.sigmoid(ga[...].astype(F32)) * _dot(y_a.astype(BF16), wba[...])
    u = u + jax.nn.sigmoid(gb[...].astype(F32)) * _dot(y_b.astype(BF16), wbb[...])
    u_ref[...] = u.astype(u_ref.dtype)


def _merge(p_lat, oa_f, oa_b, ob_f, ob_b, ha_gain, gb_gain, w_ba, w_bb):
    b, t, _ = p_lat.shape
    d = w_ba.shape[1]
    bm = SUBLANES * GRID_W
    g0 = 6 * COL // d
    def o_spec(): return pl.BlockSpec((None, bm, COL), lambda bi, i: (bi, i, 0))
    def ob_spec(): return pl.BlockSpec((None, GRID_W, SUBLANES, COL), lambda bi, i: (bi, 0, i, 0))
    def p_spec(col): return pl.BlockSpec((None, bm, COL), lambda bi, i: (bi, i, col))
    def gate_spec(k): return pl.BlockSpec((None, bm, d), lambda bi, i: (bi, i, g0 + k))
    def full(shape): return _resident(shape, lambda bi, i: (0,) * len(shape))
    return pl.pallas_call(
        _merge_kernel,
        grid=(b, t // bm),
        in_specs=[o_spec(), o_spec(), p_spec(4), ob_spec(), ob_spec(), p_spec(5),
                  gate_spec(0), gate_spec(1), full((1, COL)), full((1, COL)),
                  full((COL, d)), full((COL, d))],
        out_specs=pl.BlockSpec((None, bm, d), lambda bi, i: (bi, i, 0)),
        out_shape=jax.ShapeDtypeStruct((b, t, d), BF16),
        compiler_params=_cparams("arbitrary", "arbitrary"),
        name="merge",
    )(oa_f, oa_b, p_lat, ob_f, ob_b, p_lat, p_lat, p_lat,
      ha_gain.reshape(1, COL), gb_gain.reshape(1, COL), w_ba, w_bb)


def _outproj_kernel(u_ref, wout_ref, x_ref, g_ref, nrm_ref, sh_ref, sc_ref, rwh_ref, rwl_ref,
                    x1_ref, h2_ref, aff_ref, *, n_experts):
    x1 = x_ref[...] + g_ref[...] * _dot(u_ref[...], wout_ref[...])
    x1_ref[...] = x1
    y = x1 * lax.rsqrt(jnp.mean(x1 * x1, axis=-1, keepdims=True) + NORM_EPS) * nrm_ref[...]
    h2 = y * (1.0 + sc_ref[...]) + sh_ref[...]
    h2_ref[...] = h2
    h_hi = h2.astype(BF16)
    h_lo = (h2 - h_hi.astype(F32)).astype(BF16)
    logits = _dot(h_hi, rwh_ref[...]) + _dot(h_hi, rwl_ref[...]) + _dot(h_lo, rwh_ref[...])
    lane = lax.broadcasted_iota(jnp.int32, logits.shape, 1)
    logits = jnp.where(lane < n_experts, logits, -jnp.inf)
    e = jnp.exp(logits - jnp.max(logits, axis=-1, keepdims=True))
    aff_ref[...] = e / jnp.sum(e, axis=-1, keepdims=True)


def _outproj_router(u, w_out, x, gate, norm_gain, shift, scale, router_pad, n_experts, bm):
    b, t, d = x.shape
    router_hi = router_pad.astype(BF16)
    router_lo = (router_pad - router_hi.astype(F32)).astype(BF16)
    def row(): return pl.BlockSpec((None, bm, d), lambda bi, i: (bi, i, 0))
    def mod(): return pl.BlockSpec((None, 1, d), lambda bi, i: (bi, 0, 0))
    return pl.pallas_call(
        functools.partial(_outproj_kernel, n_experts=n_experts),
        grid=(b, t // bm),
        in_specs=[row(), _resident((d, d), lambda bi, i: (0, 0)), row(), mod(),
                  pl.BlockSpec((1, d), lambda bi, i: (0, 0)), mod(), mod(),
                  _resident((d, LANES), lambda bi, i: (0, 0)), _resident((d, LANES), lambda bi, i: (0, 0))],
        out_specs=[row(), row(), pl.BlockSpec((None, bm, LANES), lambda bi, i: (bi, i, 0))],
        out_shape=[jax.ShapeDtypeStruct((b, t, d), F32), jax.ShapeDtypeStruct((b, t, d), F32),
                   jax.ShapeDtypeStruct((b, t, LANES), F32)],
        compiler_params=_cparams("arbitrary", "arbitrary"),
        name="outproj_router",
    )(u, w_out, x, gate, norm_gain.reshape(1, d), shift, scale, router_hi, router_lo)


def _ffn_kernel(idx_ref, idx_next_ref, gate_ref, h_hbm, acc_in_hbm, wg_ref, wu_ref, wd_ref, acc_hbm,
                x_buf, a_buf, sem, *, cap, half):
    n_batch = pl.num_programs(1)
    b = pl.program_id(1)
    step = pl.program_id(0) * n_batch + b
    n_steps = pl.num_programs(0) * n_batch
    slot = step % 2
    other = 1 - slot
    b_next = jnp.where(b + 1 == n_batch, 0, b + 1)

    def start_gather(ids, bb, sl):
        def body(s, carry):
            n = ids[0, s]
            pltpu.make_async_copy(h_hbm.at[bb, pl.ds(n, 1)], x_buf.at[sl, pl.ds(s, 1)], sem.at[0, sl]).start()
            pltpu.make_async_copy(acc_in_hbm.at[bb, pl.ds(n, 1)], a_buf.at[sl, pl.ds(s, 1)],
                                  sem.at[1, sl]).start()
            return carry
        lax.fori_loop(0, cap, body, 0)

    def wait_gather(sl):
        pltpu.make_async_copy(h_hbm.at[b, pl.ds(0, cap)], x_buf.at[sl], sem.at[0, sl]).wait()
        pltpu.make_async_copy(acc_in_hbm.at[b, pl.ds(0, cap)], a_buf.at[sl], sem.at[1, sl]).wait()

    def start_scatter(sl):
        def body(s, carry):
            pltpu.make_async_copy(a_buf.at[sl, pl.ds(s, 1)], acc_hbm.at[b, pl.ds(idx_ref[0, s], 1)],
                                  sem.at[2, sl]).start()
            return carry
        lax.fori_loop(0, cap, body, 0)

    def wait_scatter(sl):
        pltpu.make_async_copy(a_buf.at[sl], acc_hbm.at[b, pl.ds(0, cap)], sem.at[2, sl]).wait()

    @pl.when(step == 0)
    def _():
        start_gather(idx_ref, b, slot)

    @pl.when(step > 0)
    def _():
        wait_scatter(other)

    @pl.when(step + 1 < n_steps)
    def _():
        start_gather(idx_next_ref, b_next, other)

    wait_gather(slot)
    for r0 in range(0, cap, half):
        rows = pl.ds(r0, half)
        x = x_buf[slot, rows, :].astype(BF16)
        a = _dot(x, wg_ref[...])
        u = _dot(x, wu_ref[...])
        y = _dot((_silu(a) * u).astype(BF16), wd_ref[...]) * gate_ref[rows, :]
        a_buf[slot, rows, :] = a_buf[slot, rows, :] + y
    start_scatter(slot)

    @pl.when(step == n_steps - 1)
    def _():
        wait_scatter(slot)


def _expert_ffn(h2, idx, gate, w_gate, w_up, w_down):
    b, t, d = h2.shape
    e, _, f = w_gate.shape
    cap = idx.shape[-1]
    assert b >= 2
    half = cap // 2 if cap % 16 == 0 else cap
    acc0 = jnp.zeros((b, t, d), F32)
    idx4 = idx.reshape(b, e, 1, cap)

    def next_pair(ei, bi):
        nb = bi + 1
        return nb % b, jnp.minimum(ei + nb // b, e - 1), 0, 0

    return pl.pallas_call(
        functools.partial(_ffn_kernel, cap=cap, half=half),
        grid=(e, b),
        in_specs=[pl.BlockSpec((None, None, 1, cap), lambda ei, bi: (bi, ei, 0, 0), memory_space=pltpu.SMEM),
                  pl.BlockSpec((None, None, 1, cap), next_pair, memory_space=pltpu.SMEM),
                  pl.BlockSpec((None, None, cap, 1), lambda ei, bi: (bi, ei, 0, 0)),
                  pl.BlockSpec(memory_space=pl.ANY),
                  pl.BlockSpec(memory_space=pl.ANY),
                  pl.BlockSpec((None, d, f), lambda ei, bi: (ei, 0, 0)),
                  pl.BlockSpec((None, d, f), lambda ei, bi: (ei, 0, 0)),
                  pl.BlockSpec((None, f, d), lambda ei, bi: (ei, 0, 0))],
        out_specs=pl.BlockSpec(memory_space=pl.ANY),
        out_shape=jax.ShapeDtypeStruct((b, t, d), F32),
        scratch_shapes=[pltpu.VMEM((2, cap, d), F32), pltpu.VMEM((2, cap, d), F32),
                        pltpu.SemaphoreType.DMA((3, 2))],
        input_output_aliases={4: 0},
        compiler_params=_cparams("arbitrary", "arbitrary"),
        name="expert_ffn",
    )(idx4, idx4, gate.reshape(b, e, cap, 1), h2, acc0, w_gate, w_up, w_down)


def _final_kernel(x1_ref, acc_ref, g_ref, nrm_ref, o_ref):
    x = x1_ref[...] + g_ref[...] * acc_ref[...]
    o_ref[...] = x * lax.rsqrt(jnp.mean(x * x, axis=-1, keepdims=True) + NORM_EPS) * nrm_ref[...]


def _final(x1, acc, gate, norm_gain, bm):
    b, t, d = x1.shape
    def row(): return pl.BlockSpec((None, bm, d), lambda bi, i: (bi, i, 0))
    return pl.pallas_call(
        _final_kernel,
        grid=(b, t // bm),
        in_specs=[row(), row(), pl.BlockSpec((None, 1, d), lambda bi, i: (bi, 0, 0)),
                  pl.BlockSpec((1, d), lambda bi, i: (0, 0))],
        out_specs=row(),
        out_shape=jax.ShapeDtypeStruct((b, t, d), F32),
        compiler_params=_cparams("arbitrary", "arbitrary"),
        name="final_norm",
    )(x1, acc, gate, norm_gain.reshape(1, d))


def _row_block(t, largest):
    for bm in (1024, 512, 256, 128, 64):
        if bm <= largest and t % bm == 0:
            return bm
    raise ValueError(f"token count {t} is not a multiple of 64")


def kernel(x, c, ctx, c_ctx, ada_w, ada_b, norm_mix, norm_ffn, w_in, gdn_conv, gdn_a_log, gdn_dt_bias,
           hgrn_lb, hgrn_norm, gdn_norm, w_branch_a, w_branch_b, w_out, router_w, w_gate, w_up, w_down,
           final_norm):
    depth = ada_w.shape[0]
    assert depth == 1, "single-layer stack"
    b, t, d = x.shape
    tc = ctx.shape[1]
    n_experts = router_w.shape[-1]
    assert d % COL == 0 and (6 * COL) % d == 0 and t == CHUNK * GRID_W and tc % CHUNK == 0
    assert w_in.shape[-1] == 9 * COL + 4 * HEADS + 2 * d and n_experts <= LANES

    lower = jnp.cumsum(jax.nn.softmax(hgrn_lb.astype(F32), axis=1), axis=1)

    c_rows = jnp.zeros((SUBLANES, d), F32).at[:b].set(c).at[b].set(c_ctx)
    mods = _modulation(c_rows, ada_w[0], ada_b[0])
    ml = [m.reshape(b, 1, d) for m in jnp.split(mods[:b], 6, axis=-1)]
    mc = [m.reshape(1, 1, d) for m in jnp.split(mods[b:b + 1], 6, axis=-1)]

    wi = w_in[0]
    s_ab = 9 * COL
    s_gates = s_ab + 4 * HEADS
    w_main = jnp.concatenate([wi[:, :5 * COL], wi[:, 8 * COL:s_ab], wi[:, s_gates:]], axis=1).astype(BF16)
    w_qkv = wi[:, 5 * COL:8 * COL].astype(BF16)
    w_ab = jnp.pad(wi[:, s_ab:s_gates], ((0, 0), (0, LANES - 4 * HEADS))).astype(BF16)

    bm_lat, bm_ctx = _row_block(t, 1024), _row_block(tc, 1024)
    (p_lat,) = _input_projection(x, norm_mix[0], ml[0], ml[1], w_main, None, BF16, bm_lat)
    qkv_lat, ab_lat = _input_projection(x, norm_mix[0], ml[0], ml[1], w_qkv, w_ab, F32, bm_lat)
    (p_ctx,) = _input_projection(ctx, norm_mix[0], mc[0], mc[1], w_main[:, :4 * COL], None, BF16, bm_ctx)
    qkv_ctx, ab_ctx = _input_projection(ctx, norm_mix[0], mc[0], mc[1], w_qkv, w_ab, F32, bm_ctx)

    oa_f, oa_b = _hgrn_scans(p_ctx, p_lat, lower[0, 0], lower[1, 0])

    x_ctx = _gdn_prep_ctx(qkv_ctx, gdn_conv[0])
    x_lat, ab_lat_cm = _gdn_prep_lat(qkv_lat, ab_lat, gdn_conv[0])
    pad = LANES - 2 * HEADS
    alog_row = jnp.pad(gdn_a_log[0].reshape(1, 2 * HEADS).astype(F32), ((0, 0), (0, pad)))
    dtb_row = jnp.pad(gdn_dt_bias[0].reshape(1, 2 * HEADS).astype(F32), ((0, 0), (0, pad)))
    ob_f, ob_b = _gdn_scans(x_ctx, x_lat, ab_ctx.reshape(b, tc // CHUNK, CHUNK, LANES), ab_lat_cm,
                            alog_row, dtb_row)

    u = _merge(p_lat, oa_f, oa_b, ob_f, ob_b, hgrn_norm[0], gdn_norm[0],
               w_branch_a[0].astype(BF16), w_branch_b[0].astype(BF16))
    router_pad = jnp.pad(router_w[0].astype(F32), ((0, 0), (0, LANES - n_experts)))
    bm_row = _row_block(t, 256)
    x1, h2, aff = _outproj_router(u, w_out[0].astype(BF16), x, ml[2], norm_ffn[0], ml[3], ml[4],
                                  router_pad, n_experts, bm_row)

    cap = max(1, (CAPACITY_FACTOR * t) // n_experts)
    gate, idx = lax.top_k(jnp.swapaxes(aff[..., :n_experts], 1, 2), cap)
    acc = _expert_ffn(h2, idx.astype(jnp.int32), gate, w_gate[0].astype(BF16), w_up[0].astype(BF16),
                      w_down[0].astype(BF16))
    return _final(x1, acc, ml[5], final_norm, bm_row)
```

```python
import functools

import jax
import jax.numpy as jnp
from jax import lax
from jax.experimental import pallas as pl
from jax.experimental.pallas import tpu as pltpu

F32 = jnp.float32
BF16 = jnp.bfloat16

NORM_EPS = 1e-6
LOG2E = 1.4426950408889634
GRID_W = 64
HEADS = 8
HEAD_DIM = 128
CHUNK = 64
CONV_K = 5
CAPACITY_FACTOR = 2
COL = HEADS * HEAD_DIM
LANES = 128
SUBLANES = 8
GROUP = SUBLANES
INV_BLOCK = 8
VMEM_LIMIT_BYTES = 56 * 1024 * 1024


def _cparams(*sem):
    return pltpu.CompilerParams(dimension_semantics=sem, vmem_limit_bytes=VMEM_LIMIT_BYTES)


def _dot(a, b):
    return jnp.dot(a, b, preferred_element_type=F32)


def _dot_nt(a, b):
    return lax.dot_general(a, b, (((1,), (1,)), ((), ())), preferred_element_type=F32)


def _silu(x):
    return x * jax.nn.sigmoid(x)


def _split3(x):
    hi = x.astype(BF16)
    r = x - hi.astype(F32)
    mid = r.astype(BF16)
    return hi, mid, (r - mid.astype(F32)).astype(BF16)


def _scan_cumsum(g, rev):
    tri = jnp.where(_tri(rev), 1.0, 0.0).astype(BF16)
    hi, mid, lo = _split3(g)
    return _dot(tri, hi) + _dot(tri, mid) + _dot(tri, lo)


def _resident(shape, index_map):
    return pl.BlockSpec(shape, index_map, pipeline_mode=pl.Buffered(1))


def _mod_kernel(c_ref, w_ref, b_ref, o_ref):
    o_ref[...] = _dot(_silu(c_ref[...]), w_ref[...]) + b_ref[...]


def _modulation(c_rows, ada_w, ada_b):
    rows, d = c_rows.shape
    n = ada_w.shape[1]
    bn = 1024
    return pl.pallas_call(
        _mod_kernel,
        grid=(n // bn,),
        in_specs=[pl.BlockSpec((rows, d), lambda j: (0, 0)),
                  pl.BlockSpec((d, bn), lambda j: (0, j)),
                  pl.BlockSpec((1, bn), lambda j: (0, j))],
        out_specs=pl.BlockSpec((rows, bn), lambda j: (0, j)),
        out_shape=jax.ShapeDtypeStruct((rows, n), F32),
        compiler_params=_cparams("arbitrary"),
        name="modulation",
    )(c_rows, ada_w, ada_b.reshape(1, n))


def _proj_kernel(x_ref, gain_ref, shift_ref, scale_ref, w_ref, *rest, with_ab):
    if with_ab:
        wab_ref, o_ref, oab_ref, h_ref = rest
    else:
        o_ref, h_ref = rest

    @pl.when(pl.program_id(2) == 0)
    def _():
        x = x_ref[...]
        y = x * lax.rsqrt(jnp.mean(x * x, axis=-1, keepdims=True) + NORM_EPS) * gain_ref[...]
        h = (y * (1.0 + scale_ref[...]) + shift_ref[...]).astype(BF16)
        h_ref[...] = h
        if with_ab:
            oab_ref[...] = _dot(h, wab_ref[...])

    o_ref[...] = _dot(h_ref[...], w_ref[...]).astype(o_ref.dtype)


def _input_projection(x, gain, shift, scale, w, w_ab, out_dtype, bm):
    b, t, d = x.shape
    n = w.shape[1]
    per_batch = shift.shape[0] > 1
    mod_map = (lambda bi, i, j: (bi, 0, 0)) if per_batch else (lambda bi, i, j: (0, 0, 0))
    with_ab = w_ab is not None
    in_specs = [pl.BlockSpec((None, bm, d), lambda bi, i, j: (bi, i, 0)),
                pl.BlockSpec((1, d), lambda bi, i, j: (0, 0)),
                pl.BlockSpec((None, 1, d), mod_map),
                pl.BlockSpec((None, 1, d), mod_map),
                pl.BlockSpec((d, COL), lambda bi, i, j: (0, j))]
    out_specs = [pl.BlockSpec((None, bm, COL), lambda bi, i, j: (bi, i, j))]
    out_shape = [jax.ShapeDtypeStruct((b, t, n), out_dtype)]
    args = [x, gain.reshape(1, d), shift, scale, w]
    if with_ab:
        in_specs.append(pl.BlockSpec((d, LANES), lambda bi, i, j: (0, 0)))
        out_specs.append(pl.BlockSpec((None, bm, LANES), lambda bi, i, j: (bi, i, 0)))
        out_shape.append(jax.ShapeDtypeStruct((b, t, LANES), F32))
        args.append(w_ab)
    return pl.pallas_call(
        functools.partial(_proj_kernel, with_ab=with_ab),
        grid=(b, t // bm, n // COL),
        in_specs=in_specs, out_specs=out_specs, out_shape=out_shape,
        scratch_shapes=[pltpu.VMEM((bm, d), BF16)],
        compiler_params=_cparams("arbitrary", "arbitrary", "arbitrary"),
        name="input_projection",
    )(*args)


def _tri(rev, strict=False):
    ri = lax.broadcasted_iota(jnp.int32, (CHUNK, CHUNK), 0)
    ci = lax.broadcasted_iota(jnp.int32, (CHUNK, CHUNK), 1)
    if rev:
        return (ci > ri) if strict else (ci >= ri)
    return (ci < ri) if strict else (ci <= ri)


def _boundary_rows(g, bs, rev):
    n, w = g.shape
    off = (bs >> 1) - 1 + (1 if rev else 0)
    if bs >= SUBLANES:
        pieces = [jnp.broadcast_to(g[p * bs + off:p * bs + off + 1], (bs, w)) for p in range(n // bs)]
        return jnp.concatenate(pieces, axis=0)
    sub = lax.broadcasted_iota(jnp.int32, (SUBLANES, w), 0)
    pieces = []
    for r0 in range(0, n, SUBLANES):
        acc = None
        for p in range(SUBLANES // bs):
            r = r0 + p * bs + off
            val = jnp.broadcast_to(g[r:r + 1], (SUBLANES, w))
            acc = val if acc is None else jnp.where(sub >= p * bs, val, acc)
        pieces.append(acc)
    return jnp.concatenate(pieces, axis=0)


def _hgrn_chunks(scans):
    row = lax.broadcasted_iota(jnp.int32, (CHUNK, 1), 0)
    ri = lax.broadcasted_iota(jnp.int32, (CHUNK, CHUNK), 0)
    ci = lax.broadcasted_iota(jnp.int32, (CHUNK, CHUNK), 1)
    n_levels = CHUNK.bit_length() - 1
    eye = ri == ci

    units = []
    for q_raw, f_raw, v, low, s_ref, rev in scans:
        q = _silu(q_raw.astype(F32))
        f_raw = f_raw.astype(F32)
        one_m = 1.0 - low
        f = low + one_m * jax.nn.sigmoid(f_raw)
        k = one_m * jax.nn.sigmoid(-f_raw)
        gc = _scan_cumsum(jnp.log(f), rev) * LOG2E
        g_last = gc[0:1] if rev else gc[CHUNK - 1:CHUNK]
        q_dec = (q * jnp.exp2(gc)).astype(BF16)
        k_dec = (k * jnp.exp2(g_last - gc)).astype(BF16)
        mixed, pair = [], []
        for lv in range(1, n_levels + 1):
            upper = ((row >> (lv - 1)) & 1) == 1
            q_rows = jnp.logical_not(upper) if rev else upper
            decay = jnp.exp2(-jnp.abs(gc - _boundary_rows(gc, 1 << lv, rev)))
            mixed.append((jnp.where(q_rows, q, k) * decay).astype(BF16))
            i_upper = ((ri >> (lv - 1)) & 1) == 1
            j_upper = ((ci >> (lv - 1)) & 1) == 1
            halves = jnp.logical_and(j_upper, jnp.logical_not(i_upper)) if rev else \
                jnp.logical_and(i_upper, jnp.logical_not(j_upper))
            pair.append(jnp.logical_and((ri >> lv) == (ci >> lv), halves))
        qk = q * k
        v_b = v.astype(BF16)
        v_f = v.astype(F32)
        for h in range(HEADS):
            sl = slice(h * HEAD_DIM, (h + 1) * HEAD_DIM)
            units.append(dict(s_ref=s_ref, h=h, diag=jnp.sum(qk[:, sl], axis=1, keepdims=True),
                              mixed=[x[:, sl] for x in mixed], pair=pair,
                              q_dec=q_dec[:, sl], k_dec=k_dec[:, sl], v_b=v_b[:, sl], v_f=v_f[:, sl],
                              decay=jnp.exp2(g_last[:, sl])))

    scores = [[_dot_nt(m, m) for m in u["mixed"]] for u in units]
    states = [u["s_ref"][u["h"]] for u in units]
    carried = [_dot_nt(u["q_dec"], s.astype(BF16)) for u, s in zip(units, states)]
    outs = []
    for u, sc, car in zip(units, scores, carried):
        a = jnp.where(eye, u["diag"], 0.0)
        for lv in range(n_levels):
            a = jnp.where(u["pair"][lv], sc[lv], a)
        outs.append(_dot(a.astype(BF16), u["v_b"]) + car)
    for u, s in zip(units, states):
        u["s_ref"][u["h"]] = u["decay"] * s + _dot(u["v_f"].T.astype(BF16), u["k_dec"])
    return [jnp.concatenate(outs[i * HEADS:(i + 1) * HEADS], axis=1) for i in range(len(scans))]


def _hgrn_kernel(lowf_ref, lowb_ref,
                 cqf, cff, cvf, cqb, cfb, cvb,
                 lqf, lff, lvf, lqb, lfb, lvb,
                 of_ref, ob_ref, sf_ref, sb_ref, *, n_ctx):
    t = pl.program_id(1)

    @pl.when(t == 0)
    def _():
        sf_ref[...] = jnp.zeros_like(sf_ref)
        sb_ref[...] = jnp.zeros_like(sb_ref)

    is_ctx = t < n_ctx

    def pick(c_ref, l_ref):
        return jnp.where(is_ctx, c_ref[...], l_ref[...])

    o_f, o_b = _hgrn_chunks([(pick(cqf, lqf), pick(cff, lff), pick(cvf, lvf), lowf_ref[...], sf_ref, False),
                             (pick(cqb, lqb), pick(cfb, lfb), pick(cvb, lvb), lowb_ref[...], sb_ref, True)])
    of_ref[...] = o_f
    ob_ref[...] = o_b


def _hgrn_scans(p_ctx, p_lat, low_f, low_b):
    b, tc, _ = p_ctx.shape
    t = p_lat.shape[1]
    n_ctx, n_lat = tc // CHUNK, t // CHUNK

    def cf(ti): return jnp.minimum(ti, n_ctx - 1)
    def cb(ti): return jnp.maximum(n_ctx - 1 - ti, 0)
    def lf(ti): return jnp.maximum(ti - n_ctx, 0)
    def lb(ti): return n_lat - 1 - jnp.maximum(ti - n_ctx, 0)

    def spec(chunk_of, col):
        return pl.BlockSpec((None, CHUNK, COL), lambda bi, ti: (bi, chunk_of(ti), col))

    low_spec = pl.BlockSpec((1, COL), lambda bi, ti: (0, 0))
    out_f, out_b = pl.pallas_call(
        functools.partial(_hgrn_kernel, n_ctx=n_ctx),
        grid=(b, n_ctx + n_lat),
        in_specs=[low_spec, low_spec,
                  spec(cf, 0), spec(cf, 1), spec(cf, 3), spec(cb, 0), spec(cb, 2), spec(cb, 3),
                  spec(lf, 0), spec(lf, 1), spec(lf, 3), spec(lb, 0), spec(lb, 2), spec(lb, 3)],
        out_specs=[pl.BlockSpec((None, CHUNK, COL), lambda bi, ti: (bi, lf(ti), 0)),
                   pl.BlockSpec((None, CHUNK, COL), lambda bi, ti: (bi, lb(ti), 0))],
        out_shape=[jax.ShapeDtypeStruct((b, t, COL), F32)] * 2,
        scratch_shapes=[pltpu.VMEM((HEADS, HEAD_DIM, HEAD_DIM), F32)] * 2,
        compiler_params=_cparams("arbitrary", "arbitrary"),
        name="hgrn_scans",
    )(low_f.reshape(1, COL), low_b.reshape(1, COL), *([p_ctx] * 6), *([p_lat] * 6))
    return out_f, out_b


def _conv_silu_norm(x, prev, nxt, w, idx):
    row = lax.broadcasted_iota(jnp.int32, (CHUNK, 1), 0)
    xm1 = jnp.where(row == 0, prev[7:8], pltpu.roll(x, 1, 0))
    xm2 = jnp.where(row == 0, prev[6:7], jnp.where(row == 1, prev[7:8], pltpu.roll(x, 2, 0)))
    xp1 = jnp.where(row == CHUNK - 1, nxt[0:1], pltpu.roll(x, CHUNK - 1, 0))
    xp2 = jnp.where(row == CHUNK - 1, nxt[1:2],
                    jnp.where(row == CHUNK - 2, nxt[0:1], pltpu.roll(x, CHUNK - 2, 0)))
    y = _silu(w[0:1] * xm2 + w[1:2] * xm1 + w[2:3] * x + w[3:4] * xp1 + w[4:5] * xp2)
    if idx == 2:
        return y
    scale = HEAD_DIM ** -0.5 if idx == 0 else 1.0
    pieces = []
    for h in range(HEADS):
        yh = y[:, h * HEAD_DIM:(h + 1) * HEAD_DIM]
        pieces.append(yh * (lax.rsqrt(jnp.sum(yh * yh, axis=1, keepdims=True) + NORM_EPS) * scale))
    return jnp.concatenate(pieces, axis=1)


def _gdn_prep_ctx_kernel(q_ref, k_ref, v_ref, pq_ref, pk_ref, pv_ref, nq_ref, nk_ref, nv_ref, w_ref, o_ref):
    t = pl.program_id(1)
    first = t == 0
    last = t == pl.num_programs(1) - 1
    for idx, (c_ref, p_ref, n_ref) in enumerate(((q_ref, pq_ref, nq_ref), (k_ref, pk_ref, nk_ref),
                                                 (v_ref, pv_ref, nv_ref))):
        prev = jnp.where(first, 0.0, p_ref[...])
        nxt = jnp.where(last, 0.0, n_ref[...])
        y = _conv_silu_norm(c_ref[...], prev, nxt, w_ref[:, idx * COL:(idx + 1) * COL], idx)
        o_ref[:, idx * COL:(idx + 1) * COL] = y.astype(o_ref.dtype)


def _gdn_prep_ctx(qkv, conv_w):
    b, t, _ = qkv.shape
    n_chunks = t // CHUNK
    rb = CHUNK // SUBLANES
    def cur(col): return pl.BlockSpec((None, CHUNK, COL), lambda bi, c: (bi, c, col))
    def prv(col): return pl.BlockSpec((None, SUBLANES, COL), lambda bi, c: (bi, jnp.maximum(c * rb - 1, 0), col))
    def nxt(col): return pl.BlockSpec((None, SUBLANES, COL),
                                      lambda bi, c: (bi, jnp.minimum((c + 1) * rb, n_chunks * rb - 1), col))
    cols = (0, 1, 2)
    return pl.pallas_call(
        _gdn_prep_ctx_kernel,
        grid=(b, n_chunks),
        in_specs=[cur(c) for c in cols] + [prv(c) for c in cols] + [nxt(c) for c in cols]
                 + [pl.BlockSpec((CONV_K, 3 * COL), lambda bi, c: (0, 0))],
        out_specs=pl.BlockSpec((None, None, CHUNK, 3 * COL), lambda bi, c: (bi, c, 0, 0)),
        out_shape=jax.ShapeDtypeStruct((b, n_chunks, CHUNK, 3 * COL), BF16),
        compiler_params=_cparams("arbitrary", "arbitrary"),
        name="gdn_prep_ctx",
    )(*([qkv] * 9), conv_w)


def _gdn_prep_lat_kernel(q_ref, k_ref, v_ref, pq_ref, pk_ref, pv_ref, nq_ref, nk_ref, nv_ref, ab_ref, w_ref,
                         o_ref, oab_ref):
    g = pl.program_id(1)
    first = g == 0
    last = g == pl.num_programs(1) - 1
    tail = CHUNK - SUBLANES
    for idx, (c_ref, p_ref, n_ref) in enumerate(((q_ref, pq_ref, nq_ref), (k_ref, pk_ref, nk_ref),
                                                 (v_ref, pv_ref, nv_ref))):
        w = w_ref[:, idx * COL:(idx + 1) * COL]
        cols = [c_ref[:, j, :] for j in range(GROUP)]
        before = jnp.where(first, 0.0, p_ref[:, GROUP - 1, :])
        after = jnp.where(last, 0.0, n_ref[:, 0, :])
        for j in range(GROUP):
            prev = cols[j - 1][tail:] if j > 0 else before
            nxt = cols[j + 1][:SUBLANES] if j < GROUP - 1 else after
            o_ref[j, :, idx * COL:(idx + 1) * COL] = _conv_silu_norm(cols[j], prev, nxt, w, idx).astype(o_ref.dtype)
    for j in range(GROUP):
        oab_ref[j] = ab_ref[:, j, :]


def _gdn_prep_lat(qkv, ab, conv_w):
    b, t, _ = qkv.shape
    assert t == CHUNK * GRID_W and GRID_W % GROUP == 0
    n_groups = GRID_W // GROUP
    rb = CHUNK // SUBLANES
    q4 = qkv.reshape(b, CHUNK, GRID_W, 3 * COL)
    ab4 = ab.reshape(b, CHUNK, GRID_W, LANES)
    def cur(col): return pl.BlockSpec((None, CHUNK, GROUP, COL), lambda bi, g: (bi, 0, g, col))
    def prv(col): return pl.BlockSpec((None, SUBLANES, GROUP, COL),
                                      lambda bi, g: (bi, rb - 1, jnp.maximum(g - 1, 0), col))
    def nxt(col): return pl.BlockSpec((None, SUBLANES, GROUP, COL),
                                      lambda bi, g: (bi, 0, jnp.minimum(g + 1, n_groups - 1), col))
    cols = (0, 1, 2)
    return pl.pallas_call(
        _gdn_prep_lat_kernel,
        grid=(b, n_groups),
        in_specs=[cur(c) for c in cols] + [prv(c) for c in cols] + [nxt(c) for c in cols]
                 + [pl.BlockSpec((None, CHUNK, GROUP, LANES), lambda bi, g: (bi, 0, g, 0)),
                    pl.BlockSpec((CONV_K, 3 * COL), lambda bi, g: (0, 0))],
        out_specs=[pl.BlockSpec((None, GROUP, CHUNK, 3 * COL), lambda bi, g: (bi, g, 0, 0)),
                   pl.BlockSpec((None, GROUP, CHUNK, LANES), lambda bi, g: (bi, g, 0, 0))],
        out_shape=[jax.ShapeDtypeStruct((b, GRID_W, CHUNK, 3 * COL), BF16),
                   jax.ShapeDtypeStruct((b, GRID_W, CHUNK, LANES), F32)],
        compiler_params=_cparams("arbitrary", "arbitrary"),
        name="gdn_prep_lat",
    )(*([q4] * 9), ab4, conv_w)


def _mm_bf16(a, b):
    return _dot(a.astype(BF16), b.astype(BF16))


def _unit_tri_inverses(ns):
    ri = lax.broadcasted_iota(jnp.int32, (CHUNK, CHUNK), 0)
    ci = lax.broadcasted_iota(jnp.int32, (CHUNK, CHUNK), 1)
    eye = jnp.where(ri == ci, 1.0, 0.0)
    shift = INV_BLOCK.bit_length() - 1
    on_diag = (ri >> shift) == (ci >> shift)

    def nilpotent_inverses(ms, degree):
        xs = [eye - m for m in ms]
        ps = ms
        for _ in range(degree.bit_length() - 2):
            ps = [_mm_bf16(p, p) for p in ps]
            xs = [x + _mm_bf16(x, p) for x, p in zip(xs, ps)]
        return xs

    d_invs = nilpotent_inverses([jnp.where(on_diag, n, 0.0) for n in ns], INV_BLOCK)
    ms = [_mm_bf16(d, jnp.where(on_diag, 0.0, n)) for d, n in zip(d_invs, ns)]
    ys = nilpotent_inverses(ms, CHUNK // INV_BLOCK)
    return [_mm_bf16(y, d) for y, d in zip(ys, d_invs)]


def _gdn_chunks(scans, alog, dtb):
    units = []
    for x, ab, s_ref, rev in scans:
        g_all = -jnp.exp(alog) * jax.nn.softplus(ab + dtb)
        beta_all = jax.nn.sigmoid(ab)
        gc_all = _scan_cumsum(g_all, rev)
        gr_all = gc_all.T
        incl = _tri(rev)
        strict = _tri(rev, strict=True)
        off = HEADS if rev else 0
        for h in range(HEADS):
            gcol = gc_all[:, off + h:off + h + 1]
            grow = gr_all[off + h:off + h + 1, :]
            units.append(dict(
                s_ref=s_ref, h=h, strict=strict, gcol=gcol,
                beta=beta_all[:, 2 * HEADS + off + h:2 * HEADS + off + h + 1],
                g_last=gcol[0:1] if rev else gcol[CHUNK - 1:CHUNK],
                gamma=jnp.where(incl, jnp.exp(jnp.where(incl, gcol - grow, 0.0)), 0.0),
                q=x[:, h * HEAD_DIM:(h + 1) * HEAD_DIM],
                k=x[:, COL + h * HEAD_DIM:COL + (h + 1) * HEAD_DIM],
                v=x[:, 2 * COL + h * HEAD_DIM:2 * COL + (h + 1) * HEAD_DIM]))

    kks = [_dot_nt(u["k"], u["k"]) for u in units]
    qks = [_dot_nt(u["q"], u["k"]) for u in units]
    ns = [jnp.where(u["strict"], u["beta"] * kk * u["gamma"], 0.0) for u, kk in zip(units, kks)]
    invs = _unit_tri_inverses(ns)
    rhss = [jnp.concatenate([u["v"].astype(F32) * u["beta"],
                             u["k"].astype(F32) * (u["beta"] * jnp.exp(u["gcol"]))], axis=1) for u in units]
    sols = [_mm_bf16(a, r) for a, r in zip(invs, rhss)]
    states = [u["s_ref"][u["h"]] for u in units]
    states_b = [s.astype(BF16) for s in states]
    v_news = [sol[:, :HEAD_DIM] - _dot_nt(sol[:, HEAD_DIM:].astype(BF16), s_b) for sol, s_b in zip(sols, states_b)]
    outs = [_dot_nt((u["q"].astype(F32) * jnp.exp(u["gcol"])).astype(BF16), s_b)
            + _mm_bf16(qk * u["gamma"], v_new)
            for u, qk, s_b, v_new in zip(units, qks, states_b, v_news)]
    for u, s, v_new in zip(units, states, v_news):
        k_dec = (u["k"].astype(F32) * jnp.exp(u["g_last"] - u["gcol"])).astype(BF16)
        u["s_ref"][u["h"]] = jnp.exp(u["g_last"]) * s + _dot(v_new.T.astype(BF16), k_dec)
    return [jnp.concatenate(outs[i * HEADS:(i + 1) * HEADS], axis=1) for i in range(len(scans))]


def _gdn_ctx_kernel(alog_ref, dtb_ref, x_ref, ab_ref, sf_ref, sb_ref, *, n_chunks):
    sf_ref[...] = jnp.zeros_like(sf_ref)
    sb_ref[...] = jnp.zeros_like(sb_ref)

    def body(j, carry):
        jb = n_chunks - 1 - j
        _gdn_chunks([(x_ref[j], ab_ref[j], sf_ref, False), (x_ref[jb], ab_ref[jb], sb_ref, True)],
                    alog_ref[...], dtb_ref[...])
        return carry

    lax.fori_loop(0, n_chunks, body, 0)


def _gdn_lat_kernel(alog_ref, dtb_ref, s0f_ref, s0b_ref, xf_ref, abf_ref, xb_ref, abb_ref,
                    of_ref, ob_ref, sf_ref, sb_ref):
    @pl.when(pl.program_id(1) == 0)
    def _():
        sf_ref[...] = s0f_ref[...]
        sb_ref[...] = s0b_ref[...]

    def body(j, carry):
        jb = GROUP - 1 - j
        o_f, o_b = _gdn_chunks([(xf_ref[j], abf_ref[j], sf_ref, False), (xb_ref[jb], abb_ref[jb], sb_ref, True)],
                               alog_ref[...], dtb_ref[...])
        of_ref[j] = o_f
        ob_ref[jb] = o_b
        return carry

    lax.fori_loop(0, GROUP, body, 0)


def _gdn_scans(x_ctx, x_lat, ab_ctx, ab_lat, alog_row, dtb_row):
    b, n_ctx = x_ctx.shape[:2]
    n_lat = x_lat.shape[1]
    n_groups = n_lat // GROUP
    state_shape = jax.ShapeDtypeStruct((b, HEADS, HEAD_DIM, HEAD_DIM), F32)
    def state_spec(nidx):
        return pl.BlockSpec((None, HEADS, HEAD_DIM, HEAD_DIM), lambda *i: (i[0], 0, 0, 0))
    s_f, s_b = pl.pallas_call(
        functools.partial(_gdn_ctx_kernel, n_chunks=n_ctx),
        grid=(b,),
        in_specs=[pl.BlockSpec((1, LANES), lambda bi: (0, 0)), pl.BlockSpec((1, LANES), lambda bi: (0, 0)),
                  pl.BlockSpec((None, n_ctx, CHUNK, 3 * COL), lambda bi: (bi, 0, 0, 0)),
                  pl.BlockSpec((None, n_ctx, CHUNK, LANES), lambda bi: (bi, 0, 0, 0))],
        out_specs=[state_spec(1), state_spec(1)],
        out_shape=[state_shape, state_shape],
        compiler_params=_cparams("arbitrary"),
        name="gdn_ctx_scans",
    )(alog_row, dtb_row, x_ctx, ab_ctx)

    def gf(ti): return ti
    def gb(ti): return n_groups - 1 - ti
    def xspec(group_of):
        return pl.BlockSpec((None, GROUP, CHUNK, 3 * COL), lambda bi, ti: (bi, group_of(ti), 0, 0))
    def abspec(group_of):
        return pl.BlockSpec((None, GROUP, CHUNK, LANES), lambda bi, ti: (bi, group_of(ti), 0, 0))
    def ospec(group_of):
        return pl.BlockSpec((None, GROUP, CHUNK, COL), lambda bi, ti: (bi, group_of(ti), 0, 0))
    row_spec = pl.BlockSpec((1, LANES), lambda bi, ti: (0, 0))
    return pl.pallas_call(
        _gdn_lat_kernel,
        grid=(b, n_groups),
        in_specs=[row_spec, row_spec, state_spec(2), state_spec(2),
                  xspec(gf), abspec(gf), xspec(gb), abspec(gb)],
        out_specs=[ospec(gf), ospec(gb)],
        out_shape=[jax.ShapeDtypeStruct((b, n_lat, CHUNK, COL), F32)] * 2,
        scratch_shapes=[pltpu.VMEM((HEADS, HEAD_DIM, HEAD_DIM), F32)] * 2,
        compiler_params=_cparams("arbitrary", "arbitrary"),
        name="gdn_lat_scans",
    )(alog_row, dtb_row, s_f, s_b, x_lat, ab_lat, x_lat, ab_lat)


def _head_norm(x):
    pieces = []
    for h in range(HEADS):
        xh = x[:, h * HEAD_DIM:(h + 1) * HEAD_DIM]
        pieces.append(xh * lax.rsqrt(jnp.mean(xh * xh, axis=1, keepdims=True) + NORM_EPS))
    return jnp.concatenate(pieces, axis=1)


def _merge_kernel(oaf, oab, og, obf, obb, z, ga, gb, hag, gbg, wba, wbb, u_ref):
    y_a = _head_norm((oaf[...] + oab[...]) * jax.nn.sigmoid(og[...].astype(F32))) * hag[...]
    o_b = jnp.concatenate([obf[:, r, :] + obb[:, r, :] for r in range(obf.shape[1])], axis=0)
    y_b = _head_norm(o_b) * gbg[...] * _silu(z[...].astype(F32))
    u = jax.nn.sigmoid(ga[...].astype(F32)) * _dot(y_a.astype(BF16), wba[...])
    u = u + jax.nn.sigmoid(gb[...].astype(F32)) * _dot(y_b.astype(BF16), wbb[...])
    u_ref[...] = u.astype(u_ref.dtype)


def _merge(p_lat, oa_f, oa_b, ob_f, ob_b, ha_gain, gb_gain, w_ba, w_bb):
    b, t, _ = p_lat.shape
    d = w_ba.shape[1]
    bm = SUBLANES * GRID_W
    g0 = 6 * COL // d
    def o_spec(): return pl.BlockSpec((None, bm, COL), lambda bi, i: (bi, i, 0))
    def ob_spec(): return pl.BlockSpec((None, GRID_W, SUBLANES, COL), lambda bi, i: (bi, 0, i, 0))
    def p_spec(col): return pl.BlockSpec((None, bm, COL), lambda bi, i: (bi, i, col))
    def gate_spec(k): return pl.BlockSpec((None, bm, d), lambda bi, i: (bi, i, g0 + k))
    def full(shape): return _resident(shape, lambda bi, i: (0,) * len(shape))
    return pl.pallas_call(
        _merge_kernel,
        grid=(b, t // bm),
        in_specs=[o_spec(), o_spec(), p_spec(4), ob_spec(), ob_spec(), p_spec(5),
                  gate_spec(0), gate_spec(1), full((1, COL)), full((1, COL)),
                  full((COL, d)), full((COL, d))],
        out_specs=pl.BlockSpec((None, bm, d), lambda bi, i: (bi, i, 0)),
        out_shape=jax.ShapeDtypeStruct((b, t, d), BF16),
        compiler_params=_cparams("arbitrary", "arbitrary"),
        name="merge",
    )(oa_f, oa_b, p_lat, ob_f, ob_b, p_lat, p_lat, p_lat,
      ha_gain.reshape(1, COL), gb_gain.reshape(1, COL), w_ba, w_bb)


def _outproj_kernel(u_ref, wout_ref, x_ref, g_ref, nrm_ref, sh_ref, sc_ref, rwh_ref, rwl_ref,
                    x1_ref, h2_ref, afft_ref, acc0_ref, *, n_experts):
    acc0_ref[...] = jnp.zeros_like(acc0_ref)
    x1 = x_ref[...] + g_ref[...] * _dot(u_ref[...], wout_ref[...])
    x1_ref[...] = x1
    y = x1 * lax.rsqrt(jnp.mean(x1 * x1, axis=-1, keepdims=True) + NORM_EPS) * nrm_ref[...]
    h2 = y * (1.0 + sc_ref[...]) + sh_ref[...]
    h2_ref[...] = h2
    h_hi = h2.astype(BF16)
    h_lo = (h2 - h_hi.astype(F32)).astype(BF16)
    logits = _dot(h_hi, rwh_ref[...]) + _dot(h_hi, rwl_ref[...]) + _dot(h_lo, rwh_ref[...])
    lane = lax.broadcasted_iota(jnp.int32, logits.shape, 1)
    logits = jnp.where(lane < n_experts, logits, -jnp.inf)
    e = jnp.exp(logits - jnp.max(logits, axis=-1, keepdims=True))
    aff = e / jnp.sum(e, axis=-1, keepdims=True)
    afft_ref[...] = aff.T[:afft_ref.shape[0]]


def _outproj_router(u, w_out, x, gate, norm_gain, shift, scale, router_pad, n_experts, bm):
    b, t, d = x.shape
    ep = -(-n_experts // SUBLANES) * SUBLANES
    router_hi = router_pad.astype(BF16)
    router_lo = (router_pad - router_hi.astype(F32)).astype(BF16)
    def row(): return pl.BlockSpec((None, bm, d), lambda bi, i: (bi, i, 0))
    def mod(): return pl.BlockSpec((None, 1, d), lambda bi, i: (bi, 0, 0))
    return pl.pallas_call(
        functools.partial(_outproj_kernel, n_experts=n_experts),
        grid=(b, t // bm),
        in_specs=[row(), _resident((d, d), lambda bi, i: (0, 0)), row(), mod(),
                  pl.BlockSpec((1, d), lambda bi, i: (0, 0)), mod(), mod(),
                  _resident((d, LANES), lambda bi, i: (0, 0)), _resident((d, LANES), lambda bi, i: (0, 0))],
        out_specs=[row(), row(), pl.BlockSpec((None, ep, bm), lambda bi, i: (bi, 0, i)), row()],
        out_shape=[jax.ShapeDtypeStruct((b, t, d), F32), jax.ShapeDtypeStruct((b, t, d), F32),
                   jax.ShapeDtypeStruct((b, ep, t), F32), jax.ShapeDtypeStruct((b, t, d), F32)],
        compiler_params=_cparams("arbitrary", "arbitrary"),
        name="outproj_router",
    )(u, w_out, x, gate, norm_gain.reshape(1, d), shift, scale, router_hi, router_lo)


def _route_kernel(aff_ref, lst_ref, gate_ref, *, cap):
    bits = pltpu.bitcast(aff_ref[...], jnp.int32)
    rows, n_blk, _ = bits.shape
    lane = lax.broadcasted_iota(jnp.int32, bits.shape, 2)
    blk = lax.broadcasted_iota(jnp.int32, bits.shape, 1)

    def count(mask):
        return jnp.sum(mask.astype(jnp.int32), axis=(1, 2), keepdims=True)

    def refine(i, lo):
        cand = lo | lax.shift_left(jnp.int32(1), 30 - i)
        return jnp.where(count(bits >= cand) >= cap, cand, lo)

    cut = lax.fori_loop(0, 31, refine, jnp.zeros((rows, 1, 1), jnp.int32))

    def block_scan(m):
        x = m
        for k in range(LANES.bit_length() - 1):
            s = 1 << k
            x = x + jnp.where(lane >= s, pltpu.roll(x, s, 2), 0)
        return x, jnp.broadcast_to(x[:, :, LANES - 1:LANES], x.shape)

    def exclusive_prefix(m):
        incl, tot = block_scan(m)
        y = tot
        k = 0
        while (1 << k) < n_blk:
            s = 1 << k
            y = y + jnp.where(blk >= s, pltpu.roll(y, s, 1), 0)
            k += 1
        return incl - m + (y - tot)

    above = bits > cut
    at_cut = bits == cut
    need = cap - count(above)
    chosen = jnp.logical_or(above, jnp.logical_and(at_cut, exclusive_prefix(at_cut.astype(jnp.int32)) < need))
    token = blk * LANES + lane
    n_tok = n_blk * LANES

    def shifted(x, s):
        if s < LANES:
            y = pltpu.roll(x, LANES - s, 2)
            y = jnp.where(lane < LANES - s, y, pltpu.roll(y, n_blk - 1, 1)) if n_blk > 1 else y
        else:
            y = pltpu.roll(x, n_blk - s // LANES, 1)
        return jnp.where(token < n_tok - s, y, 0)

    m = chosen.astype(jnp.int32)
    val = jnp.where(chosen, token, 0)
    aff = jnp.where(chosen, bits, 0)
    dist = jnp.where(chosen, token - exclusive_prefix(m), 0)
    valid = m
    for k in range((n_tok - 1).bit_length()):
        s = 1 << k
        moving = valid * ((dist >> k) & 1)
        arrive = shifted(moving, s) == 1
        stay = (valid - moving) == 1
        val = jnp.where(arrive, shifted(val, s), jnp.where(stay, val, 0))
        aff = jnp.where(arrive, shifted(aff, s), jnp.where(stay, aff, 0))
        dist = jnp.where(arrive, shifted(dist, s), jnp.where(stay, dist, 0))
        valid = jnp.logical_or(arrive, stay).astype(jnp.int32)
    lst_ref[...] = val[:, :lst_ref.shape[1], :]
    gate_ref[...] = pltpu.bitcast(aff[:, :gate_ref.shape[1], :], F32)


def _route(aff_t, cap):
    r, t = aff_t.shape
    assert r % SUBLANES == 0 and t % LANES == 0 and cap % LANES == 0
    n_blk = t // LANES
    out_spec = pl.BlockSpec((SUBLANES, cap // LANES, LANES), lambda i: (i, 0, 0))
    return pl.pallas_call(
        functools.partial(_route_kernel, cap=cap),
        grid=(r // SUBLANES,),
        in_specs=[pl.BlockSpec((SUBLANES, n_blk, LANES), lambda i: (i, 0, 0))],
        out_specs=[out_spec, out_spec],
        out_shape=[jax.ShapeDtypeStruct((r, cap // LANES, LANES), jnp.int32),
                   jax.ShapeDtypeStruct((r, cap // LANES, LANES), F32)],
        compiler_params=_cparams("arbitrary"),
        name="route",
    )(aff_t.reshape(r, n_blk, LANES))


def _ffn_kernel(lst_ref, lst_next_ref, gate_ref, h_hbm, acc_in_hbm, wg_ref, wu_ref, wd_ref,
                acc_hbm, x_buf, a_buf, sem, *, cap, half):
    n_batch = pl.num_programs(1)
    b = pl.program_id(1)
    step = pl.program_id(0) * n_batch + b
    n_steps = pl.num_programs(0) * n_batch
    slot = step % 2
    other = 1 - slot
    b_next = jnp.where(b + 1 == n_batch, 0, b + 1)
    def for_chosen(lst, fn):
        def body(s8, carry):
            for r in range(SUBLANES):
                s = s8 * SUBLANES + r
                fn(s, lst[0, s])
            return carry
        lax.fori_loop(0, cap // SUBLANES, body, 0)

    def start_gather(lst, bb, sl):
        def one(s, n):
            pltpu.make_async_copy(h_hbm.at[bb, pl.ds(n, 1)], x_buf.at[sl, pl.ds(s, 1)], sem.at[0, sl]).start()
            pltpu.make_async_copy(acc_in_hbm.at[bb, pl.ds(n, 1)], a_buf.at[sl, pl.ds(s, 1)],
                                  sem.at[1, sl]).start()
        for_chosen(lst, one)

    def wait_gather(sl):
        pltpu.make_async_copy(h_hbm.at[b, pl.ds(0, cap)], x_buf.at[sl], sem.at[0, sl]).wait()
        pltpu.make_async_copy(acc_in_hbm.at[b, pl.ds(0, cap)], a_buf.at[sl], sem.at[1, sl]).wait()

    def start_scatter(sl):
        def one(s, n):
            pltpu.make_async_copy(a_buf.at[sl, pl.ds(s, 1)], acc_hbm.at[b, pl.ds(n, 1)], sem.at[2, sl]).start()
        for_chosen(lst_ref, one)

    def wait_scatter(sl):
        pltpu.make_async_copy(a_buf.at[sl], acc_hbm.at[b, pl.ds(0, cap)], sem.at[2, sl]).wait()

    @pl.when(step == 0)
    def _():
        start_gather(lst_ref, b, slot)

    wait_gather(slot)
    for r0 in range(0, cap, half):
        rows = pl.ds(r0, half)
        x = x_buf[slot, rows, :].astype(BF16)
        a = _dot(x, wg_ref[...])
        u = _dot(x, wu_ref[...])
        y = _dot((_silu(a) * u).astype(BF16), wd_ref[...]) * gate_ref[rows, :]
        a_buf[slot, rows, :] = a_buf[slot, rows, :] + y

    @pl.when(step > 0)
    def _():
        wait_scatter(other)

    @pl.when(step + 1 < n_steps)
    def _():
        start_gather(lst_next_ref, b_next, other)

    start_scatter(slot)

    @pl.when(step == n_steps - 1)
    def _():
        wait_scatter(slot)


def _expert_ffn(h2, acc0, lst, gates, w_gate, w_up, w_down):
    b, t, d = h2.shape
    e, _, f = w_gate.shape
    ep = lst.shape[0] // b
    cap = lst.shape[1] * LANES
    assert b >= 2
    half = cap // 2 if cap % 16 == 0 else cap

    def cur(ei, bi):
        return bi * ep + ei, 0, 0

    def nxt(ei, bi):
        nb = bi + 1
        return (nb % b) * ep + jnp.minimum(ei + nb // b, e - 1), 0, 0

    def smem(shape, index_map):
        return pl.BlockSpec(shape, index_map, memory_space=pltpu.SMEM)

    return pl.pallas_call(
        functools.partial(_ffn_kernel, cap=cap, half=half),
        grid=(e, b),
        in_specs=[smem((None, 1, cap), cur), smem((None, 1, cap), nxt),
                  pl.BlockSpec((None, cap, 1), cur),
                  pl.BlockSpec(memory_space=pl.ANY),
                  pl.BlockSpec(memory_space=pl.ANY),
                  pl.BlockSpec((None, d, f), lambda ei, bi: (ei, 0, 0)),
                  pl.BlockSpec((None, d, f), lambda ei, bi: (ei, 0, 0)),
                  pl.BlockSpec((None, f, d), lambda ei, bi: (ei, 0, 0))],
        out_specs=pl.BlockSpec(memory_space=pl.ANY),
        out_shape=jax.ShapeDtypeStruct((b, t, d), F32),
        scratch_shapes=[pltpu.VMEM((2, cap, d), F32), pltpu.VMEM((2, cap, d), F32),
                        pltpu.SemaphoreType.DMA((3, 2))],
        input_output_aliases={4: 0},
        compiler_params=_cparams("arbitrary", "arbitrary"),
        name="expert_ffn",
    )(lst.reshape(b * ep, 1, cap), lst.reshape(b * ep, 1, cap), gates.reshape(b * ep, cap, 1),
      h2, acc0, w_gate, w_up, w_down)


def _final_kernel(x1_ref, acc_ref, g_ref, nrm_ref, o_ref):
    x = x1_ref[...] + g_ref[...] * acc_ref[...]
    o_ref[...] = x * lax.rsqrt(jnp.mean(x * x, axis=-1, keepdims=True) + NORM_EPS) * nrm_ref[...]


def _final(x1, acc, gate, norm_gain, bm):
    b, t, d = x1.shape
    def row(): return pl.BlockSpec((None, bm, d), lambda bi, i: (bi, i, 0))
    return pl.pallas_call(
        _final_kernel,
        grid=(b, t // bm),
        in_specs=[row(), row(), pl.BlockSpec((None, 1, d), lambda bi, i: (bi, 0, 0)),
                  pl.BlockSpec((1, d), lambda bi, i: (0, 0))],
        out_specs=row(),
        out_shape=jax.ShapeDtypeStruct((b, t, d), F32),
        compiler_params=_cparams("arbitrary", "arbitrary"),
        name="final_norm",
    )(x1, acc, gate, norm_gain.reshape(1, d))


def _row_block(t, largest):
    for bm in (1024, 512, 256, 128, 64):
        if bm <= largest and t % bm == 0:
            return bm
    raise ValueError(f"token count {t} is not a multiple of 64")


def kernel(x, c, ctx, c_ctx, ada_w, ada_b, norm_mix, norm_ffn, w_in, gdn_conv, gdn_a_log, gdn_dt_bias,
           hgrn_lb, hgrn_norm, gdn_norm, w_branch_a, w_branch_b, w_out, router_w, w_gate, w_up, w_down,
           final_norm):
    depth = ada_w.shape[0]
    assert depth == 1, "single-layer stack"
    b, t, d = x.shape
    tc = ctx.shape[1]
    n_experts = router_w.shape[-1]
    assert d % COL == 0 and (6 * COL) % d == 0 and t == CHUNK * GRID_W and tc % CHUNK == 0
    assert w_in.shape[-1] == 9 * COL + 4 * HEADS + 2 * d and n_experts <= LANES

    lower = jnp.cumsum(jax.nn.softmax(hgrn_lb.astype(F32), axis=1), axis=1)

    c_rows = jnp.zeros((SUBLANES, d), F32).at[:b].set(c).at[b].set(c_ctx)
    mods = _modulation(c_rows, ada_w[0], ada_b[0])
    ml = [m.reshape(b, 1, d) for m in jnp.split(mods[:b], 6, axis=-1)]
    mc = [m.reshape(1, 1, d) for m in jnp.split(mods[b:b + 1], 6, axis=-1)]

    wi = w_in[0]
    s_ab = 9 * COL
    s_gates = s_ab + 4 * HEADS
    w_main = jnp.concatenate([wi[:, :5 * COL], wi[:, 8 * COL:s_ab], wi[:, s_gates:]], axis=1).astype(BF16)
    w_qkv = wi[:, 5 * COL:8 * COL].astype(BF16)
    w_ab = jnp.pad(wi[:, s_ab:s_gates], ((0, 0), (0, LANES - 4 * HEADS))).astype(BF16)

    bm_lat, bm_ctx = _row_block(t, 1024), _row_block(tc, 1024)
    (p_lat,) = _input_projection(x, norm_mix[0], ml[0], ml[1], w_main, None, BF16, bm_lat)
    qkv_lat, ab_lat = _input_projection(x, norm_mix[0], ml[0], ml[1], w_qkv, w_ab, F32, bm_lat)
    (p_ctx,) = _input_projection(ctx, norm_mix[0], mc[0], mc[1], w_main[:, :4 * COL], None, BF16, bm_ctx)
    qkv_ctx, ab_ctx = _input_projection(ctx, norm_mix[0], mc[0], mc[1], w_qkv, w_ab, F32, bm_ctx)

    oa_f, oa_b = _hgrn_scans(p_ctx, p_lat, lower[0, 0], lower[1, 0])

    x_ctx = _gdn_prep_ctx(qkv_ctx, gdn_conv[0])
    x_lat, ab_lat_cm = _gdn_prep_lat(qkv_lat, ab_lat, gdn_conv[0])
    pad = LANES - 2 * HEADS
    alog_row = jnp.pad(gdn_a_log[0].reshape(1, 2 * HEADS).astype(F32), ((0, 0), (0, pad)))
    dtb_row = jnp.pad(gdn_dt_bias[0].reshape(1, 2 * HEADS).astype(F32), ((0, 0), (0, pad)))
    ob_f, ob_b = _gdn_scans(x_ctx, x_lat, ab_ctx.reshape(b, tc // CHUNK, CHUNK, LANES), ab_lat_cm,
                            alog_row, dtb_row)

    u = _merge(p_lat, oa_f, oa_b, ob_f, ob_b, hgrn_norm[0], gdn_norm[0],
               w_branch_a[0].astype(BF16), w_branch_b[0].astype(BF16))
    router_pad = jnp.pad(router_w[0].astype(F32), ((0, 0), (0, LANES - n_experts)))
    bm_row = _row_block(t, 256)
    x1, h2, aff_t, acc0 = _outproj_router(u, w_out[0].astype(BF16), x, ml[2], norm_ffn[0], ml[3], ml[4],
                                          router_pad, n_experts, bm_row)

    cap = max(1, (CAPACITY_FACTOR * t) // n_experts)
    lst, gates = _route(aff_t.reshape(-1, t), cap)
    acc = _expert_ffn(h2, acc0, lst, gates, w_gate[0].astype(BF16), w_up[0].astype(BF16),
                      w_down[0].astype(BF16))
    return _final(x1, acc, ml[5], final_norm, bm_row)
```

```python
import functools

import jax
import jax.numpy as jnp
from jax import lax
from jax.experimental import pallas as pl
from jax.experimental.pallas import tpu as pltpu

F32 = jnp.float32
BF16 = jnp.bfloat16

NORM_EPS = 1e-6
LOG2E = 1.4426950408889634
GRID_W = 64
HEADS = 8
HEAD_DIM = 128
CHUNK = 64
CONV_K = 5
CAPACITY_FACTOR = 2
COL = HEADS * HEAD_DIM
LANES = 128
SUBLANES = 8
GROUP = SUBLANES
INV_BLOCK = 8
VMEM_LIMIT_BYTES = 56 * 1024 * 1024


def _cparams(*sem):
    return pltpu.CompilerParams(dimension_semantics=sem, vmem_limit_bytes=VMEM_LIMIT_BYTES)


def _dot(a, b):
    return jnp.dot(a, b, preferred_element_type=F32)


def _dot_nt(a, b):
    return lax.dot_general(a, b, (((1,), (1,)), ((), ())), preferred_element_type=F32)


def _silu(x):
    return x * jax.nn.sigmoid(x)


def _scan_cumsum(g, rev, terms):
    tri = jnp.where(_tri(rev), 1.0, 0.0).astype(BF16)
    total = None
    rest = g
    for _ in range(terms):
        part = rest.astype(BF16)
        rest = rest - part.astype(F32)
        total = _dot(tri, part) if total is None else total + _dot(tri, part)
    return total


def _resident(shape, index_map):
    return pl.BlockSpec(shape, index_map, pipeline_mode=pl.Buffered(1))


def _mod_kernel(c_ref, w_ref, b_ref, o_ref):
    o_ref[...] = _dot(_silu(c_ref[...]), w_ref[...]) + b_ref[...]


def _modulation(c_rows, ada_w, ada_b):
    rows, d = c_rows.shape
    n = ada_w.shape[1]
    bn = 1024
    return pl.pallas_call(
        _mod_kernel,
        grid=(n // bn,),
        in_specs=[pl.BlockSpec((rows, d), lambda j: (0, 0)),
                  pl.BlockSpec((d, bn), lambda j: (0, j)),
                  pl.BlockSpec((1, bn), lambda j: (0, j))],
        out_specs=pl.BlockSpec((rows, bn), lambda j: (0, j)),
        out_shape=jax.ShapeDtypeStruct((rows, n), F32),
        compiler_params=_cparams("arbitrary"),
        name="modulation",
    )(c_rows, ada_w, ada_b.reshape(1, n))


def _proj_kernel(x_ref, gain_ref, shift_ref, scale_ref, w_ref, *rest, with_ab):
    if with_ab:
        wab_ref, o_ref, oab_ref, h_ref = rest
    else:
        o_ref, h_ref = rest

    @pl.when(pl.program_id(2) == 0)
    def _():
        x = x_ref[...]
        y = x * lax.rsqrt(jnp.mean(x * x, axis=-1, keepdims=True) + NORM_EPS) * gain_ref[...]
        h = (y * (1.0 + scale_ref[...]) + shift_ref[...]).astype(BF16)
        h_ref[...] = h
        if with_ab:
            oab_ref[...] = _dot(h, wab_ref[...])

    o_ref[...] = _dot(h_ref[...], w_ref[...]).astype(o_ref.dtype)


def _input_projection(x, gain, shift, scale, w, w_ab, out_dtype, bm):
    b, t, d = x.shape
    n = w.shape[1]
    per_batch = shift.shape[0] > 1
    mod_map = (lambda bi, i, j: (bi, 0, 0)) if per_batch else (lambda bi, i, j: (0, 0, 0))
    with_ab = w_ab is not None
    in_specs = [pl.BlockSpec((None, bm, d), lambda bi, i, j: (bi, i, 0)),
                pl.BlockSpec((1, d), lambda bi, i, j: (0, 0)),
                pl.BlockSpec((None, 1, d), mod_map),
                pl.BlockSpec((None, 1, d), mod_map),
                pl.BlockSpec((d, COL), lambda bi, i, j: (0, j))]
    out_specs = [pl.BlockSpec((None, bm, COL), lambda bi, i, j: (bi, i, j))]
    out_shape = [jax.ShapeDtypeStruct((b, t, n), out_dtype)]
    args = [x, gain.reshape(1, d), shift, scale, w]
    if with_ab:
        in_specs.append(pl.BlockSpec((d, LANES), lambda bi, i, j: (0, 0)))
        out_specs.append(pl.BlockSpec((None, bm, LANES), lambda bi, i, j: (bi, i, 0)))
        out_shape.append(jax.ShapeDtypeStruct((b, t, LANES), F32))
        args.append(w_ab)
    return pl.pallas_call(
        functools.partial(_proj_kernel, with_ab=with_ab),
        grid=(b, t // bm, n // COL),
        in_specs=in_specs, out_specs=out_specs, out_shape=out_shape,
        scratch_shapes=[pltpu.VMEM((bm, d), BF16)],
        compiler_params=_cparams("arbitrary", "arbitrary", "arbitrary"),
        name="input_projection",
    )(*args)


def _tri(rev, strict=False):
    ri = lax.broadcasted_iota(jnp.int32, (CHUNK, CHUNK), 0)
    ci = lax.broadcasted_iota(jnp.int32, (CHUNK, CHUNK), 1)
    if rev:
        return (ci > ri) if strict else (ci >= ri)
    return (ci < ri) if strict else (ci <= ri)


def _boundary_rows(g, bs, rev):
    n, w = g.shape
    off = (bs >> 1) - 1 + (1 if rev else 0)
    if bs >= SUBLANES:
        pieces = [jnp.broadcast_to(g[p * bs + off:p * bs + off + 1], (bs, w)) for p in range(n // bs)]
        return jnp.concatenate(pieces, axis=0)
    sub = lax.broadcasted_iota(jnp.int32, (SUBLANES, w), 0)
    pieces = []
    for r0 in range(0, n, SUBLANES):
        acc = None
        for p in range(SUBLANES // bs):
            r = r0 + p * bs + off
            val = jnp.broadcast_to(g[r:r + 1], (SUBLANES, w))
            acc = val if acc is None else jnp.where(sub >= p * bs, val, acc)
        pieces.append(acc)
    return jnp.concatenate(pieces, axis=0)


def _hgrn_chunks(scans):
    row = lax.broadcasted_iota(jnp.int32, (CHUNK, 1), 0)
    ri = lax.broadcasted_iota(jnp.int32, (CHUNK, CHUNK), 0)
    ci = lax.broadcasted_iota(jnp.int32, (CHUNK, CHUNK), 1)
    n_levels = CHUNK.bit_length() - 1
    eye = ri == ci

    units = []
    for q_raw, f_raw, v, low, s_ref, rev in scans:
        q = _silu(q_raw.astype(F32))
        f_raw = f_raw.astype(F32)
        one_m = 1.0 - low
        f = low + one_m * jax.nn.sigmoid(f_raw)
        k = one_m * jax.nn.sigmoid(-f_raw)
        gc = _scan_cumsum(jnp.log(f), rev, 2) * LOG2E
        g_last = gc[0:1] if rev else gc[CHUNK - 1:CHUNK]
        q_dec = (q * jnp.exp2(gc)).astype(BF16)
        k_dec = (k * jnp.exp2(g_last - gc)).astype(BF16)
        mixed, pair = [], []
        for lv in range(1, n_levels + 1):
            upper = ((row >> (lv - 1)) & 1) == 1
            q_rows = jnp.logical_not(upper) if rev else upper
            decay = jnp.exp2(-jnp.abs(gc - _boundary_rows(gc, 1 << lv, rev)))
            mixed.append((jnp.where(q_rows, q, k) * decay).astype(BF16))
            i_upper = ((ri >> (lv - 1)) & 1) == 1
            j_upper = ((ci >> (lv - 1)) & 1) == 1
            halves = jnp.logical_and(j_upper, jnp.logical_not(i_upper)) if rev else \
                jnp.logical_and(i_upper, jnp.logical_not(j_upper))
            pair.append(jnp.logical_and((ri >> lv) == (ci >> lv), halves))
        qk = q * k
        v_b = v.astype(BF16)
        v_f = v.astype(F32)
        for h in range(HEADS):
            sl = slice(h * HEAD_DIM, (h + 1) * HEAD_DIM)
            units.append(dict(s_ref=s_ref, h=h, diag=jnp.sum(qk[:, sl], axis=1, keepdims=True),
                              mixed=[x[:, sl] for x in mixed], pair=pair,
                              q_dec=q_dec[:, sl], k_dec=k_dec[:, sl], v_b=v_b[:, sl], v_f=v_f[:, sl],
                              decay=jnp.exp2(g_last[:, sl])))

    scores = [[_dot_nt(m, m) for m in u["mixed"]] for u in units]
    states = [u["s_ref"][u["h"]] for u in units]
    carried = [_dot_nt(u["q_dec"], s.astype(BF16)) for u, s in zip(units, states)]
    outs = []
    for u, sc, car in zip(units, scores, carried):
        a = jnp.where(eye, u["diag"], 0.0)
        for lv in range(n_levels):
            a = jnp.where(u["pair"][lv], sc[lv], a)
        outs.append(_dot(a.astype(BF16), u["v_b"]) + car)
    for u, s in zip(units, states):
        u["s_ref"][u["h"]] = u["decay"] * s + _dot(u["v_f"].T.astype(BF16), u["k_dec"])
    return [jnp.concatenate(outs[i * HEADS:(i + 1) * HEADS], axis=1) for i in range(len(scans))]


def _hgrn_kernel(lowf_ref, lowb_ref,
                 cqf, cff, cvf, cqb, cfb, cvb,
                 lqf, lff, lvf, lqb, lfb, lvb,
                 of_ref, ob_ref, sf_ref, sb_ref, *, n_ctx, per_step):
    t = pl.program_id(1)

    @pl.when(t == 0)
    def _():
        sf_ref[...] = jnp.zeros_like(sf_ref)
        sb_ref[...] = jnp.zeros_like(sb_ref)

    is_ctx = t < n_ctx

    def body(j, carry):
        rows_f = pl.ds(pl.multiple_of(j * CHUNK, CHUNK), CHUNK)
        rows_b = pl.ds(pl.multiple_of((per_step - 1 - j) * CHUNK, CHUNK), CHUNK)

        def pick(c_ref, l_ref, rows):
            return jnp.where(is_ctx, c_ref[rows, :], l_ref[rows, :])

        o_f, o_b = _hgrn_chunks(
            [(pick(cqf, lqf, rows_f), pick(cff, lff, rows_f), pick(cvf, lvf, rows_f), lowf_ref[...], sf_ref, False),
             (pick(cqb, lqb, rows_b), pick(cfb, lfb, rows_b), pick(cvb, lvb, rows_b), lowb_ref[...], sb_ref, True)])
        of_ref[rows_f, :] = o_f.astype(of_ref.dtype)
        ob_ref[rows_b, :] = o_b.astype(ob_ref.dtype)
        return carry

    lax.fori_loop(0, per_step, body, 0)


def _hgrn_scans(p_ctx, p_lat, low_f, low_b):
    b, tc, _ = p_ctx.shape
    t = p_lat.shape[1]
    per_step = max(g for g in (4, 2, 1) if (tc // CHUNK) % g == 0 and (t // CHUNK) % g == 0)
    rows = per_step * CHUNK
    n_ctx, n_lat = tc // rows, t // rows

    def cf(ti): return jnp.minimum(ti, n_ctx - 1)
    def cb(ti): return jnp.maximum(n_ctx - 1 - ti, 0)
    def lf(ti): return jnp.maximum(ti - n_ctx, 0)
    def lb(ti): return n_lat - 1 - jnp.maximum(ti - n_ctx, 0)

    def spec(block_of, col):
        return pl.BlockSpec((None, rows, COL), lambda bi, ti: (bi, block_of(ti), col))

    low_spec = pl.BlockSpec((1, COL), lambda bi, ti: (0, 0))
    out_f, out_b = pl.pallas_call(
        functools.partial(_hgrn_kernel, n_ctx=n_ctx, per_step=per_step),
        grid=(b, n_ctx + n_lat),
        in_specs=[low_spec, low_spec,
                  spec(cf, 0), spec(cf, 1), spec(cf, 3), spec(cb, 0), spec(cb, 2), spec(cb, 3),
                  spec(lf, 0), spec(lf, 1), spec(lf, 3), spec(lb, 0), spec(lb, 2), spec(lb, 3)],
        out_specs=[pl.BlockSpec((None, rows, COL), lambda bi, ti: (bi, lf(ti), 0)),
                   pl.BlockSpec((None, rows, COL), lambda bi, ti: (bi, lb(ti), 0))],
        out_shape=[jax.ShapeDtypeStruct((b, t, COL), BF16)] * 2,
        scratch_shapes=[pltpu.VMEM((HEADS, HEAD_DIM, HEAD_DIM), F32)] * 2,
        compiler_params=_cparams("arbitrary", "arbitrary"),
        name="hgrn_scans",
    )(low_f.reshape(1, COL), low_b.reshape(1, COL), *([p_ctx] * 6), *([p_lat] * 6))
    return out_f, out_b


def _conv_silu_norm(x, prev, nxt, w, idx):
    row = lax.broadcasted_iota(jnp.int32, (CHUNK, 1), 0)
    xm1 = jnp.where(row == 0, prev[7:8], pltpu.roll(x, 1, 0))
    xm2 = jnp.where(row == 0, prev[6:7], jnp.where(row == 1, prev[7:8], pltpu.roll(x, 2, 0)))
    xp1 = jnp.where(row == CHUNK - 1, nxt[0:1], pltpu.roll(x, CHUNK - 1, 0))
    xp2 = jnp.where(row == CHUNK - 1, nxt[1:2],
                    jnp.where(row == CHUNK - 2, nxt[0:1], pltpu.roll(x, CHUNK - 2, 0)))
    return _silu_norm(w[0:1] * xm2 + w[1:2] * xm1 + w[2:3] * x + w[3:4] * xp1 + w[4:5] * xp2, idx)


def _silu_norm(y, idx):
    y = _silu(y)
    if idx == 2:
        return y
    scale = HEAD_DIM ** -0.5 if idx == 0 else 1.0
    pieces = []
    for h in range(HEADS):
        yh = y[:, h * HEAD_DIM:(h + 1) * HEAD_DIM]
        pieces.append(yh * (lax.rsqrt(jnp.sum(yh * yh, axis=1, keepdims=True) + NORM_EPS) * scale))
    return jnp.concatenate(pieces, axis=1)


def _gdn_prep_ctx_kernel(q_ref, k_ref, v_ref, pq_ref, pk_ref, pv_ref, nq_ref, nk_ref, nv_ref, w_ref, o_ref):
    t = pl.program_id(1)
    first = t == 0
    last = t == pl.num_programs(1) - 1
    for idx, (c_ref, p_ref, n_ref) in enumerate(((q_ref, pq_ref, nq_ref), (k_ref, pk_ref, nk_ref),
                                                 (v_ref, pv_ref, nv_ref))):
        prev = jnp.where(first, 0.0, p_ref[...])
        nxt = jnp.where(last, 0.0, n_ref[...])
        y = _conv_silu_norm(c_ref[...], prev, nxt, w_ref[:, idx * COL:(idx + 1) * COL], idx)
        o_ref[:, idx * COL:(idx + 1) * COL] = y.astype(o_ref.dtype)


def _gdn_prep_ctx(qkv, conv_w):
    b, t, _ = qkv.shape
    n_chunks = t // CHUNK
    rb = CHUNK // SUBLANES
    def cur(col): return pl.BlockSpec((None, CHUNK, COL), lambda bi, c: (bi, c, col))
    def prv(col): return pl.BlockSpec((None, SUBLANES, COL), lambda bi, c: (bi, jnp.maximum(c * rb - 1, 0), col))
    def nxt(col): return pl.BlockSpec((None, SUBLANES, COL),
                                      lambda bi, c: (bi, jnp.minimum((c + 1) * rb, n_chunks * rb - 1), col))
    cols = (0, 1, 2)
    return pl.pallas_call(
        _gdn_prep_ctx_kernel,
        grid=(b, n_chunks),
        in_specs=[cur(c) for c in cols] + [prv(c) for c in cols] + [nxt(c) for c in cols]
                 + [pl.BlockSpec((CONV_K, 3 * COL), lambda bi, c: (0, 0))],
        out_specs=pl.BlockSpec((None, None, CHUNK, 3 * COL), lambda bi, c: (bi, c, 0, 0)),
        out_shape=jax.ShapeDtypeStruct((b, n_chunks, CHUNK, 3 * COL), BF16),
        compiler_params=_cparams("arbitrary", "arbitrary"),
        name="gdn_prep_ctx",
    )(*([qkv] * 9), conv_w)


def _gdn_prep_lat_kernel(q_ref, k_ref, v_ref, pq_ref, pk_ref, pv_ref, nq_ref, nk_ref, nv_ref, ab_ref, w_ref,
                         o_ref, oab_ref):
    g = pl.program_id(1)
    first = g == 0
    last = g == pl.num_programs(1) - 1
    tail = CHUNK - SUBLANES
    for idx, (c_ref, p_ref, n_ref) in enumerate(((q_ref, pq_ref, nq_ref), (k_ref, pk_ref, nk_ref),
                                                 (v_ref, pv_ref, nv_ref))):
        w = w_ref[:, idx * COL:(idx + 1) * COL]
        cols = [c_ref[:, j, :] for j in range(GROUP)]
        before = jnp.where(first, 0.0, p_ref[:, GROUP - 1, :])
        after = jnp.where(last, 0.0, n_ref[:, 0, :])
        for j in range(GROUP):
            prev = cols[j - 1][tail:] if j > 0 else before
            nxt = cols[j + 1][:SUBLANES] if j < GROUP - 1 else after
            o_ref[j, :, idx * COL:(idx + 1) * COL] = _conv_silu_norm(cols[j], prev, nxt, w, idx).astype(o_ref.dtype)
    for j in range(GROUP):
        oab_ref[j] = ab_ref[:, j, :]


def _gdn_prep_lat(qkv, ab, conv_w):
    b, t, _ = qkv.shape
    assert t == CHUNK * GRID_W and GRID_W % GROUP == 0
    n_groups = GRID_W // GROUP
    rb = CHUNK // SUBLANES
    q4 = qkv.reshape(b, CHUNK, GRID_W, 3 * COL)
    ab4 = ab.reshape(b, CHUNK, GRID_W, LANES)
    def cur(col): return pl.BlockSpec((None, CHUNK, GROUP, COL), lambda bi, g: (bi, 0, g, col))
    def prv(col): return pl.BlockSpec((None, SUBLANES, GROUP, COL),
                                      lambda bi, g: (bi, rb - 1, jnp.maximum(g - 1, 0), col))
    def nxt(col): return pl.BlockSpec((None, SUBLANES, GROUP, COL),
                                      lambda bi, g: (bi, 0, jnp.minimum(g + 1, n_groups - 1), col))
    cols = (0, 1, 2)
    return pl.pallas_call(
        _gdn_prep_lat_kernel,
        grid=(b, n_groups),
        in_specs=[cur(c) for c in cols] + [prv(c) for c in cols] + [nxt(c) for c in cols]
                 + [pl.BlockSpec((None, CHUNK, GROUP, LANES), lambda bi, g: (bi, 0, g, 0)),
                    pl.BlockSpec((CONV_K, 3 * COL), lambda bi, g: (0, 0))],
        out_specs=[pl.BlockSpec((None, GROUP, CHUNK, 3 * COL), lambda bi, g: (bi, g, 0, 0)),
                   pl.BlockSpec((None, GROUP, CHUNK, LANES), lambda bi, g: (bi, g, 0, 0))],
        out_shape=[jax.ShapeDtypeStruct((b, GRID_W, CHUNK, 3 * COL), BF16),
                   jax.ShapeDtypeStruct((b, GRID_W, CHUNK, LANES), F32)],
        compiler_params=_cparams("arbitrary", "arbitrary"),
        name="gdn_prep_lat",
    )(*([q4] * 9), ab4, conv_w)


def _mm_bf16(a, b):
    return _dot(a.astype(BF16), b.astype(BF16))


def _unit_tri_inverses(ns):
    ri = lax.broadcasted_iota(jnp.int32, (CHUNK, CHUNK), 0)
    ci = lax.broadcasted_iota(jnp.int32, (CHUNK, CHUNK), 1)
    eye = jnp.where(ri == ci, 1.0, 0.0)
    shift = INV_BLOCK.bit_length() - 1
    on_diag = (ri >> shift) == (ci >> shift)

    def nilpotent_inverses(ms, degree):
        xs = [eye - m for m in ms]
        ps = ms
        for _ in range(degree.bit_length() - 2):
            ps = [_mm_bf16(p, p) for p in ps]
            xs = [x + _mm_bf16(x, p) for x, p in zip(xs, ps)]
        return xs

    d_invs = nilpotent_inverses([jnp.where(on_diag, n, 0.0) for n in ns], INV_BLOCK)
    ms = [_mm_bf16(d, jnp.where(on_diag, 0.0, n)) for d, n in zip(d_invs, ns)]
    ys = nilpotent_inverses(ms, CHUNK // INV_BLOCK)
    return [_mm_bf16(y, d) for y, d in zip(ys, d_invs)]


def _gdn_chunks(scans, alog, dtb):
    units = []
    for x, ab, s_ref, rev in scans:
        g_all = -jnp.exp(alog) * jax.nn.softplus(ab + dtb)
        beta_all = jax.nn.sigmoid(ab)
        gc_all = _scan_cumsum(g_all, rev, 3)
        gr_all = gc_all.T
        incl = _tri(rev)
        strict = _tri(rev, strict=True)
        off = HEADS if rev else 0
        for h in range(HEADS):
            gcol = gc_all[:, off + h:off + h + 1]
            grow = gr_all[off + h:off + h + 1, :]
            units.append(dict(
                s_ref=s_ref, h=h, strict=strict, gcol=gcol,
                beta=beta_all[:, 2 * HEADS + off + h:2 * HEADS + off + h + 1],
                g_last=gcol[0:1] if rev else gcol[CHUNK - 1:CHUNK],
                gamma=jnp.where(incl, jnp.exp(jnp.where(incl, gcol - grow, 0.0)), 0.0),
                q=x[:, h * HEAD_DIM:(h + 1) * HEAD_DIM],
                k=x[:, COL + h * HEAD_DIM:COL + (h + 1) * HEAD_DIM],
                v=x[:, 2 * COL + h * HEAD_DIM:2 * COL + (h + 1) * HEAD_DIM]))

    kks = [_dot_nt(u["k"], u["k"]) for u in units]
    qks = [_dot_nt(u["q"], u["k"]) for u in units]
    ns = [jnp.where(u["strict"], u["beta"] * kk * u["gamma"], 0.0) for u, kk in zip(units, kks)]
    invs = _unit_tri_inverses(ns)
    rhss = [jnp.concatenate([u["v"].astype(F32) * u["beta"],
                             u["k"].astype(F32) * (u["beta"] * jnp.exp(u["gcol"]))], axis=1) for u in units]
    sols = [_mm_bf16(a, r) for a, r in zip(invs, rhss)]
    states = [u["s_ref"][u["h"]] for u in units]
    states_b = [s.astype(BF16) for s in states]
    v_news = [sol[:, :HEAD_DIM] - _dot_nt(sol[:, HEAD_DIM:].astype(BF16), s_b) for sol, s_b in zip(sols, states_b)]
    outs = [_dot_nt((u["q"].astype(F32) * jnp.exp(u["gcol"])).astype(BF16), s_b)
            + _mm_bf16(qk * u["gamma"], v_new)
            for u, qk, s_b, v_new in zip(units, qks, states_b, v_news)]
    for u, s, v_new in zip(units, states, v_news):
        k_dec = (u["k"].astype(F32) * jnp.exp(u["g_last"] - u["gcol"])).astype(BF16)
        u["s_ref"][u["h"]] = jnp.exp(u["g_last"]) * s + _dot(v_new.T.astype(BF16), k_dec)
    return [jnp.concatenate(outs[i * HEADS:(i + 1) * HEADS], axis=1) for i in range(len(scans))]


def _gdn_ctx_kernel(alog_ref, dtb_ref, x_ref, ab_ref, sf_ref, sb_ref, *, n_chunks):
    sf_ref[...] = jnp.zeros_like(sf_ref)
    sb_ref[...] = jnp.zeros_like(sb_ref)

    def body(j, carry):
        jb = n_chunks - 1 - j
        _gdn_chunks([(x_ref[j], ab_ref[j], sf_ref, False), (x_ref[jb], ab_ref[jb], sb_ref, True)],
                    alog_ref[...], dtb_ref[...])
        return carry

    lax.fori_loop(0, n_chunks, body, 0)


def _gdn_lat_kernel(alog_ref, dtb_ref, s0f_ref, s0b_ref, xf_ref, abf_ref, xb_ref, abb_ref,
                    of_ref, ob_ref, sf_ref, sb_ref):
    n_samples, per_step = xf_ref.shape[:2]

    @pl.when(pl.program_id(1) == 0)
    def _():
        sf_ref[...] = s0f_ref[...]
        sb_ref[...] = s0b_ref[...]

    def body(j, carry):
        jb = per_step - 1 - j
        scans = []
        for i in range(n_samples):
            scans.append((xf_ref[i, j], abf_ref[i, j], sf_ref.at[i], False))
            scans.append((xb_ref[i, jb], abb_ref[i, jb], sb_ref.at[i], True))
        outs = _gdn_chunks(scans, alog_ref[...], dtb_ref[...])
        for i in range(n_samples):
            of_ref[i, j] = outs[2 * i]
            ob_ref[i, jb] = outs[2 * i + 1]
        return carry

    lax.fori_loop(0, per_step, body, 0)


def _gdn_scans(x_ctx, x_lat, ab_ctx, ab_lat, alog_row, dtb_row):
    b, n_ctx = x_ctx.shape[:2]
    n_lat = x_lat.shape[1]
    state_shape = jax.ShapeDtypeStruct((b, HEADS, HEAD_DIM, HEAD_DIM), F32)
    def state_spec(nidx):
        return pl.BlockSpec((None, HEADS, HEAD_DIM, HEAD_DIM), lambda *i: (i[0], 0, 0, 0))
    s_f, s_b = pl.pallas_call(
        functools.partial(_gdn_ctx_kernel, n_chunks=n_ctx),
        grid=(b,),
        in_specs=[pl.BlockSpec((1, LANES), lambda bi: (0, 0)), pl.BlockSpec((1, LANES), lambda bi: (0, 0)),
                  pl.BlockSpec((None, n_ctx, CHUNK, 3 * COL), lambda bi: (bi, 0, 0, 0)),
                  pl.BlockSpec((None, n_ctx, CHUNK, LANES), lambda bi: (bi, 0, 0, 0))],
        out_specs=[state_spec(1), state_spec(1)],
        out_shape=[state_shape, state_shape],
        compiler_params=_cparams("arbitrary"),
        name="gdn_ctx_scans",
    )(alog_row, dtb_row, x_ctx, ab_ctx)

    nb = 2 if b % 2 == 0 else 1
    per_step = max(g for g in (4, 2, 1) if n_lat % g == 0)
    n_steps = n_lat // per_step
    def gf(ti): return ti
    def gb(ti): return n_steps - 1 - ti
    def lat_state_spec():
        return pl.BlockSpec((nb, HEADS, HEAD_DIM, HEAD_DIM), lambda bi, ti: (bi, 0, 0, 0))
    def xspec(group_of):
        return pl.BlockSpec((nb, per_step, CHUNK, 3 * COL), lambda bi, ti: (bi, group_of(ti), 0, 0))
    def abspec(group_of):
        return pl.BlockSpec((nb, per_step, CHUNK, LANES), lambda bi, ti: (bi, group_of(ti), 0, 0))
    def ospec(group_of):
        return pl.BlockSpec((nb, per_step, CHUNK, COL), lambda bi, ti: (bi, group_of(ti), 0, 0))
    row_spec = pl.BlockSpec((1, LANES), lambda bi, ti: (0, 0))
    return pl.pallas_call(
        _gdn_lat_kernel,
        grid=(b // nb, n_steps),
        in_specs=[row_spec, row_spec, lat_state_spec(), lat_state_spec(),
                  xspec(gf), abspec(gf), xspec(gb), abspec(gb)],
        out_specs=[ospec(gf), ospec(gb)],
        out_shape=[jax.ShapeDtypeStruct((b, n_lat, CHUNK, COL), F32)] * 2,
        scratch_shapes=[pltpu.VMEM((nb, HEADS, HEAD_DIM, HEAD_DIM), F32)] * 2,
        compiler_params=_cparams("arbitrary", "arbitrary"),
        name="gdn_lat_scans",
    )(alog_row, dtb_row, s_f, s_b, x_lat, ab_lat, x_lat, ab_lat)


def _head_norm(x):
    pieces = []
    for h in range(HEADS):
        xh = x[:, h * HEAD_DIM:(h + 1) * HEAD_DIM]
        pieces.append(xh * lax.rsqrt(jnp.mean(xh * xh, axis=1, keepdims=True) + NORM_EPS))
    return jnp.concatenate(pieces, axis=1)


def _merge_kernel(oaf, oab, og, obf, obb, z, ga, gb, hag, gbg, wba, wbb, u_ref):
    o_a = oaf[...].astype(F32) + oab[...].astype(F32)
    y_a = _head_norm(o_a * jax.nn.sigmoid(og[...].astype(F32))) * hag[...]
    o_b = jnp.concatenate([obf[:, r, :] + obb[:, r, :] for r in range(obf.shape[1])], axis=0)
    y_b = _head_norm(o_b) * gbg[...] * _silu(z[...].astype(F32))
    u = jax.nn.sigmoid(ga[...].astype(F32)) * _dot(y_a.astype(BF16), wba[...])
    u = u + jax.nn.sigmoid(gb[...].astype(F32)) * _dot(y_b.astype(BF16), wbb[...])
    u_ref[...] = u.astype(u_ref.dtype)


def _merge(p_lat, oa_f, oa_b, ob_f, ob_b, ha_gain, gb_gain, w_ba, w_bb):
    b, t, _ = p_lat.shape
    d = w_ba.shape[1]
    bm = SUBLANES * GRID_W
    g0 = 6 * COL // d
    def o_spec(): return pl.BlockSpec((None, bm, COL), lambda bi, i: (bi, i, 0))
    def ob_spec(): return pl.BlockSpec((None, GRID_W, SUBLANES, COL), lambda bi, i: (bi, 0, i, 0))
    def p_spec(col): return pl.BlockSpec((None, bm, COL), lambda bi, i: (bi, i, col))
    def gate_spec(k): return pl.BlockSpec((None, bm, d), lambda bi, i: (bi, i, g0 + k))
    def full(shape): return _resident(shape, lambda bi, i: (0,) * len(shape))
    return pl.pallas_call(
        _merge_kernel,
        grid=(b, t // bm),
        in_specs=[o_spec(), o_spec(), p_spec(4), ob_spec(), ob_spec(), p_spec(5),
                  gate_spec(0), gate_spec(1), full((1, COL)), full((1, COL)),
                  full((COL, d)), full((COL, d))],
        out_specs=pl.BlockSpec((None, bm, d), lambda bi, i: (bi, i, 0)),
        out_shape=jax.ShapeDtypeStruct((b, t, d), BF16),
        compiler_params=_cparams("arbitrary", "arbitrary"),
        name="merge",
    )(oa_f, oa_b, p_lat, ob_f, ob_b, p_lat, p_lat, p_lat,
      ha_gain.reshape(1, COL), gb_gain.reshape(1, COL), w_ba, w_bb)


def _outproj_kernel(u_ref, wout_ref, x_ref, g_ref, nrm_ref, sh_ref, sc_ref, rwh_ref, rwl_ref,
                    x1_ref, h2_ref, afft_ref, acc0_ref, *, n_experts):
    acc0_ref[...] = jnp.zeros_like(acc0_ref)
    x1 = x_ref[...] + g_ref[...] * _dot(u_ref[...], wout_ref[...])
    x1_ref[...] = x1
    y = x1 * lax.rsqrt(jnp.mean(x1 * x1, axis=-1, keepdims=True) + NORM_EPS) * nrm_ref[...]
    h2 = y * (1.0 + sc_ref[...]) + sh_ref[...]
    h2_ref[...] = h2
    h_hi = h2.astype(BF16)
    h_lo = (h2 - h_hi.astype(F32)).astype(BF16)
    logits = _dot(h_hi, rwh_ref[...]) + _dot(h_hi, rwl_ref[...]) + _dot(h_lo, rwh_ref[...])
    lane = lax.broadcasted_iota(jnp.int32, logits.shape, 1)
    logits = jnp.where(lane < n_experts, logits, -jnp.inf)
    e = jnp.exp(logits - jnp.max(logits, axis=-1, keepdims=True))
    aff = e / jnp.sum(e, axis=-1, keepdims=True)
    afft_ref[...] = aff.T[:afft_ref.shape[0]]


def _outproj_router(u, w_out, x, gate, norm_gain, shift, scale, router_pad, n_experts, bm):
    b, t, d = x.shape
    ep = -(-n_experts // SUBLANES) * SUBLANES
    router_hi = router_pad.astype(BF16)
    router_lo = (router_pad - router_hi.astype(F32)).astype(BF16)
    def row(): return pl.BlockSpec((None, bm, d), lambda bi, i: (bi, i, 0))
    def mod(): return pl.BlockSpec((None, 1, d), lambda bi, i: (bi, 0, 0))
    return pl.pallas_call(
        functools.partial(_outproj_kernel, n_experts=n_experts),
        grid=(b, t // bm),
        in_specs=[row(), _resident((d, d), lambda bi, i: (0, 0)), row(), mod(),
                  pl.BlockSpec((1, d), lambda bi, i: (0, 0)), mod(), mod(),
                  _resident((d, LANES), lambda bi, i: (0, 0)), _resident((d, LANES), lambda bi, i: (0, 0))],
        out_specs=[row(), row(), pl.BlockSpec((None, ep, bm), lambda bi, i: (bi, 0, i)), row()],
        out_shape=[jax.ShapeDtypeStruct((b, t, d), F32), jax.ShapeDtypeStruct((b, t, d), F32),
                   jax.ShapeDtypeStruct((b, ep, t), F32), jax.ShapeDtypeStruct((b, t, d), F32)],
        compiler_params=_cparams("arbitrary", "arbitrary"),
        name="outproj_router",
    )(u, w_out, x, gate, norm_gain.reshape(1, d), shift, scale, router_hi, router_lo)


def _route_kernel(aff_ref, lst_ref, gate_ref, *, cap):
    bits = pltpu.bitcast(aff_ref[...], jnp.int32)
    rows, n_blk, _ = bits.shape
    lane = lax.broadcasted_iota(jnp.int32, bits.shape, 2)
    blk = lax.broadcasted_iota(jnp.int32, bits.shape, 1)

    def count(mask):
        return jnp.sum(mask.astype(jnp.int32), axis=(1, 2), keepdims=True)

    def refine(i, lo):
        cand = lo | lax.shift_left(jnp.int32(1), 30 - i)
        return jnp.where(count(bits >= cand) >= cap, cand, lo)

    cut = lax.fori_loop(0, 31, refine, jnp.zeros((rows, 1, 1), jnp.int32))

    def block_scan(m):
        x = m
        for k in range(LANES.bit_length() - 1):
            s = 1 << k
            x = x + jnp.where(lane >= s, pltpu.roll(x, s, 2), 0)
        return x, jnp.broadcast_to(x[:, :, LANES - 1:LANES], x.shape)

    def exclusive_prefix(m):
        incl, tot = block_scan(m)
        y = tot
        k = 0
        while (1 << k) < n_blk:
            s = 1 << k
            y = y + jnp.where(blk >= s, pltpu.roll(y, s, 1), 0)
            k += 1
        return incl - m + (y - tot)

    above = bits > cut
    at_cut = bits == cut
    need = cap - count(above)
    chosen = jnp.logical_or(above, jnp.logical_and(at_cut, exclusive_prefix(at_cut.astype(jnp.int32)) < need))
    token = blk * LANES + lane
    n_tok = n_blk * LANES

    def shifted(x, s):
        if s < LANES:
            y = pltpu.roll(x, LANES - s, 2)
            y = jnp.where(lane < LANES - s, y, pltpu.roll(y, n_blk - 1, 1)) if n_blk > 1 else y
        else:
            y = pltpu.roll(x, n_blk - s // LANES, 1)
        return jnp.where(token < n_tok - s, y, 0)

    m = chosen.astype(jnp.int32)
    val = jnp.where(chosen, token, 0)
    aff = jnp.where(chosen, bits, 0)
    dist = jnp.where(chosen, token - exclusive_prefix(m), 0)
    valid = m
    for k in range((n_tok - 1).bit_length()):
        s = 1 << k
        moving = valid * ((dist >> k) & 1)
        arrive = shifted(moving, s) == 1
        stay = (valid - moving) == 1
        val = jnp.where(arrive, shifted(val, s), jnp.where(stay, val, 0))
        aff = jnp.where(arrive, shifted(aff, s), jnp.where(stay, aff, 0))
        dist = jnp.where(arrive, shifted(dist, s), jnp.where(stay, dist, 0))
        valid = jnp.logical_or(arrive, stay).astype(jnp.int32)
    lst_ref[...] = val[:, :lst_ref.shape[1], :]
    gate_ref[...] = pltpu.bitcast(aff[:, :gate_ref.shape[1], :], F32)


def _route(aff_t, cap):
    r, t = aff_t.shape
    assert r % SUBLANES == 0 and t % LANES == 0 and cap % LANES == 0
    n_blk = t // LANES
    out_spec = pl.BlockSpec((SUBLANES, cap // LANES, LANES), lambda i: (i, 0, 0))
    return pl.pallas_call(
        functools.partial(_route_kernel, cap=cap),
        grid=(r // SUBLANES,),
        in_specs=[pl.BlockSpec((SUBLANES, n_blk, LANES), lambda i: (i, 0, 0))],
        out_specs=[out_spec, out_spec],
        out_shape=[jax.ShapeDtypeStruct((r, cap // LANES, LANES), jnp.int32),
                   jax.ShapeDtypeStruct((r, cap // LANES, LANES), F32)],
        compiler_params=_cparams("arbitrary"),
        name="route",
    )(aff_t.reshape(r, n_blk, LANES))


def _ffn_kernel(lst_ref, lst_next_ref, gate_ref, h_hbm, acc_in_hbm, wg_ref, wu_ref, wd_ref,
                acc_hbm, x_buf, a_buf, sem, *, cap, half):
    n_batch = pl.num_programs(1)
    b = pl.program_id(1)
    step = pl.program_id(0) * n_batch + b
    n_steps = pl.num_programs(0) * n_batch
    slot = step % 2
    other = 1 - slot
    b_next = jnp.where(b + 1 == n_batch, 0, b + 1)
    def for_chosen(lst, fn):
        def body(s8, carry):
            for r in range(SUBLANES):
                s = s8 * SUBLANES + r
                fn(s, lst[0, s])
            return carry
        lax.fori_loop(0, cap // SUBLANES, body, 0)

    def start_gather(lst, bb, sl):
        def one(s, n):
            pltpu.make_async_copy(h_hbm.at[bb, pl.ds(n, 1)], x_buf.at[sl, pl.ds(s, 1)], sem.at[0, sl]).start()
            pltpu.make_async_copy(acc_in_hbm.at[bb, pl.ds(n, 1)], a_buf.at[sl, pl.ds(s, 1)],
                                  sem.at[1, sl]).start()
        for_chosen(lst, one)

    def wait_gather(sl):
        pltpu.make_async_copy(h_hbm.at[b, pl.ds(0, cap)], x_buf.at[sl], sem.at[0, sl]).wait()
        pltpu.make_async_copy(acc_in_hbm.at[b, pl.ds(0, cap)], a_buf.at[sl], sem.at[1, sl]).wait()

    def start_scatter(sl):
        def one(s, n):
            pltpu.make_async_copy(a_buf.at[sl, pl.ds(s, 1)], acc_hbm.at[b, pl.ds(n, 1)], sem.at[2, sl]).start()
        for_chosen(lst_ref, one)

    def wait_scatter(sl):
        pltpu.make_async_copy(a_buf.at[sl], acc_hbm.at[b, pl.ds(0, cap)], sem.at[2, sl]).wait()

    @pl.when(step == 0)
    def _():
        start_gather(lst_ref, b, slot)

    wait_gather(slot)
    for r0 in range(0, cap, half):
        rows = pl.ds(r0, half)
        x = x_buf[slot, rows, :].astype(BF16)
        a = _dot(x, wg_ref[...])
        u = _dot(x, wu_ref[...])
        y = _dot((_silu(a) * u).astype(BF16), wd_ref[...]) * gate_ref[rows, :]
        a_buf[slot, rows, :] = a_buf[slot, rows, :] + y

    @pl.when(step > 0)
    def _():
        wait_scatter(other)

    @pl.when(step + 1 < n_steps)
    def _():
        start_gather(lst_next_ref, b_next, other)

    start_scatter(slot)

    @pl.when(step == n_steps - 1)
    def _():
        wait_scatter(slot)


def _expert_ffn(h2, acc0, lst, gates, w_gate, w_up, w_down):
    b, t, d = h2.shape
    e, _, f = w_gate.shape
    ep = lst.shape[0] // b
    cap = lst.shape[1] * LANES
    assert b >= 2
    half = cap // 2 if cap % 16 == 0 else cap

    def cur(ei, bi):
        return bi * ep + ei, 0, 0

    def nxt(ei, bi):
        nb = bi + 1
        return (nb % b) * ep + jnp.minimum(ei + nb // b, e - 1), 0, 0

    def smem(shape, index_map):
        return pl.BlockSpec(shape, index_map, memory_space=pltpu.SMEM)

    return pl.pallas_call(
        functools.partial(_ffn_kernel, cap=cap, half=half),
        grid=(e, b),
        in_specs=[smem((None, 1, cap), cur), smem((None, 1, cap), nxt),
                  pl.BlockSpec((None, cap, 1), cur),
                  pl.BlockSpec(memory_space=pl.ANY),
                  pl.BlockSpec(memory_space=pl.ANY),
                  pl.BlockSpec((None, d, f), lambda ei, bi: (ei, 0, 0)),
                  pl.BlockSpec((None, d, f), lambda ei, bi: (ei, 0, 0)),
                  pl.BlockSpec((None, f, d), lambda ei, bi: (ei, 0, 0))],
        out_specs=pl.BlockSpec(memory_space=pl.ANY),
        out_shape=jax.ShapeDtypeStruct((b, t, d), F32),
        scratch_shapes=[pltpu.VMEM((2, cap, d), F32), pltpu.VMEM((2, cap, d), F32),
                        pltpu.SemaphoreType.DMA((3, 2))],
        input_output_aliases={4: 0},
        compiler_params=_cparams("arbitrary", "arbitrary"),
        name="expert_ffn",
    )(lst.reshape(b * ep, 1, cap), lst.reshape(b * ep, 1, cap), gates.reshape(b * ep, cap, 1),
      h2, acc0, w_gate, w_up, w_down)


def _final_kernel(x1_ref, acc_ref, g_ref, nrm_ref, o_ref):
    x = x1_ref[...] + g_ref[...] * acc_ref[...]
    o_ref[...] = x * lax.rsqrt(jnp.mean(x * x, axis=-1, keepdims=True) + NORM_EPS) * nrm_ref[...]


def _final(x1, acc, gate, norm_gain, bm):
    b, t, d = x1.shape
    def row(): return pl.BlockSpec((None, bm, d), lambda bi, i: (bi, i, 0))
    return pl.pallas_call(
        _final_kernel,
        grid=(b, t // bm),
        in_specs=[row(), row(), pl.BlockSpec((None, 1, d), lambda bi, i: (bi, 0, 0)),
                  pl.BlockSpec((1, d), lambda bi, i: (0, 0))],
        out_specs=row(),
        out_shape=jax.ShapeDtypeStruct((b, t, d), F32),
        compiler_params=_cparams("arbitrary", "arbitrary"),
        name="final_norm",
    )(x1, acc, gate, norm_gain.reshape(1, d))


def _row_block(t, largest):
    for bm in (1024, 512, 256, 128, 64):
        if bm <= largest and t % bm == 0:
            return bm
    raise ValueError(f"token count {t} is not a multiple of 64")


def kernel(x, c, ctx, c_ctx, ada_w, ada_b, norm_mix, norm_ffn, w_in, gdn_conv, gdn_a_log, gdn_dt_bias,
           hgrn_lb, hgrn_norm, gdn_norm, w_branch_a, w_branch_b, w_out, router_w, w_gate, w_up, w_down,
           final_norm):
    depth = ada_w.shape[0]
    assert depth == 1, "single-layer stack"
    b, t, d = x.shape
    tc = ctx.shape[1]
    n_experts = router_w.shape[-1]
    assert d % COL == 0 and (6 * COL) % d == 0 and t == CHUNK * GRID_W and tc % CHUNK == 0
    assert w_in.shape[-1] == 9 * COL + 4 * HEADS + 2 * d and n_experts <= LANES

    lower = jnp.cumsum(jax.nn.softmax(hgrn_lb.astype(F32), axis=1), axis=1)

    c_rows = jnp.zeros((SUBLANES, d), F32).at[:b].set(c).at[b].set(c_ctx)
    mods = _modulation(c_rows, ada_w[0], ada_b[0])
    ml = [m.reshape(b, 1, d) for m in jnp.split(mods[:b], 6, axis=-1)]
    mc = [m.reshape(1, 1, d) for m in jnp.split(mods[b:b + 1], 6, axis=-1)]

    wi = w_in[0]
    s_ab = 9 * COL
    s_gates = s_ab + 4 * HEADS
    w_main = jnp.concatenate([wi[:, :5 * COL], wi[:, 8 * COL:s_ab], wi[:, s_gates:]], axis=1).astype(BF16)
    w_qkv = wi[:, 5 * COL:8 * COL].astype(BF16)
    w_ab = jnp.pad(wi[:, s_ab:s_gates], ((0, 0), (0, LANES - 4 * HEADS))).astype(BF16)

    bm_lat, bm_ctx = _row_block(t, 1024), _row_block(tc, 1024)
    (p_lat,) = _input_projection(x, norm_mix[0], ml[0], ml[1], w_main, None, BF16, bm_lat)
    qkv_lat, ab_lat = _input_projection(x, norm_mix[0], ml[0], ml[1], w_qkv, w_ab, F32, bm_lat)
    (p_ctx,) = _input_projection(ctx, norm_mix[0], mc[0], mc[1], w_main[:, :4 * COL], None, BF16, bm_ctx)
    qkv_ctx, ab_ctx = _input_projection(ctx, norm_mix[0], mc[0], mc[1], w_qkv, w_ab, F32, bm_ctx)

    oa_f, oa_b = _hgrn_scans(p_ctx, p_lat, lower[0, 0], lower[1, 0])

    x_ctx = _gdn_prep_ctx(qkv_ctx, gdn_conv[0])
    x_lat, ab_lat_cm = _gdn_prep_lat(qkv_lat, ab_lat, gdn_conv[0])
    pad = LANES - 2 * HEADS
    alog_row = jnp.pad(gdn_a_log[0].reshape(1, 2 * HEADS).astype(F32), ((0, 0), (0, pad)))
    dtb_row = jnp.pad(gdn_dt_bias[0].reshape(1, 2 * HEADS).astype(F32), ((0, 0), (0, pad)))
    ob_f, ob_b = _gdn_scans(x_ctx, x_lat, ab_ctx.reshape(b, tc // CHUNK, CHUNK, LANES), ab_lat_cm,
                            alog_row, dtb_row)

    u = _merge(p_lat, oa_f, oa_b, ob_f, ob_b, hgrn_norm[0], gdn_norm[0],
               w_branch_a[0].astype(BF16), w_branch_b[0].astype(BF16))
    router_pad = jnp.pad(router_w[0].astype(F32), ((0, 0), (0, LANES - n_experts)))
    bm_row = _row_block(t, 256)
    x1, h2, aff_t, acc0 = _outproj_router(u, w_out[0].astype(BF16), x, ml[2], norm_ffn[0], ml[3], ml[4],
                                          router_pad, n_experts, bm_row)

    cap = max(1, (CAPACITY_FACTOR * t) // n_experts)
    lst, gates = _route(aff_t.reshape(-1, t), cap)
    acc = _expert_ffn(h2, acc0, lst, gates, w_gate[0].astype(BF16), w_up[0].astype(BF16),
                      w_down[0].astype(BF16))
    return _final(x1, acc, ml[5], final_norm, bm_row)
```

```python
import functools

import jax
import jax.numpy as jnp
from jax import lax
from jax.experimental import pallas as pl
from jax.experimental.pallas import tpu as pltpu

F32 = jnp.float32
BF16 = jnp.bfloat16

NORM_EPS = 1e-6
LOG2E = 1.4426950408889634
GRID_W = 64
HEADS = 8
HEAD_DIM = 128
CHUNK = 64
CONV_K = 5
CAPACITY_FACTOR = 2
COL = HEADS * HEAD_DIM
LANES = 128
SUBLANES = 8
GROUP = SUBLANES
INV_BLOCK = 8
HGRN_SAMPLES_PER_STEP = 2
VMEM_LIMIT_BYTES = 56 * 1024 * 1024


def _cparams(*sem):
    return pltpu.CompilerParams(dimension_semantics=sem, vmem_limit_bytes=VMEM_LIMIT_BYTES)


def _dot(a, b):
    return jnp.dot(a, b, preferred_element_type=F32)


def _dot_nt(a, b):
    return lax.dot_general(a, b, (((1,), (1,)), ((), ())), preferred_element_type=F32)


def _silu(x):
    return x * jax.nn.sigmoid(x)


def _scan_cumsum(g, rev, terms):
    tri = jnp.where(_tri(rev), 1.0, 0.0).astype(BF16)
    total = None
    rest = g
    for _ in range(terms):
        part = rest.astype(BF16)
        rest = rest - part.astype(F32)
        total = _dot(tri, part) if total is None else total + _dot(tri, part)
    return total


def _resident(shape, index_map):
    return pl.BlockSpec(shape, index_map, pipeline_mode=pl.Buffered(1))


def _mod_kernel(c_ref, w_ref, b_ref, o_ref):
    o_ref[...] = _dot(_silu(c_ref[...]), w_ref[...]) + b_ref[...]


def _modulation(c_rows, ada_w, ada_b):
    rows, d = c_rows.shape
    n = ada_w.shape[1]
    bn = 1024
    return pl.pallas_call(
        _mod_kernel,
        grid=(n // bn,),
        in_specs=[pl.BlockSpec((rows, d), lambda j: (0, 0)),
                  pl.BlockSpec((d, bn), lambda j: (0, j)),
                  pl.BlockSpec((1, bn), lambda j: (0, j))],
        out_specs=pl.BlockSpec((rows, bn), lambda j: (0, j)),
        out_shape=jax.ShapeDtypeStruct((rows, n), F32),
        compiler_params=_cparams("arbitrary"),
        name="modulation",
    )(c_rows, ada_w, ada_b.reshape(1, n))


def _proj_kernel(x_ref, gain_ref, shift_ref, scale_ref, w_ref, *rest, with_ab):
    if with_ab:
        wab_ref, o_ref, oab_ref, h_ref = rest
    else:
        o_ref, h_ref = rest

    @pl.when(pl.program_id(2) == 0)
    def _():
        x = x_ref[...]
        rs = lax.rsqrt(jnp.mean(x * x, axis=-1, keepdims=True) + NORM_EPS)
        h = ((x * rs) * (gain_ref[...] * (1.0 + scale_ref[...])) + shift_ref[...]).astype(BF16)
        h_ref[...] = h
        if with_ab:
            oab_ref[...] = _dot(h, wab_ref[...])

    o_ref[...] = _dot(h_ref[...], w_ref[...]).astype(o_ref.dtype)


def _input_projection(x, gain, shift, scale, w, w_ab, out_dtype, bm, bn=COL):
    b, t, d = x.shape
    n = w.shape[1]
    assert n % bn == 0
    per_batch = shift.shape[0] > 1
    mod_map = (lambda bi, i, j: (bi, 0, 0)) if per_batch else (lambda bi, i, j: (0, 0, 0))
    with_ab = w_ab is not None
    in_specs = [pl.BlockSpec((None, bm, d), lambda bi, i, j: (bi, i, 0)),
                pl.BlockSpec((1, d), lambda bi, i, j: (0, 0)),
                pl.BlockSpec((None, 1, d), mod_map),
                pl.BlockSpec((None, 1, d), mod_map),
                pl.BlockSpec((d, bn), lambda bi, i, j: (0, j))]
    out_specs = [pl.BlockSpec((None, bm, bn), lambda bi, i, j: (bi, i, j))]
    out_shape = [jax.ShapeDtypeStruct((b, t, n), out_dtype)]
    args = [x, gain.reshape(1, d), shift, scale, w]
    if with_ab:
        in_specs.append(pl.BlockSpec((d, LANES), lambda bi, i, j: (0, 0)))
        out_specs.append(pl.BlockSpec((None, bm, LANES), lambda bi, i, j: (bi, i, 0)))
        out_shape.append(jax.ShapeDtypeStruct((b, t, LANES), F32))
        args.append(w_ab)
    return pl.pallas_call(
        functools.partial(_proj_kernel, with_ab=with_ab),
        grid=(b, t // bm, n // bn),
        in_specs=in_specs, out_specs=out_specs, out_shape=out_shape,
        scratch_shapes=[pltpu.VMEM((bm, d), BF16)],
        compiler_params=_cparams("arbitrary", "arbitrary", "arbitrary"),
        name="input_projection",
    )(*args)


def _tri(rev, strict=False):
    ri = lax.broadcasted_iota(jnp.int32, (CHUNK, CHUNK), 0)
    ci = lax.broadcasted_iota(jnp.int32, (CHUNK, CHUNK), 1)
    if rev:
        return (ci > ri) if strict else (ci >= ri)
    return (ci < ri) if strict else (ci <= ri)


def _boundary_rows(g, bs, rev):
    n, w = g.shape
    off = (bs >> 1) - 1 + (1 if rev else 0)
    if bs >= SUBLANES:
        pieces = [jnp.broadcast_to(g[p * bs + off:p * bs + off + 1], (bs, w)) for p in range(n // bs)]
        return jnp.concatenate(pieces, axis=0)
    sub = lax.broadcasted_iota(jnp.int32, (SUBLANES, w), 0)
    pieces = []
    for r0 in range(0, n, SUBLANES):
        acc = None
        for p in range(SUBLANES // bs):
            r = r0 + p * bs + off
            val = jnp.broadcast_to(g[r:r + 1], (SUBLANES, w))
            acc = val if acc is None else jnp.where(sub >= p * bs, val, acc)
        pieces.append(acc)
    return jnp.concatenate(pieces, axis=0)


def _hgrn_chunks(scans):
    row = lax.broadcasted_iota(jnp.int32, (CHUNK, 1), 0)
    ri = lax.broadcasted_iota(jnp.int32, (CHUNK, CHUNK), 0)
    ci = lax.broadcasted_iota(jnp.int32, (CHUNK, CHUNK), 1)
    n_levels = CHUNK.bit_length() - 1
    eye = ri == ci

    units = []
    for q_raw, f_raw, v, low, s_ref, rev in scans:
        q = _silu(q_raw.astype(F32))
        f_raw = f_raw.astype(F32)
        one_m = 1.0 - low
        f = low + one_m * jax.nn.sigmoid(f_raw)
        k = one_m * jax.nn.sigmoid(-f_raw)
        gc = _scan_cumsum(jnp.log(f), rev, 2) * LOG2E
        g_last = gc[0:1] if rev else gc[CHUNK - 1:CHUNK]
        q_dec = (q * jnp.exp2(gc)).astype(BF16)
        k_dec = (k * jnp.exp2(g_last - gc)).astype(BF16)
        mixed, pair = [], []
        for lv in range(1, n_levels + 1):
            upper = ((row >> (lv - 1)) & 1) == 1
            q_rows = jnp.logical_not(upper) if rev else upper
            decay = jnp.exp2(-jnp.abs(gc - _boundary_rows(gc, 1 << lv, rev)))
            mixed.append((jnp.where(q_rows, q, k) * decay).astype(BF16))
            i_upper = ((ri >> (lv - 1)) & 1) == 1
            j_upper = ((ci >> (lv - 1)) & 1) == 1
            halves = jnp.logical_and(j_upper, jnp.logical_not(i_upper)) if rev else \
                jnp.logical_and(i_upper, jnp.logical_not(j_upper))
            pair.append(jnp.logical_and((ri >> lv) == (ci >> lv), halves))
        qk = q * k
        v_b = v.astype(BF16)
        v_f = v.astype(F32)
        for h in range(HEADS):
            sl = slice(h * HEAD_DIM, (h + 1) * HEAD_DIM)
            units.append(dict(s_ref=s_ref, h=h, diag=jnp.sum(qk[:, sl], axis=1, keepdims=True),
                              mixed=[x[:, sl] for x in mixed], pair=pair,
                              q_dec=q_dec[:, sl], k_dec=k_dec[:, sl], v_b=v_b[:, sl], v_f=v_f[:, sl],
                              decay=jnp.exp2(g_last[:, sl])))

    scores = [[_dot_nt(m, m) for m in u["mixed"]] for u in units]
    states = [u["s_ref"][u["h"]] for u in units]
    carried = [_dot_nt(u["q_dec"], s.astype(BF16)) for u, s in zip(units, states)]
    outs = []
    for u, sc, car in zip(units, scores, carried):
        a = jnp.where(eye, u["diag"], 0.0)
        for lv in range(n_levels):
            a = jnp.where(u["pair"][lv], sc[lv], a)
        outs.append(_dot(a.astype(BF16), u["v_b"]) + car)
    for u, s in zip(units, states):
        u["s_ref"][u["h"]] = u["decay"] * s + _dot(u["v_f"].T.astype(BF16), u["k_dec"])
    return [jnp.concatenate(outs[i * HEADS:(i + 1) * HEADS], axis=1) for i in range(len(scans))]


def _hgrn_kernel(lowf_ref, lowb_ref,
                 cqf, cff, cvf, cqb, cfb, cvb,
                 lqf, lff, lvf, lqb, lfb, lvb,
                 of_ref, ob_ref, sf_ref, sb_ref, *, n_ctx, per_step):
    t = pl.program_id(1)
    n_samples = of_ref.shape[0]

    @pl.when(t == 0)
    def _():
        sf_ref[...] = jnp.zeros_like(sf_ref)
        sb_ref[...] = jnp.zeros_like(sb_ref)

    is_ctx = t < n_ctx

    def body(j, carry):
        rows_f = pl.ds(pl.multiple_of(j * CHUNK, CHUNK), CHUNK)
        rows_b = pl.ds(pl.multiple_of((per_step - 1 - j) * CHUNK, CHUNK), CHUNK)

        def pick(c_ref, l_ref, i, rows):
            return jnp.where(is_ctx, c_ref[i, rows, :], l_ref[i, rows, :])

        scans = []
        for i in range(n_samples):
            scans.append((pick(cqf, lqf, i, rows_f), pick(cff, lff, i, rows_f), pick(cvf, lvf, i, rows_f),
                          lowf_ref[...], sf_ref.at[i], False))
            scans.append((pick(cqb, lqb, i, rows_b), pick(cfb, lfb, i, rows_b), pick(cvb, lvb, i, rows_b),
                          lowb_ref[...], sb_ref.at[i], True))
        outs = _hgrn_chunks(scans)
        for i in range(n_samples):
            of_ref[i, rows_f, :] = outs[2 * i].astype(of_ref.dtype)
            ob_ref[i, rows_b, :] = outs[2 * i + 1].astype(ob_ref.dtype)
        return carry

    lax.fori_loop(0, per_step, body, 0)


def _hgrn_scans(p_ctx, p_lat, low_f, low_b):
    b, tc, _ = p_ctx.shape
    t = p_lat.shape[1]
    per_step = max(g for g in (4, 2, 1) if (tc // CHUNK) % g == 0 and (t // CHUNK) % g == 0)
    rows = per_step * CHUNK
    n_ctx, n_lat = tc // rows, t // rows

    def cf(ti): return jnp.minimum(ti, n_ctx - 1)
    def cb(ti): return jnp.maximum(n_ctx - 1 - ti, 0)
    def lf(ti): return jnp.maximum(ti - n_ctx, 0)
    def lb(ti): return n_lat - 1 - jnp.maximum(ti - n_ctx, 0)

    nb = HGRN_SAMPLES_PER_STEP if b % HGRN_SAMPLES_PER_STEP == 0 else 1

    def spec(block_of, col):
        return pl.BlockSpec((nb, rows, COL), lambda bi, ti: (bi, block_of(ti), col))

    low_spec = pl.BlockSpec((1, COL), lambda bi, ti: (0, 0))
    out_f, out_b = pl.pallas_call(
        functools.partial(_hgrn_kernel, n_ctx=n_ctx, per_step=per_step),
        grid=(b // nb, n_ctx + n_lat),
        in_specs=[low_spec, low_spec,
                  spec(cf, 0), spec(cf, 1), spec(cf, 3), spec(cb, 0), spec(cb, 2), spec(cb, 3),
                  spec(lf, 0), spec(lf, 1), spec(lf, 3), spec(lb, 0), spec(lb, 2), spec(lb, 3)],
        out_specs=[pl.BlockSpec((nb, rows, COL), lambda bi, ti: (bi, lf(ti), 0)),
                   pl.BlockSpec((nb, rows, COL), lambda bi, ti: (bi, lb(ti), 0))],
        out_shape=[jax.ShapeDtypeStruct((b, t, COL), BF16)] * 2,
        scratch_shapes=[pltpu.VMEM((nb, HEADS, HEAD_DIM, HEAD_DIM), F32)] * 2,
        compiler_params=_cparams("arbitrary", "arbitrary"),
        name="hgrn_scans",
    )(low_f.reshape(1, COL), low_b.reshape(1, COL), *([p_ctx] * 6), *([p_lat] * 6))
    return out_f, out_b


def _conv_silu_norm(x, prev, nxt, w, idx):
    row = lax.broadcasted_iota(jnp.int32, (CHUNK, 1), 0)
    xm1 = jnp.where(row == 0, prev[7:8], pltpu.roll(x, 1, 0))
    xm2 = jnp.where(row == 0, prev[6:7], jnp.where(row == 1, prev[7:8], pltpu.roll(x, 2, 0)))
    xp1 = jnp.where(row == CHUNK - 1, nxt[0:1], pltpu.roll(x, CHUNK - 1, 0))
    xp2 = jnp.where(row == CHUNK - 1, nxt[1:2],
                    jnp.where(row == CHUNK - 2, nxt[0:1], pltpu.roll(x, CHUNK - 2, 0)))
    return _silu_norm(w[0:1] * xm2 + w[1:2] * xm1 + w[2:3] * x + w[3:4] * xp1 + w[4:5] * xp2, idx)


def _silu_norm(y, idx):
    y = _silu(y)
    if idx == 2:
        return y
    scale = HEAD_DIM ** -0.5 if idx == 0 else 1.0
    pieces = []
    for h in range(HEADS):
        yh = y[:, h * HEAD_DIM:(h + 1) * HEAD_DIM]
        pieces.append(yh * (lax.rsqrt(jnp.sum(yh * yh, axis=1, keepdims=True) + NORM_EPS) * scale))
    return jnp.concatenate(pieces, axis=1)


def _gdn_prep_ctx_kernel(q_ref, k_ref, v_ref, pq_ref, pk_ref, pv_ref, nq_ref, nk_ref, nv_ref, w_ref, o_ref):
    t = pl.program_id(1)
    first = t == 0
    last = t == pl.num_programs(1) - 1
    for idx, (c_ref, p_ref, n_ref) in enumerate(((q_ref, pq_ref, nq_ref), (k_ref, pk_ref, nk_ref),
                                                 (v_ref, pv_ref, nv_ref))):
        prev = jnp.where(first, 0.0, p_ref[...])
        nxt = jnp.where(last, 0.0, n_ref[...])
        y = _conv_silu_norm(c_ref[...], prev, nxt, w_ref[:, idx * COL:(idx + 1) * COL], idx)
        o_ref[:, idx * COL:(idx + 1) * COL] = y.astype(o_ref.dtype)


def _gdn_prep_ctx(qkv, conv_w):
    b, t, _ = qkv.shape
    n_chunks = t // CHUNK
    rb = CHUNK // SUBLANES
    def cur(col): return pl.BlockSpec((None, CHUNK, COL), lambda bi, c: (bi, c, col))
    def prv(col): return pl.BlockSpec((None, SUBLANES, COL), lambda bi, c: (bi, jnp.maximum(c * rb - 1, 0), col))
    def nxt(col): return pl.BlockSpec((None, SUBLANES, COL),
                                      lambda bi, c: (bi, jnp.minimum((c + 1) * rb, n_chunks * rb - 1), col))
    cols = (0, 1, 2)
    return pl.pallas_call(
        _gdn_prep_ctx_kernel,
        grid=(b, n_chunks),
        in_specs=[cur(c) for c in cols] + [prv(c) for c in cols] + [nxt(c) for c in cols]
                 + [pl.BlockSpec((CONV_K, 3 * COL), lambda bi, c: (0, 0))],
        out_specs=pl.BlockSpec((None, None, CHUNK, 3 * COL), lambda bi, c: (bi, c, 0, 0)),
        out_shape=jax.ShapeDtypeStruct((b, n_chunks, CHUNK, 3 * COL), BF16),
        compiler_params=_cparams("arbitrary", "arbitrary"),
        name="gdn_prep_ctx",
    )(*([qkv] * 9), conv_w)


def _gdn_prep_lat_kernel(q_ref, k_ref, v_ref, pq_ref, pk_ref, pv_ref, nq_ref, nk_ref, nv_ref, ab_ref, w_ref,
                         o_ref, oab_ref):
    g = pl.program_id(1)
    first = g == 0
    last = g == pl.num_programs(1) - 1
    tail = CHUNK - SUBLANES
    for idx, (c_ref, p_ref, n_ref) in enumerate(((q_ref, pq_ref, nq_ref), (k_ref, pk_ref, nk_ref),
                                                 (v_ref, pv_ref, nv_ref))):
        w = w_ref[:, idx * COL:(idx + 1) * COL]
        cols = [c_ref[:, j, :] for j in range(GROUP)]
        before = jnp.where(first, 0.0, p_ref[:, GROUP - 1, :])
        after = jnp.where(last, 0.0, n_ref[:, 0, :])
        for j in range(GROUP):
            prev = cols[j - 1][tail:] if j > 0 else before
            nxt = cols[j + 1][:SUBLANES] if j < GROUP - 1 else after
            o_ref[j, :, idx * COL:(idx + 1) * COL] = _conv_silu_norm(cols[j], prev, nxt, w, idx).astype(o_ref.dtype)
    for j in range(GROUP):
        oab_ref[j] = ab_ref[:, j, :]


def _gdn_prep_lat(qkv, ab, conv_w):
    b, t, _ = qkv.shape
    assert t == CHUNK * GRID_W and GRID_W % GROUP == 0
    n_groups = GRID_W // GROUP
    rb = CHUNK // SUBLANES
    q4 = qkv.reshape(b, CHUNK, GRID_W, 3 * COL)
    ab4 = ab.reshape(b, CHUNK, GRID_W, LANES)
    def cur(col): return pl.BlockSpec((None, CHUNK, GROUP, COL), lambda bi, g: (bi, 0, g, col))
    def prv(col): return pl.BlockSpec((None, SUBLANES, GROUP, COL),
                                      lambda bi, g: (bi, rb - 1, jnp.maximum(g - 1, 0), col))
    def nxt(col): return pl.BlockSpec((None, SUBLANES, GROUP, COL),
                                      lambda bi, g: (bi, 0, jnp.minimum(g + 1, n_groups - 1), col))
    cols = (0, 1, 2)
    return pl.pallas_call(
        _gdn_prep_lat_kernel,
        grid=(b, n_groups),
        in_specs=[cur(c) for c in cols] + [prv(c) for c in cols] + [nxt(c) for c in cols]
                 + [pl.BlockSpec((None, CHUNK, GROUP, LANES), lambda bi, g: (bi, 0, g, 0)),
                    pl.BlockSpec((CONV_K, 3 * COL), lambda bi, g: (0, 0))],
        out_specs=[pl.BlockSpec((None, GROUP, CHUNK, 3 * COL), lambda bi, g: (bi, g, 0, 0)),
                   pl.BlockSpec((None, GROUP, CHUNK, LANES), lambda bi, g: (bi, g, 0, 0))],
        out_shape=[jax.ShapeDtypeStruct((b, GRID_W, CHUNK, 3 * COL), BF16),
                   jax.ShapeDtypeStruct((b, GRID_W, CHUNK, LANES), F32)],
        compiler_params=_cparams("arbitrary", "arbitrary"),
        name="gdn_prep_lat",
    )(*([q4] * 9), ab4, conv_w)


def _mm_bf16(a, b):
    return _dot(a.astype(BF16), b.astype(BF16))


def _unit_tri_inverses(ns):
    ri = lax.broadcasted_iota(jnp.int32, (CHUNK, CHUNK), 0)
    ci = lax.broadcasted_iota(jnp.int32, (CHUNK, CHUNK), 1)
    eye = jnp.where(ri == ci, 1.0, 0.0)
    shift = INV_BLOCK.bit_length() - 1
    on_diag = (ri >> shift) == (ci >> shift)

    def nilpotent_inverses(ms, degree):
        xs = [eye - m for m in ms]
        ps = ms
        for _ in range(degree.bit_length() - 2):
            ps = [_mm_bf16(p, p) for p in ps]
            xs = [x + _mm_bf16(x, p) for x, p in zip(xs, ps)]
        return xs

    d_invs = nilpotent_inverses([jnp.where(on_diag, n, 0.0) for n in ns], INV_BLOCK)
    ms = [_mm_bf16(d, jnp.where(on_diag, 0.0, n)) for d, n in zip(d_invs, ns)]
    ys = nilpotent_inverses(ms, CHUNK // INV_BLOCK)
    return [_mm_bf16(y, d) for y, d in zip(ys, d_invs)]


def _gdn_chunks(scans, alog, dtb):
    units = []
    for x, ab, s_ref, rev in scans:
        g_all = -jnp.exp(alog) * jax.nn.softplus(ab + dtb)
        beta_all = jax.nn.sigmoid(ab)
        gc_all = _scan_cumsum(g_all, rev, 3)
        gr_all = gc_all.T
        incl = _tri(rev)
        strict = _tri(rev, strict=True)
        off = HEADS if rev else 0
        for h in range(HEADS):
            gcol = gc_all[:, off + h:off + h + 1]
            grow = gr_all[off + h:off + h + 1, :]
            units.append(dict(
                s_ref=s_ref, h=h, strict=strict, gcol=gcol,
                beta=beta_all[:, 2 * HEADS + off + h:2 * HEADS + off + h + 1],
                g_last=gcol[0:1] if rev else gcol[CHUNK - 1:CHUNK],
                gamma=jnp.where(incl, jnp.exp(jnp.where(incl, gcol - grow, 0.0)), 0.0),
                q=x[:, h * HEAD_DIM:(h + 1) * HEAD_DIM],
                k=x[:, COL + h * HEAD_DIM:COL + (h + 1) * HEAD_DIM],
                v=x[:, 2 * COL + h * HEAD_DIM:2 * COL + (h + 1) * HEAD_DIM]))

    both = [_dot_nt(jnp.concatenate([u["q"], u["k"]], axis=0), u["k"]) for u in units]
    qks = [x[:CHUNK] for x in both]
    ns = [jnp.where(u["strict"], u["beta"] * x[CHUNK:] * u["gamma"], 0.0) for u, x in zip(units, both)]
    invs = _unit_tri_inverses(ns)
    rhss = [jnp.concatenate([u["v"].astype(F32) * u["beta"],
                             u["k"].astype(F32) * (u["beta"] * jnp.exp(u["gcol"]))], axis=1) for u in units]
    sols = [_mm_bf16(a, r) for a, r in zip(invs, rhss)]
    states = [u["s_ref"][u["h"]] for u in units]
    states_b = [s.astype(BF16) for s in states]
    v_news = [sol[:, :HEAD_DIM] - _dot_nt(sol[:, HEAD_DIM:].astype(BF16), s_b) for sol, s_b in zip(sols, states_b)]
    outs = [_dot_nt((u["q"].astype(F32) * jnp.exp(u["gcol"])).astype(BF16), s_b)
            + _mm_bf16(qk * u["gamma"], v_new)
            for u, qk, s_b, v_new in zip(units, qks, states_b, v_news)]
    for u, s, v_new in zip(units, states, v_news):
        k_dec = (u["k"].astype(F32) * jnp.exp(u["g_last"] - u["gcol"])).astype(BF16)
        u["s_ref"][u["h"]] = jnp.exp(u["g_last"]) * s + _dot(v_new.T.astype(BF16), k_dec)
    return [jnp.concatenate(outs[i * HEADS:(i + 1) * HEADS], axis=1) for i in range(len(scans))]


def _gdn_ctx_kernel(alog_ref, dtb_ref, x_ref, ab_ref, sf_ref, sb_ref, *, n_chunks):
    sf_ref[...] = jnp.zeros_like(sf_ref)
    sb_ref[...] = jnp.zeros_like(sb_ref)

    def body(j, carry):
        jb = n_chunks - 1 - j
        _gdn_chunks([(x_ref[j], ab_ref[j], sf_ref, False), (x_ref[jb], ab_ref[jb], sb_ref, True)],
                    alog_ref[...], dtb_ref[...])
        return carry

    lax.fori_loop(0, n_chunks, body, 0)


def _gdn_lat_kernel(alog_ref, dtb_ref, s0f_ref, s0b_ref, xf_ref, abf_ref, xb_ref, abb_ref,
                    of_ref, ob_ref, sf_ref, sb_ref):
    n_samples, per_step = xf_ref.shape[:2]

    @pl.when(pl.program_id(1) == 0)
    def _():
        sf_ref[...] = s0f_ref[...]
        sb_ref[...] = s0b_ref[...]

    def body(j, carry):
        jb = per_step - 1 - j
        scans = []
        for i in range(n_samples):
            scans.append((xf_ref[i, j], abf_ref[i, j], sf_ref.at[i], False))
            scans.append((xb_ref[i, jb], abb_ref[i, jb], sb_ref.at[i], True))
        outs = _gdn_chunks(scans, alog_ref[...], dtb_ref[...])
        for i in range(n_samples):
            of_ref[i, j] = outs[2 * i]
            ob_ref[i, jb] = outs[2 * i + 1]
        return carry

    lax.fori_loop(0, per_step, body, 0)


def _gdn_scans(x_ctx, x_lat, ab_ctx, ab_lat, alog_row, dtb_row):
    b, n_ctx = x_ctx.shape[:2]
    n_lat = x_lat.shape[1]
    state_shape = jax.ShapeDtypeStruct((b, HEADS, HEAD_DIM, HEAD_DIM), F32)
    def state_spec(nidx):
        return pl.BlockSpec((None, HEADS, HEAD_DIM, HEAD_DIM), lambda *i: (i[0], 0, 0, 0))
    s_f, s_b = pl.pallas_call(
        functools.partial(_gdn_ctx_kernel, n_chunks=n_ctx),
        grid=(b,),
        in_specs=[pl.BlockSpec((1, LANES), lambda bi: (0, 0)), pl.BlockSpec((1, LANES), lambda bi: (0, 0)),
                  pl.BlockSpec((None, n_ctx, CHUNK, 3 * COL), lambda bi: (bi, 0, 0, 0)),
                  pl.BlockSpec((None, n_ctx, CHUNK, LANES), lambda bi: (bi, 0, 0, 0))],
        out_specs=[state_spec(1), state_spec(1)],
        out_shape=[state_shape, state_shape],
        compiler_params=_cparams("arbitrary"),
        name="gdn_ctx_scans",
    )(alog_row, dtb_row, x_ctx, ab_ctx)

    nb = 2 if b % 2 == 0 else 1
    per_step = max(g for g in (4, 2, 1) if n_lat % g == 0)
    n_steps = n_lat // per_step
    def gf(ti): return ti
    def gb(ti): return n_steps - 1 - ti
    def lat_state_spec():
        return pl.BlockSpec((nb, HEADS, HEAD_DIM, HEAD_DIM), lambda bi, ti: (bi, 0, 0, 0))
    def xspec(group_of):
        return pl.BlockSpec((nb, per_step, CHUNK, 3 * COL), lambda bi, ti: (bi, group_of(ti), 0, 0))
    def abspec(group_of):
        return pl.BlockSpec((nb, per_step, CHUNK, LANES), lambda bi, ti: (bi, group_of(ti), 0, 0))
    def ospec(group_of):
        return pl.BlockSpec((nb, per_step, CHUNK, COL), lambda bi, ti: (bi, group_of(ti), 0, 0))
    row_spec = pl.BlockSpec((1, LANES), lambda bi, ti: (0, 0))
    return pl.pallas_call(
        _gdn_lat_kernel,
        grid=(b // nb, n_steps),
        in_specs=[row_spec, row_spec, lat_state_spec(), lat_state_spec(),
                  xspec(gf), abspec(gf), xspec(gb), abspec(gb)],
        out_specs=[ospec(gf), ospec(gb)],
        out_shape=[jax.ShapeDtypeStruct((b, n_lat, CHUNK, COL), F32)] * 2,
        scratch_shapes=[pltpu.VMEM((nb, HEADS, HEAD_DIM, HEAD_DIM), F32)] * 2,
        compiler_params=_cparams("arbitrary", "arbitrary"),
        name="gdn_lat_scans",
    )(alog_row, dtb_row, s_f, s_b, x_lat, ab_lat, x_lat, ab_lat)


def _head_norm(x):
    pieces = []
    for h in range(HEADS):
        xh = x[:, h * HEAD_DIM:(h + 1) * HEAD_DIM]
        pieces.append(xh * lax.rsqrt(jnp.mean(xh * xh, axis=1, keepdims=True) + NORM_EPS))
    return jnp.concatenate(pieces, axis=1)


def _merge_kernel(oaf, oab, og, obf, obb, z, ga, gb, hag, gbg, wba, wbb, u_ref):
    o_a = oaf[...].astype(F32) + oab[...].astype(F32)
    y_a = _head_norm(o_a * jax.nn.sigmoid(og[...].astype(F32))) * hag[...]
    o_b = jnp.concatenate([obf[:, r, :] + obb[:, r, :] for r in range(obf.shape[1])], axis=0)
    y_b = _head_norm(o_b) * gbg[...] * _silu(z[...].astype(F32))
    u = jax.nn.sigmoid(ga[...].astype(F32)) * _dot(y_a.astype(BF16), wba[...])
    u = u + jax.nn.sigmoid(gb[...].astype(F32)) * _dot(y_b.astype(BF16), wbb[...])
    u_ref[...] = u.astype(u_ref.dtype)


def _merge(p_lat, oa_f, oa_b, ob_f, ob_b, ha_gain, gb_gain, w_ba, w_bb):
    b, t, _ = p_lat.shape
    d = w_ba.shape[1]
    bm = SUBLANES * GRID_W
    g0 = 6 * COL // d
    def o_spec(): return pl.BlockSpec((None, bm, COL), lambda bi, i: (bi, i, 0))
    def ob_spec(): return pl.BlockSpec((None, GRID_W, SUBLANES, COL), lambda bi, i: (bi, 0, i, 0))
    def p_spec(col): return pl.BlockSpec((None, bm, COL), lambda bi, i: (bi, i, col))
    def gate_spec(k): return pl.BlockSpec((None, bm, d), lambda bi, i: (bi, i, g0 + k))
    def full(shape): return _resident(shape, lambda bi, i: (0,) * len(shape))
    return pl.pallas_call(
        _merge_kernel,
        grid=(b, t // bm),
        in_specs=[o_spec(), o_spec(), p_spec(4), ob_spec(), ob_spec(), p_spec(5),
                  gate_spec(0), gate_spec(1), full((1, COL)), full((1, COL)),
                  full((COL, d)), full((COL, d))],
        out_specs=pl.BlockSpec((None, bm, d), lambda bi, i: (bi, i, 0)),
        out_shape=jax.ShapeDtypeStruct((b, t, d), BF16),
        compiler_params=_cparams("arbitrary", "arbitrary"),
        name="merge",
    )(oa_f, oa_b, p_lat, ob_f, ob_b, p_lat, p_lat, p_lat,
      ha_gain.reshape(1, COL), gb_gain.reshape(1, COL), w_ba, w_bb)


def _outproj_kernel(u_ref, wout_ref, x_ref, g_ref, nrm_ref, sh_ref, sc_ref, rwh_ref, rwl_ref,
                    x1_ref, h2_ref, afft_ref, acc0_ref, *, n_experts):
    acc0_ref[...] = jnp.zeros_like(acc0_ref)
    x1 = x_ref[...] + g_ref[...] * _dot(u_ref[...], wout_ref[...])
    x1_ref[...] = x1
    rs = lax.rsqrt(jnp.mean(x1 * x1, axis=-1, keepdims=True) + NORM_EPS)
    h2 = (x1 * rs) * (nrm_ref[...] * (1.0 + sc_ref[...])) + sh_ref[...]
    h2_ref[...] = h2
    h_hi = h2.astype(BF16)
    h_lo = (h2 - h_hi.astype(F32)).astype(BF16)
    logits = _dot(h_hi, rwh_ref[...]) + _dot(h_hi, rwl_ref[...]) + _dot(h_lo, rwh_ref[...])
    lane = lax.broadcasted_iota(jnp.int32, logits.shape, 1)
    logits = jnp.where(lane < n_experts, logits, -jnp.inf)
    e = jnp.exp(logits - jnp.max(logits, axis=-1, keepdims=True))
    aff = e / jnp.sum(e, axis=-1, keepdims=True)
    afft_ref[...] = aff.T[:afft_ref.shape[0]]


def _outproj_router(u, w_out, x, gate, norm_gain, shift, scale, router_pad, n_experts, bm):
    b, t, d = x.shape
    ep = -(-n_experts // SUBLANES) * SUBLANES
    router_hi = router_pad.astype(BF16)
    router_lo = (router_pad - router_hi.astype(F32)).astype(BF16)
    def row(): return pl.BlockSpec((None, bm, d), lambda bi, i: (bi, i, 0))
    def mod(): return pl.BlockSpec((None, 1, d), lambda bi, i: (bi, 0, 0))
    return pl.pallas_call(
        functools.partial(_outproj_kernel, n_experts=n_experts),
        grid=(b, t // bm),
        in_specs=[row(), _resident((d, d), lambda bi, i: (0, 0)), row(), mod(),
                  pl.BlockSpec((1, d), lambda bi, i: (0, 0)), mod(), mod(),
                  _resident((d, LANES), lambda bi, i: (0, 0)), _resident((d, LANES), lambda bi, i: (0, 0))],
        out_specs=[row(), row(), pl.BlockSpec((None, ep, bm), lambda bi, i: (bi, 0, i)), row()],
        out_shape=[jax.ShapeDtypeStruct((b, t, d), F32), jax.ShapeDtypeStruct((b, t, d), F32),
                   jax.ShapeDtypeStruct((b, ep, t), F32), jax.ShapeDtypeStruct((b, t, d), F32)],
        compiler_params=_cparams("arbitrary", "arbitrary"),
        name="outproj_router",
    )(u, w_out, x, gate, norm_gain.reshape(1, d), shift, scale, router_hi, router_lo)


def _route_kernel(aff_ref, lst_ref, gate_ref, *, cap):
    bits = pltpu.bitcast(aff_ref[...], jnp.int32)
    rows, n_blk, _ = bits.shape
    lane = lax.broadcasted_iota(jnp.int32, bits.shape, 2)
    blk = lax.broadcasted_iota(jnp.int32, bits.shape, 1)

    def count(mask):
        return jnp.sum(mask.astype(jnp.int32), axis=(1, 2), keepdims=True)

    def refine(i, lo):
        cand = lo | lax.shift_left(jnp.int32(1), 30 - i)
        return jnp.where(count(bits >= cand) >= cap, cand, lo)

    cut = lax.fori_loop(0, 31, refine, jnp.zeros((rows, 1, 1), jnp.int32))

    def block_scan(m):
        x = m
        for k in range(LANES.bit_length() - 1):
            s = 1 << k
            x = x + jnp.where(lane >= s, pltpu.roll(x, s, 2), 0)
        return x, jnp.broadcast_to(x[:, :, LANES - 1:LANES], x.shape)

    def exclusive_prefix(m):
        incl, tot = block_scan(m)
        y = tot
        k = 0
        while (1 << k) < n_blk:
            s = 1 << k
            y = y + jnp.where(blk >= s, pltpu.roll(y, s, 1), 0)
            k += 1
        return incl - m + (y - tot)

    above = bits > cut
    at_cut = bits == cut
    need = cap - count(above)
    chosen = jnp.logical_or(above, jnp.logical_and(at_cut, exclusive_prefix(at_cut.astype(jnp.int32)) < need))
    token = blk * LANES + lane
    n_tok = n_blk * LANES

    def shifted(x, s):
        if s < LANES:
            y = pltpu.roll(x, LANES - s, 2)
            y = jnp.where(lane < LANES - s, y, pltpu.roll(y, n_blk - 1, 1)) if n_blk > 1 else y
        else:
            y = pltpu.roll(x, n_blk - s // LANES, 1)
        return jnp.where(token < n_tok - s, y, 0)

    m = chosen.astype(jnp.int32)
    val = jnp.where(chosen, token, 0)
    aff = jnp.where(chosen, bits, 0)
    dist = jnp.where(chosen, token - exclusive_prefix(m), 0)
    valid = m
    for k in range((n_tok - 1).bit_length()):
        s = 1 << k
        moving = valid * ((dist >> k) & 1)
        arrive = shifted(moving, s) == 1
        stay = (valid - moving) == 1
        val = jnp.where(arrive, shifted(val, s), jnp.where(stay, val, 0))
        aff = jnp.where(arrive, shifted(aff, s), jnp.where(stay, aff, 0))
        dist = jnp.where(arrive, shifted(dist, s), jnp.where(stay, dist, 0))
        valid = jnp.logical_or(arrive, stay).astype(jnp.int32)
    lst_ref[...] = val[:, :lst_ref.shape[1], :]
    gate_ref[...] = pltpu.bitcast(aff[:, :gate_ref.shape[1], :], F32)


def _route(aff_t, cap):
    r, t = aff_t.shape
    assert r % SUBLANES == 0 and t % LANES == 0 and cap % LANES == 0
    n_blk = t // LANES
    out_spec = pl.BlockSpec((SUBLANES, cap // LANES, LANES), lambda i: (i, 0, 0))
    return pl.pallas_call(
        functools.partial(_route_kernel, cap=cap),
        grid=(r // SUBLANES,),
        in_specs=[pl.BlockSpec((SUBLANES, n_blk, LANES), lambda i: (i, 0, 0))],
        out_specs=[out_spec, out_spec],
        out_shape=[jax.ShapeDtypeStruct((r, cap // LANES, LANES), jnp.int32),
                   jax.ShapeDtypeStruct((r, cap // LANES, LANES), F32)],
        compiler_params=_cparams("arbitrary"),
        name="route",
    )(aff_t.reshape(r, n_blk, LANES))


def _ffn_kernel(lst_ref, lst_next_ref, gate_ref, h_hbm, acc_in_hbm, wg_ref, wu_ref, wd_ref,
                acc_hbm, x_buf, a_buf, sem, *, cap, half):
    n_batch = pl.num_programs(1)
    b = pl.program_id(1)
    step = pl.program_id(0) * n_batch + b
    n_steps = pl.num_programs(0) * n_batch
    slot = step % 2
    other = 1 - slot
    b_next = jnp.where(b + 1 == n_batch, 0, b + 1)
    def for_chosen(lst, fn):
        def body(s8, carry):
            for r in range(SUBLANES):
                s = s8 * SUBLANES + r
                fn(s, lst[0, s])
            return carry
        lax.fori_loop(0, cap // SUBLANES, body, 0)

    def start_gather(lst, bb, sl):
        def one(s, n):
            pltpu.make_async_copy(h_hbm.at[bb, pl.ds(n, 1)], x_buf.at[sl, pl.ds(s, 1)], sem.at[0, sl]).start()
            pltpu.make_async_copy(acc_in_hbm.at[bb, pl.ds(n, 1)], a_buf.at[sl, pl.ds(s, 1)],
                                  sem.at[1, sl]).start()
        for_chosen(lst, one)

    def wait_gather(sl):
        pltpu.make_async_copy(h_hbm.at[b, pl.ds(0, cap)], x_buf.at[sl], sem.at[0, sl]).wait()
        pltpu.make_async_copy(acc_in_hbm.at[b, pl.ds(0, cap)], a_buf.at[sl], sem.at[1, sl]).wait()

    def start_scatter(sl):
        def one(s, n):
            pltpu.make_async_copy(a_buf.at[sl, pl.ds(s, 1)], acc_hbm.at[b, pl.ds(n, 1)], sem.at[2, sl]).start()
        for_chosen(lst_ref, one)

    def wait_scatter(sl):
        pltpu.make_async_copy(a_buf.at[sl], acc_hbm.at[b, pl.ds(0, cap)], sem.at[2, sl]).wait()

    @pl.when(step == 0)
    def _():
        start_gather(lst_ref, b, slot)

    wait_gather(slot)
    for r0 in range(0, cap, half):
        rows = pl.ds(r0, half)
        x = x_buf[slot, rows, :].astype(BF16)
        a = _dot(x, wg_ref[...])
        u = _dot(x, wu_ref[...])
        y = _dot((_silu(a) * u).astype(BF16), wd_ref[...]) * gate_ref[rows, :]
        a_buf[slot, rows, :] = a_buf[slot, rows, :] + y

    @pl.when(step > 0)
    def _():
        wait_scatter(other)

    @pl.when(step + 1 < n_steps)
    def _():
        start_gather(lst_next_ref, b_next, other)

    start_scatter(slot)

    @pl.when(step == n_steps - 1)
    def _():
        wait_scatter(slot)


def _expert_ffn(h2, acc0, lst, gates, w_gate, w_up, w_down):
    b, t, d = h2.shape
    e, _, f = w_gate.shape
    ep = lst.shape[0] // b
    cap = lst.shape[1] * LANES
    assert b >= 2
    half = cap // 2 if cap % 16 == 0 else cap

    def cur(ei, bi):
        return bi * ep + ei, 0, 0

    def nxt(ei, bi):
        nb = bi + 1
        return (nb % b) * ep + jnp.minimum(ei + nb // b, e - 1), 0, 0

    def smem(shape, index_map):
        return pl.BlockSpec(shape, index_map, memory_space=pltpu.SMEM)

    return pl.pallas_call(
        functools.partial(_ffn_kernel, cap=cap, half=half),
        grid=(e, b),
        in_specs=[smem((None, 1, cap), cur), smem((None, 1, cap), nxt),
                  pl.BlockSpec((None, cap, 1), cur),
                  pl.BlockSpec(memory_space=pl.ANY),
                  pl.BlockSpec(memory_space=pl.ANY),
                  pl.BlockSpec((None, d, f), lambda ei, bi: (ei, 0, 0)),
                  pl.BlockSpec((None, d, f), lambda ei, bi: (ei, 0, 0)),
                  pl.BlockSpec((None, f, d), lambda ei, bi: (ei, 0, 0))],
        out_specs=pl.BlockSpec(memory_space=pl.ANY),
        out_shape=jax.ShapeDtypeStruct((b, t, d), F32),
        scratch_shapes=[pltpu.VMEM((2, cap, d), F32), pltpu.VMEM((2, cap, d), F32),
                        pltpu.SemaphoreType.DMA((3, 2))],
        input_output_aliases={4: 0},
        compiler_params=_cparams("arbitrary", "arbitrary"),
        name="expert_ffn",
    )(lst.reshape(b * ep, 1, cap), lst.reshape(b * ep, 1, cap), gates.reshape(b * ep, cap, 1),
      h2, acc0, w_gate, w_up, w_down)


def _final_kernel(x1_ref, acc_ref, g_ref, nrm_ref, o_ref):
    x = x1_ref[...] + g_ref[...] * acc_ref[...]
    o_ref[...] = x * lax.rsqrt(jnp.mean(x * x, axis=-1, keepdims=True) + NORM_EPS) * nrm_ref[...]


def _final(x1, acc, gate, norm_gain, bm):
    b, t, d = x1.shape
    def row(): return pl.BlockSpec((None, bm, d), lambda bi, i: (bi, i, 0))
    return pl.pallas_call(
        _final_kernel,
        grid=(b, t // bm),
        in_specs=[row(), row(), pl.BlockSpec((None, 1, d), lambda bi, i: (bi, 0, 0)),
                  pl.BlockSpec((1, d), lambda bi, i: (0, 0))],
        out_specs=row(),
        out_shape=jax.ShapeDtypeStruct((b, t, d), F32),
        compiler_params=_cparams("arbitrary", "arbitrary"),
        name="final_norm",
    )(x1, acc, gate, norm_gain.reshape(1, d))


def _row_block(t, largest):
    for bm in (1024, 512, 256, 128, 64):
        if bm <= largest and t % bm == 0:
            return bm
    raise ValueError(f"token count {t} is not a multiple of 64")


def kernel(x, c, ctx, c_ctx, ada_w, ada_b, norm_mix, norm_ffn, w_in, gdn_conv, gdn_a_log, gdn_dt_bias,
           hgrn_lb, hgrn_norm, gdn_norm, w_branch_a, w_branch_b, w_out, router_w, w_gate, w_up, w_down,
           final_norm):
    depth = ada_w.shape[0]
    assert depth == 1, "single-layer stack"
    b, t, d = x.shape
    tc = ctx.shape[1]
    n_experts = router_w.shape[-1]
    assert d % COL == 0 and (6 * COL) % d == 0 and t == CHUNK * GRID_W and tc % CHUNK == 0
    assert w_in.shape[-1] == 9 * COL + 4 * HEADS + 2 * d and n_experts <= LANES

    lower = jnp.cumsum(jax.nn.softmax(hgrn_lb.astype(F32), axis=1), axis=1)

    c_rows = jnp.zeros((SUBLANES, d), F32).at[:b].set(c).at[b].set(c_ctx)
    mods = _modulation(c_rows, ada_w[0], ada_b[0])
    ml = [m.reshape(b, 1, d) for m in jnp.split(mods[:b], 6, axis=-1)]
    mc = [m.reshape(1, 1, d) for m in jnp.split(mods[b:b + 1], 6, axis=-1)]

    wi = w_in[0]
    s_ab = 9 * COL
    s_gates = s_ab + 4 * HEADS
    w_main = jnp.concatenate([wi[:, :5 * COL], wi[:, 8 * COL:s_ab], wi[:, s_gates:]], axis=1).astype(BF16)
    w_qkv = wi[:, 5 * COL:8 * COL].astype(BF16)
    w_ab = jnp.pad(wi[:, s_ab:s_gates], ((0, 0), (0, LANES - 4 * HEADS))).astype(BF16)

    bm_lat, bm_ctx = _row_block(t, 1024), _row_block(tc, 1024)
    bn_main = 2 * COL if w_main.shape[1] % (2 * COL) == 0 else COL
    (p_lat,) = _input_projection(x, norm_mix[0], ml[0], ml[1], w_main, None, BF16, bm_lat, bn_main)
    qkv_lat, ab_lat = _input_projection(x, norm_mix[0], ml[0], ml[1], w_qkv, w_ab, F32, bm_lat)
    (p_ctx,) = _input_projection(ctx, norm_mix[0], mc[0], mc[1], w_main[:, :4 * COL], None, BF16, bm_ctx)
    qkv_ctx, ab_ctx = _input_projection(ctx, norm_mix[0], mc[0], mc[1], w_qkv, w_ab, F32, bm_ctx)

    oa_f, oa_b = _hgrn_scans(p_ctx, p_lat, lower[0, 0], lower[1, 0])

    x_ctx = _gdn_prep_ctx(qkv_ctx, gdn_conv[0])
    x_lat, ab_lat_cm = _gdn_prep_lat(qkv_lat, ab_lat, gdn_conv[0])
    pad = LANES - 2 * HEADS
    alog_row = jnp.pad(gdn_a_log[0].reshape(1, 2 * HEADS).astype(F32), ((0, 0), (0, pad)))
    dtb_row = jnp.pad(gdn_dt_bias[0].reshape(1, 2 * HEADS).astype(F32), ((0, 0), (0, pad)))
    ob_f, ob_b = _gdn_scans(x_ctx, x_lat, ab_ctx.reshape(b, tc // CHUNK, CHUNK, LANES), ab_lat_cm,
                            alog_row, dtb_row)

    u = _merge(p_lat, oa_f, oa_b, ob_f, ob_b, hgrn_norm[0], gdn_norm[0],
               w_branch_a[0].astype(BF16), w_branch_b[0].astype(BF16))
    router_pad = jnp.pad(router_w[0].astype(F32), ((0, 0), (0, LANES - n_experts)))
    bm_row = _row_block(t, 256)
    x1, h2, aff_t, acc0 = _outproj_router(u, w_out[0].astype(BF16), x, ml[2], norm_ffn[0], ml[3], ml[4],
                                          router_pad, n_experts, bm_row)

    cap = max(1, (CAPACITY_FACTOR * t) // n_experts)
    lst, gates = _route(aff_t.reshape(-1, t), cap)
    acc = _expert_ffn(h2, acc0, lst, gates, w_gate[0].astype(BF16), w_up[0].astype(BF16),
                      w_down[0].astype(BF16))
    return _final(x1, acc, ml[5], final_norm, bm_row)
```

```python
import functools

import jax
import jax.numpy as jnp
from jax import lax
from jax.experimental import pallas as pl
from jax.experimental.pallas import tpu as pltpu

F32 = jnp.float32
BF16 = jnp.bfloat16

NORM_EPS = 1e-6
LOG2E = 1.4426950408889634
GRID_W = 64
HEADS = 8
HEAD_DIM = 128
CHUNK = 64
CONV_K = 5
CAPACITY_FACTOR = 2
COL = HEADS * HEAD_DIM
LANES = 128
SUBLANES = 8
GROUP = SUBLANES
INV_BLOCK = 8
HGRN_SAMPLES_PER_STEP = 2
W_CHUNKS = 8
W_STAGES = 4
VMEM_LIMIT_BYTES = 56 * 1024 * 1024


def _cparams(*sem):
    return pltpu.CompilerParams(dimension_semantics=sem, vmem_limit_bytes=VMEM_LIMIT_BYTES)


def _dot(a, b):
    return jnp.dot(a, b, preferred_element_type=F32)


def _dot_nt(a, b):
    return lax.dot_general(a, b, (((1,), (1,)), ((), ())), preferred_element_type=F32)


def _silu(x):
    return x * jax.nn.sigmoid(x)


def _scan_cumsum(g, rev, terms):
    tri = jnp.where(_tri(rev), 1.0, 0.0).astype(BF16)
    total = None
    rest = g
    for _ in range(terms):
        part = rest.astype(BF16)
        rest = rest - part.astype(F32)
        total = _dot(tri, part) if total is None else total + _dot(tri, part)
    return total


def _resident(shape, index_map):
    return pl.BlockSpec(shape, index_map, pipeline_mode=pl.Buffered(1))


def _mod_kernel(c_ref, w_ref, b_ref, o_ref):
    o_ref[...] = _dot(_silu(c_ref[...]), w_ref[...]) + b_ref[...]


def _modulation(c_rows, ada_w, ada_b):
    rows, d = c_rows.shape
    n = ada_w.shape[1]
    bn = 1024
    return pl.pallas_call(
        _mod_kernel,
        grid=(n // bn,),
        in_specs=[pl.BlockSpec((rows, d), lambda j: (0, 0)),
                  pl.BlockSpec((d, bn), lambda j: (0, j)),
                  pl.BlockSpec((1, bn), lambda j: (0, j))],
        out_specs=pl.BlockSpec((rows, bn), lambda j: (0, j)),
        out_shape=jax.ShapeDtypeStruct((rows, n), F32),
        compiler_params=_cparams("arbitrary"),
        name="modulation",
    )(c_rows, ada_w, ada_b.reshape(1, n))


def _proj_kernel(x_ref, gain_ref, shift_ref, scale_ref, w_ref, *rest, with_ab):
    if with_ab:
        wab_ref, o_ref, oab_ref, h_ref = rest
    else:
        o_ref, h_ref = rest

    @pl.when(pl.program_id(2) == 0)
    def _():
        x = x_ref[...]
        rs = lax.rsqrt(jnp.mean(x * x, axis=-1, keepdims=True) + NORM_EPS)
        h = ((x * rs) * (gain_ref[...] * (1.0 + scale_ref[...])) + shift_ref[...]).astype(BF16)
        h_ref[...] = h
        if with_ab:
            oab_ref[...] = _dot(h, wab_ref[...])

    o_ref[...] = _dot(h_ref[...], w_ref[...]).astype(o_ref.dtype)


def _input_projection(x, gain, shift, scale, w, w_ab, out_dtype, bm, bn=COL):
    b, t, d = x.shape
    n = w.shape[1]
    assert n % bn == 0
    per_batch = shift.shape[0] > 1
    mod_map = (lambda bi, i, j: (bi, 0, 0)) if per_batch else (lambda bi, i, j: (0, 0, 0))
    with_ab = w_ab is not None
    in_specs = [pl.BlockSpec((None, bm, d), lambda bi, i, j: (bi, i, 0)),
                pl.BlockSpec((1, d), lambda bi, i, j: (0, 0)),
                pl.BlockSpec((None, 1, d), mod_map),
                pl.BlockSpec((None, 1, d), mod_map),
                pl.BlockSpec((d, bn), lambda bi, i, j: (0, j))]
    out_specs = [pl.BlockSpec((None, bm, bn), lambda bi, i, j: (bi, i, j))]
    out_shape = [jax.ShapeDtypeStruct((b, t, n), out_dtype)]
    args = [x, gain.reshape(1, d), shift, scale, w]
    if with_ab:
        in_specs.append(pl.BlockSpec((d, LANES), lambda bi, i, j: (0, 0)))
        out_specs.append(pl.BlockSpec((None, bm, LANES), lambda bi, i, j: (bi, i, 0)))
        out_shape.append(jax.ShapeDtypeStruct((b, t, LANES), F32))
        args.append(w_ab)
    return pl.pallas_call(
        functools.partial(_proj_kernel, with_ab=with_ab),
        grid=(b, t // bm, n // bn),
        in_specs=in_specs, out_specs=out_specs, out_shape=out_shape,
        scratch_shapes=[pltpu.VMEM((bm, d), BF16)],
        compiler_params=_cparams("arbitrary", "arbitrary", "arbitrary"),
        name="input_projection",
    )(*args)


def _tri(rev, strict=False):
    ri = lax.broadcasted_iota(jnp.int32, (CHUNK, CHUNK), 0)
    ci = lax.broadcasted_iota(jnp.int32, (CHUNK, CHUNK), 1)
    if rev:
        return (ci > ri) if strict else (ci >= ri)
    return (ci < ri) if strict else (ci <= ri)


def _boundary_rows(g, bs, rev):
    n, w = g.shape
    off = (bs >> 1) - 1 + (1 if rev else 0)
    if bs >= SUBLANES:
        pieces = [jnp.broadcast_to(g[p * bs + off:p * bs + off + 1], (bs, w)) for p in range(n // bs)]
        return jnp.concatenate(pieces, axis=0)
    sub = lax.broadcasted_iota(jnp.int32, (SUBLANES, w), 0)
    pieces = []
    for r0 in range(0, n, SUBLANES):
        acc = None
        for p in range(SUBLANES // bs):
            r = r0 + p * bs + off
            val = jnp.broadcast_to(g[r:r + 1], (SUBLANES, w))
            acc = val if acc is None else jnp.where(sub >= p * bs, val, acc)
        pieces.append(acc)
    return jnp.concatenate(pieces, axis=0)


def _hgrn_chunks(scans):
    row = lax.broadcasted_iota(jnp.int32, (CHUNK, 1), 0)
    ri = lax.broadcasted_iota(jnp.int32, (CHUNK, CHUNK), 0)
    ci = lax.broadcasted_iota(jnp.int32, (CHUNK, CHUNK), 1)
    n_levels = CHUNK.bit_length() - 1
    eye = ri == ci

    units = []
    for q_raw, f_raw, v, low, s_ref, rev in scans:
        q = _silu(q_raw.astype(F32))
        f_raw = f_raw.astype(F32)
        one_m = 1.0 - low
        f = low + one_m * jax.nn.sigmoid(f_raw)
        k = one_m * jax.nn.sigmoid(-f_raw)
        gc = _scan_cumsum(jnp.log(f), rev, 2) * LOG2E
        g_last = gc[0:1] if rev else gc[CHUNK - 1:CHUNK]
        q_dec = (q * jnp.exp2(gc)).astype(BF16)
        k_dec = (k * jnp.exp2(g_last - gc)).astype(BF16)
        mixed, pair = [], []
        for lv in range(1, n_levels + 1):
            upper = ((row >> (lv - 1)) & 1) == 1
            q_rows = jnp.logical_not(upper) if rev else upper
            decay = jnp.exp2(-jnp.abs(gc - _boundary_rows(gc, 1 << lv, rev)))
            mixed.append((jnp.where(q_rows, q, k) * decay).astype(BF16))
            i_upper = ((ri >> (lv - 1)) & 1) == 1
            j_upper = ((ci >> (lv - 1)) & 1) == 1
            halves = jnp.logical_and(j_upper, jnp.logical_not(i_upper)) if rev else \
                jnp.logical_and(i_upper, jnp.logical_not(j_upper))
            pair.append(jnp.logical_and((ri >> lv) == (ci >> lv), halves))
        qk = q * k
        v_b = v.astype(BF16)
        v_f = v.astype(F32)
        for h in range(HEADS):
            sl = slice(h * HEAD_DIM, (h + 1) * HEAD_DIM)
            units.append(dict(s_ref=s_ref, h=h, diag=jnp.sum(qk[:, sl], axis=1, keepdims=True),
                              mixed=[x[:, sl] for x in mixed], pair=pair,
                              q_dec=q_dec[:, sl], k_dec=k_dec[:, sl], v_b=v_b[:, sl], v_f=v_f[:, sl],
                              decay=jnp.exp2(g_last[:, sl])))

    scores = [[_dot_nt(m, m) for m in u["mixed"]] for u in units]
    states = [u["s_ref"][u["h"]] for u in units]
    carried = [_dot_nt(u["q_dec"], s.astype(BF16)) for u, s in zip(units, states)]
    outs = []
    for u, sc, car in zip(units, scores, carried):
        a = jnp.where(eye, u["diag"], 0.0)
        for lv in range(n_levels):
            a = jnp.where(u["pair"][lv], sc[lv], a)
        outs.append(_dot(a.astype(BF16), u["v_b"]) + car)
    for u, s in zip(units, states):
        u["s_ref"][u["h"]] = u["decay"] * s + _dot(u["v_f"].T.astype(BF16), u["k_dec"])
    return [jnp.concatenate(outs[i * HEADS:(i + 1) * HEADS], axis=1) for i in range(len(scans))]


def _hgrn_kernel(lowf_ref, lowb_ref,
                 cqf, cff, cvf, cqb, cfb, cvb,
                 lqf, lff, lvf, lqb, lfb, lvb,
                 of_ref, ob_ref, sf_ref, sb_ref, *, n_ctx, per_step):
    t = pl.program_id(1)
    n_samples = of_ref.shape[0]

    @pl.when(t == 0)
    def _():
        sf_ref[...] = jnp.zeros_like(sf_ref)
        sb_ref[...] = jnp.zeros_like(sb_ref)

    is_ctx = t < n_ctx

    def body(j, carry):
        rows_f = pl.ds(pl.multiple_of(j * CHUNK, CHUNK), CHUNK)
        rows_b = pl.ds(pl.multiple_of((per_step - 1 - j) * CHUNK, CHUNK), CHUNK)

        def pick(c_ref, l_ref, i, rows):
            return jnp.where(is_ctx, c_ref[i, rows, :], l_ref[i, rows, :])

        scans = []
        for i in range(n_samples):
            scans.append((pick(cqf, lqf, i, rows_f), pick(cff, lff, i, rows_f), pick(cvf, lvf, i, rows_f),
                          lowf_ref[...], sf_ref.at[i], False))
            scans.append((pick(cqb, lqb, i, rows_b), pick(cfb, lfb, i, rows_b), pick(cvb, lvb, i, rows_b),
                          lowb_ref[...], sb_ref.at[i], True))
        outs = _hgrn_chunks(scans)
        for i in range(n_samples):
            of_ref[i, rows_f, :] = outs[2 * i].astype(of_ref.dtype)
            ob_ref[i, rows_b, :] = outs[2 * i + 1].astype(ob_ref.dtype)
        return carry

    lax.fori_loop(0, per_step, body, 0)


def _hgrn_scans(p_ctx, p_lat, low_f, low_b):
    b, tc, _ = p_ctx.shape
    t = p_lat.shape[1]
    per_step = max(g for g in (4, 2, 1) if (tc // CHUNK) % g == 0 and (t // CHUNK) % g == 0)
    rows = per_step * CHUNK
    n_ctx, n_lat = tc // rows, t // rows

    def cf(ti): return jnp.minimum(ti, n_ctx - 1)
    def cb(ti): return jnp.maximum(n_ctx - 1 - ti, 0)
    def lf(ti): return jnp.maximum(ti - n_ctx, 0)
    def lb(ti): return n_lat - 1 - jnp.maximum(ti - n_ctx, 0)

    nb = HGRN_SAMPLES_PER_STEP if b % HGRN_SAMPLES_PER_STEP == 0 else 1

    def spec(block_of, col):
        return pl.BlockSpec((nb, rows, COL), lambda bi, ti: (bi, block_of(ti), col))

    low_spec = pl.BlockSpec((1, COL), lambda bi, ti: (0, 0))
    out_f, out_b = pl.pallas_call(
        functools.partial(_hgrn_kernel, n_ctx=n_ctx, per_step=per_step),
        grid=(b // nb, n_ctx + n_lat),
        in_specs=[low_spec, low_spec,
                  spec(cf, 0), spec(cf, 1), spec(cf, 3), spec(cb, 0), spec(cb, 2), spec(cb, 3),
                  spec(lf, 0), spec(lf, 1), spec(lf, 3), spec(lb, 0), spec(lb, 2), spec(lb, 3)],
        out_specs=[pl.BlockSpec((nb, rows, COL), lambda bi, ti: (bi, lf(ti), 0)),
                   pl.BlockSpec((nb, rows, COL), lambda bi, ti: (bi, lb(ti), 0))],
        out_shape=[jax.ShapeDtypeStruct((b, t, COL), BF16)] * 2,
        scratch_shapes=[pltpu.VMEM((nb, HEADS, HEAD_DIM, HEAD_DIM), F32)] * 2,
        compiler_params=_cparams("arbitrary", "arbitrary"),
        name="hgrn_scans",
    )(low_f.reshape(1, COL), low_b.reshape(1, COL), *([p_ctx] * 6), *([p_lat] * 6))
    return out_f, out_b


def _conv_silu_norm(x, prev, nxt, w, idx):
    row = lax.broadcasted_iota(jnp.int32, (CHUNK, 1), 0)
    xm1 = jnp.where(row == 0, prev[7:8], pltpu.roll(x, 1, 0))
    xm2 = jnp.where(row == 0, prev[6:7], jnp.where(row == 1, prev[7:8], pltpu.roll(x, 2, 0)))
    xp1 = jnp.where(row == CHUNK - 1, nxt[0:1], pltpu.roll(x, CHUNK - 1, 0))
    xp2 = jnp.where(row == CHUNK - 1, nxt[1:2],
                    jnp.where(row == CHUNK - 2, nxt[0:1], pltpu.roll(x, CHUNK - 2, 0)))
    return _silu_norm(w[0:1] * xm2 + w[1:2] * xm1 + w[2:3] * x + w[3:4] * xp1 + w[4:5] * xp2, idx)


def _silu_norm(y, idx):
    y = _silu(y)
    if idx == 2:
        return y
    scale = HEAD_DIM ** -0.5 if idx == 0 else 1.0
    pieces = []
    for h in range(HEADS):
        yh = y[:, h * HEAD_DIM:(h + 1) * HEAD_DIM]
        pieces.append(yh * (lax.rsqrt(jnp.sum(yh * yh, axis=1, keepdims=True) + NORM_EPS) * scale))
    return jnp.concatenate(pieces, axis=1)


def _gdn_prep_ctx_kernel(q_ref, k_ref, v_ref, pq_ref, pk_ref, pv_ref, nq_ref, nk_ref, nv_ref, w_ref, o_ref):
    t = pl.program_id(1)
    first = t == 0
    last = t == pl.num_programs(1) - 1
    for idx, (c_ref, p_ref, n_ref) in enumerate(((q_ref, pq_ref, nq_ref), (k_ref, pk_ref, nk_ref),
                                                 (v_ref, pv_ref, nv_ref))):
        prev = jnp.where(first, 0.0, p_ref[...])
        nxt = jnp.where(last, 0.0, n_ref[...])
        y = _conv_silu_norm(c_ref[...], prev, nxt, w_ref[:, idx * COL:(idx + 1) * COL], idx)
        o_ref[:, idx * COL:(idx + 1) * COL] = y.astype(o_ref.dtype)


def _gdn_prep_ctx(qkv, conv_w):
    b, t, _ = qkv.shape
    n_chunks = t // CHUNK
    rb = CHUNK // SUBLANES
    def cur(col): return pl.BlockSpec((None, CHUNK, COL), lambda bi, c: (bi, c, col))
    def prv(col): return pl.BlockSpec((None, SUBLANES, COL), lambda bi, c: (bi, jnp.maximum(c * rb - 1, 0), col))
    def nxt(col): return pl.BlockSpec((None, SUBLANES, COL),
                                      lambda bi, c: (bi, jnp.minimum((c + 1) * rb, n_chunks * rb - 1), col))
    cols = (0, 1, 2)
    return pl.pallas_call(
        _gdn_prep_ctx_kernel,
        grid=(b, n_chunks),
        in_specs=[cur(c) for c in cols] + [prv(c) for c in cols] + [nxt(c) for c in cols]
                 + [pl.BlockSpec((CONV_K, 3 * COL), lambda bi, c: (0, 0))],
        out_specs=pl.BlockSpec((None, None, CHUNK, 3 * COL), lambda bi, c: (bi, c, 0, 0)),
        out_shape=jax.ShapeDtypeStruct((b, n_chunks, CHUNK, 3 * COL), BF16),
        compiler_params=_cparams("arbitrary", "arbitrary"),
        name="gdn_prep_ctx",
    )(*([qkv] * 9), conv_w)


def _gdn_prep_lat_kernel(q_ref, k_ref, v_ref, pq_ref, pk_ref, pv_ref, nq_ref, nk_ref, nv_ref, ab_ref, w_ref,
                         o_ref, oab_ref):
    g = pl.program_id(1)
    first = g == 0
    last = g == pl.num_programs(1) - 1
    tail = CHUNK - SUBLANES
    for idx, (c_ref, p_ref, n_ref) in enumerate(((q_ref, pq_ref, nq_ref), (k_ref, pk_ref, nk_ref),
                                                 (v_ref, pv_ref, nv_ref))):
        w = w_ref[:, idx * COL:(idx + 1) * COL]
        cols = [c_ref[:, j, :] for j in range(GROUP)]
        before = jnp.where(first, 0.0, p_ref[:, GROUP - 1, :])
        after = jnp.where(last, 0.0, n_ref[:, 0, :])
        for j in range(GROUP):
            prev = cols[j - 1][tail:] if j > 0 else before
            nxt = cols[j + 1][:SUBLANES] if j < GROUP - 1 else after
            o_ref[j, :, idx * COL:(idx + 1) * COL] = _conv_silu_norm(cols[j], prev, nxt, w, idx).astype(o_ref.dtype)
    for j in range(GROUP):
        oab_ref[j] = ab_ref[:, j, :]


def _gdn_prep_lat(qkv, ab, conv_w):
    b, t, _ = qkv.shape
    assert t == CHUNK * GRID_W and GRID_W % GROUP == 0
    n_groups = GRID_W // GROUP
    rb = CHUNK // SUBLANES
    q4 = qkv.reshape(b, CHUNK, GRID_W, 3 * COL)
    ab4 = ab.reshape(b, CHUNK, GRID_W, LANES)
    def cur(col): return pl.BlockSpec((None, CHUNK, GROUP, COL), lambda bi, g: (bi, 0, g, col))
    def prv(col): return pl.BlockSpec((None, SUBLANES, GROUP, COL),
                                      lambda bi, g: (bi, rb - 1, jnp.maximum(g - 1, 0), col))
    def nxt(col): return pl.BlockSpec((None, SUBLANES, GROUP, COL),
                                      lambda bi, g: (bi, 0, jnp.minimum(g + 1, n_groups - 1), col))
    cols = (0, 1, 2)
    return pl.pallas_call(
        _gdn_prep_lat_kernel,
        grid=(b, n_groups),
        in_specs=[cur(c) for c in cols] + [prv(c) for c in cols] + [nxt(c) for c in cols]
                 + [pl.BlockSpec((None, CHUNK, GROUP, LANES), lambda bi, g: (bi, 0, g, 0)),
                    pl.BlockSpec((CONV_K, 3 * COL), lambda bi, g: (0, 0))],
        out_specs=[pl.BlockSpec((None, GROUP, CHUNK, 3 * COL), lambda bi, g: (bi, g, 0, 0)),
                   pl.BlockSpec((None, GROUP, CHUNK, LANES), lambda bi, g: (bi, g, 0, 0))],
        out_shape=[jax.ShapeDtypeStruct((b, GRID_W, CHUNK, 3 * COL), BF16),
                   jax.ShapeDtypeStruct((b, GRID_W, CHUNK, LANES), F32)],
        compiler_params=_cparams("arbitrary", "arbitrary"),
        name="gdn_prep_lat",
    )(*([q4] * 9), ab4, conv_w)


def _mm_bf16(a, b):
    return _dot(a.astype(BF16), b.astype(BF16))


def _unit_tri_inverses(ns):
    ri = lax.broadcasted_iota(jnp.int32, (CHUNK, CHUNK), 0)
    ci = lax.broadcasted_iota(jnp.int32, (CHUNK, CHUNK), 1)
    eye = jnp.where(ri == ci, 1.0, 0.0)
    shift = INV_BLOCK.bit_length() - 1
    on_diag = (ri >> shift) == (ci >> shift)

    def nilpotent_inverses(ms, degree):
        xs = [eye - m for m in ms]
        ps = ms
        for _ in range(degree.bit_length() - 2):
            ps = [_mm_bf16(p, p) for p in ps]
            xs = [x + _mm_bf16(x, p) for x, p in zip(xs, ps)]
        return xs

    d_invs = nilpotent_inverses([jnp.where(on_diag, n, 0.0) for n in ns], INV_BLOCK)
    ms = [_mm_bf16(d, jnp.where(on_diag, 0.0, n)) for d, n in zip(d_invs, ns)]
    ys = nilpotent_inverses(ms, CHUNK // INV_BLOCK)
    return [_mm_bf16(y, d) for y, d in zip(ys, d_invs)]


def _gdn_chunks(scans, alog, dtb):
    units = []
    for x, ab, s_ref, rev in scans:
        g_all = -jnp.exp(alog) * jax.nn.softplus(ab + dtb)
        beta_all = jax.nn.sigmoid(ab)
        gc_all = _scan_cumsum(g_all, rev, 3)
        gr_all = gc_all.T
        incl = _tri(rev)
        strict = _tri(rev, strict=True)
        off = HEADS if rev else 0
        for h in range(HEADS):
            gcol = gc_all[:, off + h:off + h + 1]
            grow = gr_all[off + h:off + h + 1, :]
            units.append(dict(
                s_ref=s_ref, h=h, strict=strict, gcol=gcol,
                beta=beta_all[:, 2 * HEADS + off + h:2 * HEADS + off + h + 1],
                g_last=gcol[0:1] if rev else gcol[CHUNK - 1:CHUNK],
                gamma=jnp.where(incl, jnp.exp(jnp.where(incl, gcol - grow, 0.0)), 0.0),
                q=x[:, h * HEAD_DIM:(h + 1) * HEAD_DIM],
                k=x[:, COL + h * HEAD_DIM:COL + (h + 1) * HEAD_DIM],
                v=x[:, 2 * COL + h * HEAD_DIM:2 * COL + (h + 1) * HEAD_DIM]))

    both = [_dot_nt(jnp.concatenate([u["q"], u["k"]], axis=0), u["k"]) for u in units]
    qks = [x[:CHUNK] for x in both]
    ns = [jnp.where(u["strict"], u["beta"] * x[CHUNK:] * u["gamma"], 0.0) for u, x in zip(units, both)]
    invs = _unit_tri_inverses(ns)
    rhss = [jnp.concatenate([u["v"].astype(F32) * u["beta"],
                             u["k"].astype(F32) * (u["beta"] * jnp.exp(u["gcol"]))], axis=1) for u in units]
    sols = [_mm_bf16(a, r) for a, r in zip(invs, rhss)]
    states = [u["s_ref"][u["h"]] for u in units]
    states_b = [s.astype(BF16) for s in states]
    v_news = [sol[:, :HEAD_DIM] - _dot_nt(sol[:, HEAD_DIM:].astype(BF16), s_b) for sol, s_b in zip(sols, states_b)]
    outs = [_dot_nt((u["q"].astype(F32) * jnp.exp(u["gcol"])).astype(BF16), s_b)
            + _mm_bf16(qk * u["gamma"], v_new)
            for u, qk, s_b, v_new in zip(units, qks, states_b, v_news)]
    for u, s, v_new in zip(units, states, v_news):
        k_dec = (u["k"].astype(F32) * jnp.exp(u["g_last"] - u["gcol"])).astype(BF16)
        u["s_ref"][u["h"]] = jnp.exp(u["g_last"]) * s + _dot(v_new.T.astype(BF16), k_dec)
    return [jnp.concatenate(outs[i * HEADS:(i + 1) * HEADS], axis=1) for i in range(len(scans))]


def _gdn_ctx_kernel(alog_ref, dtb_ref, x_ref, ab_ref, sf_ref, sb_ref, *, n_chunks):
    sf_ref[...] = jnp.zeros_like(sf_ref)
    sb_ref[...] = jnp.zeros_like(sb_ref)

    def body(j, carry):
        jb = n_chunks - 1 - j
        _gdn_chunks([(x_ref[j], ab_ref[j], sf_ref, False), (x_ref[jb], ab_ref[jb], sb_ref, True)],
                    alog_ref[...], dtb_ref[...])
        return carry

    lax.fori_loop(0, n_chunks, body, 0)


def _gdn_lat_kernel(alog_ref, dtb_ref, s0f_ref, s0b_ref, xf_ref, abf_ref, xb_ref, abb_ref,
                    of_ref, ob_ref, sf_ref, sb_ref):
    n_samples, per_step = xf_ref.shape[:2]

    @pl.when(pl.program_id(1) == 0)
    def _():
        sf_ref[...] = s0f_ref[...]
        sb_ref[...] = s0b_ref[...]

    def body(j, carry):
        jb = per_step - 1 - j
        scans = []
        for i in range(n_samples):
            scans.append((xf_ref[i, j], abf_ref[i, j], sf_ref.at[i], False))
            scans.append((xb_ref[i, jb], abb_ref[i, jb], sb_ref.at[i], True))
        outs = _gdn_chunks(scans, alog_ref[...], dtb_ref[...])
        for i in range(n_samples):
            of_ref[i, j] = outs[2 * i]
            ob_ref[i, jb] = outs[2 * i + 1]
        return carry

    lax.fori_loop(0, per_step, body, 0)


def _gdn_scans(x_ctx, x_lat, ab_ctx, ab_lat, alog_row, dtb_row):
    b, n_ctx = x_ctx.shape[:2]
    n_lat = x_lat.shape[1]
    state_shape = jax.ShapeDtypeStruct((b, HEADS, HEAD_DIM, HEAD_DIM), F32)
    def state_spec(nidx):
        return pl.BlockSpec((None, HEADS, HEAD_DIM, HEAD_DIM), lambda *i: (i[0], 0, 0, 0))
    s_f, s_b = pl.pallas_call(
        functools.partial(_gdn_ctx_kernel, n_chunks=n_ctx),
        grid=(b,),
        in_specs=[pl.BlockSpec((1, LANES), lambda bi: (0, 0)), pl.BlockSpec((1, LANES), lambda bi: (0, 0)),
                  pl.BlockSpec((None, n_ctx, CHUNK, 3 * COL), lambda bi: (bi, 0, 0, 0)),
                  pl.BlockSpec((None, n_ctx, CHUNK, LANES), lambda bi: (bi, 0, 0, 0))],
        out_specs=[state_spec(1), state_spec(1)],
        out_shape=[state_shape, state_shape],
        compiler_params=_cparams("arbitrary"),
        name="gdn_ctx_scans",
    )(alog_row, dtb_row, x_ctx, ab_ctx)

    nb = 2 if b % 2 == 0 else 1
    per_step = max(g for g in (4, 2, 1) if n_lat % g == 0)
    n_steps = n_lat // per_step
    def gf(ti): return ti
    def gb(ti): return n_steps - 1 - ti
    def lat_state_spec():
        return pl.BlockSpec((nb, HEADS, HEAD_DIM, HEAD_DIM), lambda bi, ti: (bi, 0, 0, 0))
    def xspec(group_of):
        return pl.BlockSpec((nb, per_step, CHUNK, 3 * COL), lambda bi, ti: (bi, group_of(ti), 0, 0))
    def abspec(group_of):
        return pl.BlockSpec((nb, per_step, CHUNK, LANES), lambda bi, ti: (bi, group_of(ti), 0, 0))
    def ospec(group_of):
        return pl.BlockSpec((nb, per_step, CHUNK, COL), lambda bi, ti: (bi, group_of(ti), 0, 0))
    row_spec = pl.BlockSpec((1, LANES), lambda bi, ti: (0, 0))
    return pl.pallas_call(
        _gdn_lat_kernel,
        grid=(b // nb, n_steps),
        in_specs=[row_spec, row_spec, lat_state_spec(), lat_state_spec(),
                  xspec(gf), abspec(gf), xspec(gb), abspec(gb)],
        out_specs=[ospec(gf), ospec(gb)],
        out_shape=[jax.ShapeDtypeStruct((b, n_lat, CHUNK, COL), F32)] * 2,
        scratch_shapes=[pltpu.VMEM((nb, HEADS, HEAD_DIM, HEAD_DIM), F32)] * 2,
        compiler_params=_cparams("arbitrary", "arbitrary"),
        name="gdn_lat_scans",
    )(alog_row, dtb_row, s_f, s_b, x_lat, ab_lat, x_lat, ab_lat)


def _head_norm(x):
    pieces = []
    for h in range(HEADS):
        xh = x[:, h * HEAD_DIM:(h + 1) * HEAD_DIM]
        pieces.append(xh * lax.rsqrt(jnp.mean(xh * xh, axis=1, keepdims=True) + NORM_EPS))
    return jnp.concatenate(pieces, axis=1)


def _merge_kernel(oaf, oab, og, obf, obb, z, ga, gb, hag, gbg, wba, wbb, u_ref):
    o_a = oaf[...].astype(F32) + oab[...].astype(F32)
    y_a = _head_norm(o_a * jax.nn.sigmoid(og[...].astype(F32))) * hag[...]
    o_b = jnp.concatenate([obf[:, r, :] + obb[:, r, :] for r in range(obf.shape[1])], axis=0)
    y_b = _head_norm(o_b) * gbg[...] * _silu(z[...].astype(F32))
    u = jax.nn.sigmoid(ga[...].astype(F32)) * _dot(y_a.astype(BF16), wba[...])
    u = u + jax.nn.sigmoid(gb[...].astype(F32)) * _dot(y_b.astype(BF16), wbb[...])
    u_ref[...] = u.astype(u_ref.dtype)


def _merge(p_lat, oa_f, oa_b, ob_f, ob_b, ha_gain, gb_gain, w_ba, w_bb):
    b, t, _ = p_lat.shape
    d = w_ba.shape[1]
    bm = SUBLANES * GRID_W
    g0 = 6 * COL // d
    def o_spec(): return pl.BlockSpec((None, bm, COL), lambda bi, i: (bi, i, 0))
    def ob_spec(): return pl.BlockSpec((None, GRID_W, SUBLANES, COL), lambda bi, i: (bi, 0, i, 0))
    def p_spec(col): return pl.BlockSpec((None, bm, COL), lambda bi, i: (bi, i, col))
    def gate_spec(k): return pl.BlockSpec((None, bm, d), lambda bi, i: (bi, i, g0 + k))
    def full(shape): return _resident(shape, lambda bi, i: (0,) * len(shape))
    return pl.pallas_call(
        _merge_kernel,
        grid=(b, t // bm),
        in_specs=[o_spec(), o_spec(), p_spec(4), ob_spec(), ob_spec(), p_spec(5),
                  gate_spec(0), gate_spec(1), full((1, COL)), full((1, COL)),
                  full((COL, d)), full((COL, d))],
        out_specs=pl.BlockSpec((None, bm, d), lambda bi, i: (bi, i, 0)),
        out_shape=jax.ShapeDtypeStruct((b, t, d), BF16),
        compiler_params=_cparams("arbitrary", "arbitrary"),
        name="merge",
    )(oa_f, oa_b, p_lat, ob_f, ob_b, p_lat, p_lat, p_lat,
      ha_gain.reshape(1, COL), gb_gain.reshape(1, COL), w_ba, w_bb)


def _outproj_kernel(u_ref, wout_ref, x_ref, g_ref, nrm_ref, sh_ref, sc_ref, rwh_ref, rwl_ref,
                    x1_ref, h2_ref, afft_ref, acc0_ref, *, n_experts):
    acc0_ref[...] = jnp.zeros_like(acc0_ref)
    x1 = x_ref[...] + g_ref[...] * _dot(u_ref[...], wout_ref[...])
    x1_ref[...] = x1
    rs = lax.rsqrt(jnp.mean(x1 * x1, axis=-1, keepdims=True) + NORM_EPS)
    h2 = (x1 * rs) * (nrm_ref[...] * (1.0 + sc_ref[...])) + sh_ref[...]
    h2_ref[...] = h2
    h_hi = h2.astype(BF16)
    h_lo = (h2 - h_hi.astype(F32)).astype(BF16)
    logits = _dot(h_hi, rwh_ref[...]) + _dot(h_hi, rwl_ref[...]) + _dot(h_lo, rwh_ref[...])
    lane = lax.broadcasted_iota(jnp.int32, logits.shape, 1)
    logits = jnp.where(lane < n_experts, logits, -jnp.inf)
    e = jnp.exp(logits - jnp.max(logits, axis=-1, keepdims=True))
    aff = e / jnp.sum(e, axis=-1, keepdims=True)
    afft_ref[...] = aff.T[:afft_ref.shape[0]]


def _outproj_router(u, w_out, x, gate, norm_gain, shift, scale, router_pad, n_experts, bm):
    b, t, d = x.shape
    ep = -(-n_experts // SUBLANES) * SUBLANES
    router_hi = router_pad.astype(BF16)
    router_lo = (router_pad - router_hi.astype(F32)).astype(BF16)
    def row(): return pl.BlockSpec((None, bm, d), lambda bi, i: (bi, i, 0))
    def mod(): return pl.BlockSpec((None, 1, d), lambda bi, i: (bi, 0, 0))
    return pl.pallas_call(
        functools.partial(_outproj_kernel, n_experts=n_experts),
        grid=(b, t // bm),
        in_specs=[row(), _resident((d, d), lambda bi, i: (0, 0)), row(), mod(),
                  pl.BlockSpec((1, d), lambda bi, i: (0, 0)), mod(), mod(),
                  _resident((d, LANES), lambda bi, i: (0, 0)), _resident((d, LANES), lambda bi, i: (0, 0))],
        out_specs=[row(), row(), pl.BlockSpec((None, ep, bm), lambda bi, i: (bi, 0, i)), row()],
        out_shape=[jax.ShapeDtypeStruct((b, t, d), F32), jax.ShapeDtypeStruct((b, t, d), F32),
                   jax.ShapeDtypeStruct((b, ep, t), F32), jax.ShapeDtypeStruct((b, t, d), F32)],
        compiler_params=_cparams("arbitrary", "arbitrary"),
        name="outproj_router",
    )(u, w_out, x, gate, norm_gain.reshape(1, d), shift, scale, router_hi, router_lo)


def _route_kernel(aff_ref, lst_ref, gate_ref, *, cap):
    bits = pltpu.bitcast(aff_ref[...], jnp.int32)
    rows, n_blk, _ = bits.shape
    lane = lax.broadcasted_iota(jnp.int32, bits.shape, 2)
    blk = lax.broadcasted_iota(jnp.int32, bits.shape, 1)

    def count(mask):
        return jnp.sum(mask.astype(jnp.int32), axis=(1, 2), keepdims=True)

    def refine(i, lo):
        cand = lo | lax.shift_left(jnp.int32(1), 30 - i)
        return jnp.where(count(bits >= cand) >= cap, cand, lo)

    cut = lax.fori_loop(0, 31, refine, jnp.zeros((rows, 1, 1), jnp.int32))

    def block_scan(m):
        x = m
        for k in range(LANES.bit_length() - 1):
            s = 1 << k
            x = x + jnp.where(lane >= s, pltpu.roll(x, s, 2), 0)
        return x, jnp.broadcast_to(x[:, :, LANES - 1:LANES], x.shape)

    def exclusive_prefix(m):
        incl, tot = block_scan(m)
        y = tot
        k = 0
        while (1 << k) < n_blk:
            s = 1 << k
            y = y + jnp.where(blk >= s, pltpu.roll(y, s, 1), 0)
            k += 1
        return incl - m + (y - tot)

    above = bits > cut
    at_cut = bits == cut
    need = cap - count(above)
    chosen = jnp.logical_or(above, jnp.logical_and(at_cut, exclusive_prefix(at_cut.astype(jnp.int32)) < need))
    token = blk * LANES + lane
    n_tok = n_blk * LANES

    def shifted(x, s):
        if s < LANES:
            y = pltpu.roll(x, LANES - s, 2)
            y = jnp.where(lane < LANES - s, y, pltpu.roll(y, n_blk - 1, 1)) if n_blk > 1 else y
        else:
            y = pltpu.roll(x, n_blk - s // LANES, 1)
        return jnp.where(token < n_tok - s, y, 0)

    m = chosen.astype(jnp.int32)
    val = jnp.where(chosen, token, 0)
    aff = jnp.where(chosen, bits, 0)
    dist = jnp.where(chosen, token - exclusive_prefix(m), 0)
    valid = m
    for k in range((n_tok - 1).bit_length()):
        s = 1 << k
        moving = valid * ((dist >> k) & 1)
        arrive = shifted(moving, s) == 1
        stay = (valid - moving) == 1
        val = jnp.where(arrive, shifted(val, s), jnp.where(stay, val, 0))
        aff = jnp.where(arrive, shifted(aff, s), jnp.where(stay, aff, 0))
        dist = jnp.where(arrive, shifted(dist, s), jnp.where(stay, dist, 0))
        valid = jnp.logical_or(arrive, stay).astype(jnp.int32)
    lst_ref[...] = val[:, :lst_ref.shape[1], :]
    gate_ref[...] = pltpu.bitcast(aff[:, :gate_ref.shape[1], :], F32)


def _route(aff_t, cap):
    r, t = aff_t.shape
    assert r % SUBLANES == 0 and t % LANES == 0 and cap % LANES == 0
    n_blk = t // LANES
    out_spec = pl.BlockSpec((SUBLANES, cap // LANES, LANES), lambda i: (i, 0, 0))
    return pl.pallas_call(
        functools.partial(_route_kernel, cap=cap),
        grid=(r // SUBLANES,),
        in_specs=[pl.BlockSpec((SUBLANES, n_blk, LANES), lambda i: (i, 0, 0))],
        out_specs=[out_spec, out_spec],
        out_shape=[jax.ShapeDtypeStruct((r, cap // LANES, LANES), jnp.int32),
                   jax.ShapeDtypeStruct((r, cap // LANES, LANES), F32)],
        compiler_params=_cparams("arbitrary"),
        name="route",
    )(aff_t.reshape(r, n_blk, LANES))


def _ffn_kernel(lst_ref, lst_next_ref, gate_ref, h_hbm, acc_in_hbm, wg_hbm, wu_hbm, wd_hbm,
                acc_hbm, x_buf, a_buf, wg_s, wu_s, wd_s, stage, sem, wsem, *, cap, half):
    n_batch = pl.num_programs(1)
    n_experts = pl.num_programs(0)
    e = pl.program_id(0)
    b = pl.program_id(1)
    step = e * n_batch + b
    n_steps = n_experts * n_batch
    slot = step % 2
    other = 1 - slot
    b_next = jnp.where(b + 1 == n_batch, 0, b + 1)
    w_slot = e % 2
    n_stage, ch_rows, ch_cols = stage.shape

    def w_chunk(m, c):
        if m < 2:
            return (wg_hbm, wu_hbm)[m], (wg_s, wu_s)[m], (pl.ds(c * ch_rows, ch_rows),)
        return wd_hbm, wd_s, (pl.ds((c // 2) * ch_rows, ch_rows), pl.ds((c % 2) * ch_cols, ch_cols))

    def chunk_copy(m, ee, c):
        hbm, _, idx = w_chunk(m, c)
        return pltpu.make_async_copy(hbm.at[(ee,) + idx], stage.at[c % n_stage], wsem.at[c % n_stage])

    def finish_chunk(m, ee, sl, c):
        chunk_copy(m, ee, c).wait()
        _, scratch, idx = w_chunk(m, c)
        scratch.at[(sl,) + idx][...] = stage[c % n_stage].astype(BF16)
        if c + n_stage < W_CHUNKS:
            chunk_copy(m, ee, c + n_stage).start()

    def stream_ticks(first, last):
        for m in range(3):
            @pl.when(jnp.logical_and(e + 1 < n_experts, b == m))
            def _():
                for c in range(first, last):
                    if c < 0:
                        for c0 in range(n_stage):
                            chunk_copy(m, e + 1, c0).start()
                    else:
                        finish_chunk(m, e + 1, 1 - w_slot, c)

    @pl.when(step == 0)
    def _():
        for m in range(3):
            for c0 in range(n_stage):
                chunk_copy(m, 0, c0).start()
            for c in range(W_CHUNKS):
                finish_chunk(m, 0, 0, c)
    def for_chosen(lst, fn):
        def body(s8, carry):
            for r in range(SUBLANES):
                s = s8 * SUBLANES + r
                fn(s, lst[0, s])
            return carry
        lax.fori_loop(0, cap // SUBLANES, body, 0)

    def start_gather(lst, bb, sl):
        def one(s, n):
            pltpu.make_async_copy(h_hbm.at[bb, pl.ds(n, 1)], x_buf.at[sl, pl.ds(s, 1)], sem.at[0, sl]).start()
            pltpu.make_async_copy(acc_in_hbm.at[bb, pl.ds(n, 1)], a_buf.at[sl, pl.ds(s, 1)],
                                  sem.at[1, sl]).start()
        for_chosen(lst, one)

    def wait_gather(sl):
        pltpu.make_async_copy(h_hbm.at[b, pl.ds(0, cap)], x_buf.at[sl], sem.at[0, sl]).wait()
        pltpu.make_async_copy(acc_in_hbm.at[b, pl.ds(0, cap)], a_buf.at[sl], sem.at[1, sl]).wait()

    def start_scatter(sl):
        def one(s, n):
            pltpu.make_async_copy(a_buf.at[sl, pl.ds(s, 1)], acc_hbm.at[b, pl.ds(n, 1)], sem.at[2, sl]).start()
        for_chosen(lst_ref, one)

    def wait_scatter(sl):
        pltpu.make_async_copy(a_buf.at[sl], acc_hbm.at[b, pl.ds(0, cap)], sem.at[2, sl]).wait()

    @pl.when(step == 0)
    def _():
        start_gather(lst_ref, b, slot)

    stream_ticks(-1, 0)
    wait_gather(slot)
    tick = 0
    for r0 in range(0, cap, half):
        rows = pl.ds(r0, half)
        x = x_buf[slot, rows, :].astype(BF16)
        a = _dot(x, wg_s[w_slot])
        stream_ticks(tick, tick + 1)
        u = _dot(x, wu_s[w_slot])
        stream_ticks(tick + 1, tick + 2)
        y = _dot((_silu(a) * u).astype(BF16), wd_s[w_slot]) * gate_ref[rows, :]
        a_buf[slot, rows, :] = a_buf[slot, rows, :] + y
        stream_ticks(tick + 2, tick + 3)
        tick += 3
    stream_ticks(tick, W_CHUNKS)

    @pl.when(step > 0)
    def _():
        wait_scatter(other)

    @pl.when(step + 1 < n_steps)
    def _():
        start_gather(lst_next_ref, b_next, other)

    start_scatter(slot)

    @pl.when(step == n_steps - 1)
    def _():
        wait_scatter(slot)


def _expert_ffn(h2, acc0, lst, gates, w_gate, w_up, w_down):
    b, t, d = h2.shape
    e, _, f = w_gate.shape
    ep = lst.shape[0] // b
    cap = lst.shape[1] * LANES
    assert b >= 3
    assert d == 2 * f and d % (W_CHUNKS * SUBLANES) == 0
    ch_rows, ch_cols = d // W_CHUNKS, f
    half = cap // 2 if cap % 16 == 0 else cap

    def cur(ei, bi):
        return bi * ep + ei, 0, 0

    def nxt(ei, bi):
        nb = bi + 1
        return (nb % b) * ep + jnp.minimum(ei + nb // b, e - 1), 0, 0

    def smem(shape, index_map):
        return pl.BlockSpec(shape, index_map, memory_space=pltpu.SMEM)

    return pl.pallas_call(
        functools.partial(_ffn_kernel, cap=cap, half=half),
        grid=(e, b),
        in_specs=[smem((None, 1, cap), cur), smem((None, 1, cap), nxt),
                  pl.BlockSpec((None, cap, 1), cur),
                  pl.BlockSpec(memory_space=pl.ANY),
                  pl.BlockSpec(memory_space=pl.ANY),
                  pl.BlockSpec(memory_space=pl.ANY),
                  pl.BlockSpec(memory_space=pl.ANY),
                  pl.BlockSpec(memory_space=pl.ANY)],
        out_specs=pl.BlockSpec(memory_space=pl.ANY),
        out_shape=jax.ShapeDtypeStruct((b, t, d), F32),
        scratch_shapes=[pltpu.VMEM((2, cap, d), F32), pltpu.VMEM((2, cap, d), F32),
                        pltpu.VMEM((2, d, f), BF16), pltpu.VMEM((2, d, f), BF16), pltpu.VMEM((2, f, d), BF16),
                        pltpu.VMEM((W_STAGES, ch_rows, ch_cols), F32),
                        pltpu.SemaphoreType.DMA((3, 2)), pltpu.SemaphoreType.DMA((W_STAGES,))],
        input_output_aliases={4: 0},
        compiler_params=_cparams("arbitrary", "arbitrary"),
        name="expert_ffn",
    )(lst.reshape(b * ep, 1, cap), lst.reshape(b * ep, 1, cap), gates.reshape(b * ep, cap, 1),
      h2, acc0, w_gate, w_up, w_down)


def _final_kernel(x1_ref, acc_ref, g_ref, nrm_ref, o_ref):
    x = x1_ref[...] + g_ref[...] * acc_ref[...]
    o_ref[...] = x * lax.rsqrt(jnp.mean(x * x, axis=-1, keepdims=True) + NORM_EPS) * nrm_ref[...]


def _final(x1, acc, gate, norm_gain, bm):
    b, t, d = x1.shape
    def row(): return pl.BlockSpec((None, bm, d), lambda bi, i: (bi, i, 0))
    return pl.pallas_call(
        _final_kernel,
        grid=(b, t // bm),
        in_specs=[row(), row(), pl.BlockSpec((None, 1, d), lambda bi, i: (bi, 0, 0)),
                  pl.BlockSpec((1, d), lambda bi, i: (0, 0))],
        out_specs=row(),
        out_shape=jax.ShapeDtypeStruct((b, t, d), F32),
        compiler_params=_cparams("arbitrary", "arbitrary"),
        name="final_norm",
    )(x1, acc, gate, norm_gain.reshape(1, d))


def _row_block(t, largest):
    for bm in (1024, 512, 256, 128, 64):
        if bm <= largest and t % bm == 0:
            return bm
    raise ValueError(f"token count {t} is not a multiple of 64")


def kernel(x, c, ctx, c_ctx, ada_w, ada_b, norm_mix, norm_ffn, w_in, gdn_conv, gdn_a_log, gdn_dt_bias,
           hgrn_lb, hgrn_norm, gdn_norm, w_branch_a, w_branch_b, w_out, router_w, w_gate, w_up, w_down,
           final_norm):
    depth = ada_w.shape[0]
    assert depth == 1, "single-layer stack"
    b, t, d = x.shape
    tc = ctx.shape[1]
    n_experts = router_w.shape[-1]
    assert d % COL == 0 and (6 * COL) % d == 0 and t == CHUNK * GRID_W and tc % CHUNK == 0
    assert w_in.shape[-1] == 9 * COL + 4 * HEADS + 2 * d and n_experts <= LANES

    lower = jnp.cumsum(jax.nn.softmax(hgrn_lb.astype(F32), axis=1), axis=1)

    c_rows = jnp.zeros((SUBLANES, d), F32).at[:b].set(c).at[b].set(c_ctx)
    mods = _modulation(c_rows, ada_w[0], ada_b[0])
    ml = [m.reshape(b, 1, d) for m in jnp.split(mods[:b], 6, axis=-1)]
    mc = [m.reshape(1, 1, d) for m in jnp.split(mods[b:b + 1], 6, axis=-1)]

    wi = w_in[0]
    s_ab = 9 * COL
    s_gates = s_ab + 4 * HEADS
    w_main = jnp.concatenate([wi[:, :5 * COL], wi[:, 8 * COL:s_ab], wi[:, s_gates:]], axis=1).astype(BF16)
    w_qkv = wi[:, 5 * COL:8 * COL].astype(BF16)
    w_ab = jnp.pad(wi[:, s_ab:s_gates], ((0, 0), (0, LANES - 4 * HEADS))).astype(BF16)

    bm_lat, bm_ctx = _row_block(t, 1024), _row_block(tc, 1024)
    bn_main = 2 * COL if w_main.shape[1] % (2 * COL) == 0 else COL
    (p_lat,) = _input_projection(x, norm_mix[0], ml[0], ml[1], w_main, None, BF16, bm_lat, bn_main)
    qkv_lat, ab_lat = _input_projection(x, norm_mix[0], ml[0], ml[1], w_qkv, w_ab, F32, bm_lat)
    (p_ctx,) = _input_projection(ctx, norm_mix[0], mc[0], mc[1], w_main[:, :4 * COL], None, BF16, bm_ctx)
    qkv_ctx, ab_ctx = _input_projection(ctx, norm_mix[0], mc[0], mc[1], w_qkv, w_ab, F32, bm_ctx)

    oa_f, oa_b = _hgrn_scans(p_ctx, p_lat, lower[0, 0], lower[1, 0])

    x_ctx = _gdn_prep_ctx(qkv_ctx, gdn_conv[0])
    x_lat, ab_lat_cm = _gdn_prep_lat(qkv_lat, ab_lat, gdn_conv[0])
    pad = LANES - 2 * HEADS
    alog_row = jnp.pad(gdn_a_log[0].reshape(1, 2 * HEADS).astype(F32), ((0, 0), (0, pad)))
    dtb_row = jnp.pad(gdn_dt_bias[0].reshape(1, 2 * HEADS).astype(F32), ((0, 0), (0, pad)))
    ob_f, ob_b = _gdn_scans(x_ctx, x_lat, ab_ctx.reshape(b, tc // CHUNK, CHUNK, LANES), ab_lat_cm,
                            alog_row, dtb_row)

    u = _merge(p_lat, oa_f, oa_b, ob_f, ob_b, hgrn_norm[0], gdn_norm[0],
               w_branch_a[0].astype(BF16), w_branch_b[0].astype(BF16))
    router_pad = jnp.pad(router_w[0].astype(F32), ((0, 0), (0, LANES - n_experts)))
    bm_row = _row_block(t, 256)
    x1, h2, aff_t, acc0 = _outproj_router(u, w_out[0].astype(BF16), x, ml[2], norm_ffn[0], ml[3], ml[4],
                                          router_pad, n_experts, bm_row)

    cap = max(1, (CAPACITY_FACTOR * t) // n_experts)
    lst, gates = _route(aff_t.reshape(-1, t), cap)
    acc = _expert_ffn(h2, acc0, lst, gates, w_gate[0], w_up[0], w_down[0])
    return _final(x1, acc, ml[5], final_norm, bm_row)
```

```python
import functools

import jax
import jax.numpy as jnp
from jax import lax
from jax.experimental import pallas as pl
from jax.experimental.pallas import tpu as pltpu

F32 = jnp.float32
BF16 = jnp.bfloat16

NORM_EPS = 1e-6
LOG2E = 1.4426950408889634
GRID_W = 64
HEADS = 8
HEAD_DIM = 128
CHUNK = 64
CONV_K = 5
CAPACITY_FACTOR = 2
COL = HEADS * HEAD_DIM
LANES = 128
SUBLANES = 8
GROUP = SUBLANES
INV_BLOCK = 8
HGRN_SAMPLES_PER_STEP = 2
W_PER_STEP = 2
VMEM_LIMIT_BYTES = 56 * 1024 * 1024


def _cparams(*sem):
    return pltpu.CompilerParams(dimension_semantics=sem, vmem_limit_bytes=VMEM_LIMIT_BYTES)


def _dot(a, b):
    return jnp.dot(a, b, preferred_element_type=F32)


def _dot_nt(a, b):
    return lax.dot_general(a, b, (((1,), (1,)), ((), ())), preferred_element_type=F32)


def _silu(x):
    return x * jax.nn.sigmoid(x)


def _scan_cumsum(g, rev, terms):
    tri = jnp.where(_tri(rev), 1.0, 0.0).astype(BF16)
    total = None
    rest = g
    for _ in range(terms):
        part = rest.astype(BF16)
        rest = rest - part.astype(F32)
        total = _dot(tri, part) if total is None else total + _dot(tri, part)
    return total


def _resident(shape, index_map):
    return pl.BlockSpec(shape, index_map, pipeline_mode=pl.Buffered(1))


def _mod_kernel(c_ref, w_ref, b_ref, o_ref):
    o_ref[...] = _dot(_silu(c_ref[...]), w_ref[...]) + b_ref[...]


def _modulation(c_rows, ada_w, ada_b):
    rows, d = c_rows.shape
    n = ada_w.shape[1]
    bn = 1024
    return pl.pallas_call(
        _mod_kernel,
        grid=(n // bn,),
        in_specs=[pl.BlockSpec((rows, d), lambda j: (0, 0)),
                  pl.BlockSpec((d, bn), lambda j: (0, j)),
                  pl.BlockSpec((1, bn), lambda j: (0, j))],
        out_specs=pl.BlockSpec((rows, bn), lambda j: (0, j)),
        out_shape=jax.ShapeDtypeStruct((rows, n), F32),
        compiler_params=_cparams("arbitrary"),
        name="modulation",
    )(c_rows, ada_w, ada_b.reshape(1, n))


def _proj_kernel(x_ref, gain_ref, shift_ref, scale_ref, w_ref, *rest, with_ab):
    if with_ab:
        wab_ref, o_ref, oab_ref, h_ref = rest
    else:
        o_ref, h_ref = rest

    @pl.when(pl.program_id(2) == 0)
    def _():
        x = x_ref[...]
        rs = lax.rsqrt(jnp.mean(x * x, axis=-1, keepdims=True) + NORM_EPS)
        h = ((x * rs) * (gain_ref[...] * (1.0 + scale_ref[...])) + shift_ref[...]).astype(BF16)
        h_ref[...] = h
        if with_ab:
            oab_ref[...] = _dot(h, wab_ref[...])

    o_ref[...] = _dot(h_ref[...], w_ref[...]).astype(o_ref.dtype)


def _input_projection(x, gain, shift, scale, w, w_ab, out_dtype, bm, bn=COL):
    b, t, d = x.shape
    n = w.shape[1]
    assert n % bn == 0
    per_batch = shift.shape[0] > 1
    mod_map = (lambda bi, i, j: (bi, 0, 0)) if per_batch else (lambda bi, i, j: (0, 0, 0))
    with_ab = w_ab is not None
    in_specs = [pl.BlockSpec((None, bm, d), lambda bi, i, j: (bi, i, 0)),
                pl.BlockSpec((1, d), lambda bi, i, j: (0, 0)),
                pl.BlockSpec((None, 1, d), mod_map),
                pl.BlockSpec((None, 1, d), mod_map),
                pl.BlockSpec((d, bn), lambda bi, i, j: (0, j))]
    out_specs = [pl.BlockSpec((None, bm, bn), lambda bi, i, j: (bi, i, j))]
    out_shape = [jax.ShapeDtypeStruct((b, t, n), out_dtype)]
    args = [x, gain.reshape(1, d), shift, scale, w]
    if with_ab:
        in_specs.append(pl.BlockSpec((d, LANES), lambda bi, i, j: (0, 0)))
        out_specs.append(pl.BlockSpec((None, bm, LANES), lambda bi, i, j: (bi, i, 0)))
        out_shape.append(jax.ShapeDtypeStruct((b, t, LANES), F32))
        args.append(w_ab)
    return pl.pallas_call(
        functools.partial(_proj_kernel, with_ab=with_ab),
        grid=(b, t // bm, n // bn),
        in_specs=in_specs, out_specs=out_specs, out_shape=out_shape,
        scratch_shapes=[pltpu.VMEM((bm, d), BF16)],
        compiler_params=_cparams("arbitrary", "arbitrary", "arbitrary"),
        name="input_projection",
    )(*args)


def _tri(rev, strict=False):
    ri = lax.broadcasted_iota(jnp.int32, (CHUNK, CHUNK), 0)
    ci = lax.broadcasted_iota(jnp.int32, (CHUNK, CHUNK), 1)
    if rev:
        return (ci > ri) if strict else (ci >= ri)
    return (ci < ri) if strict else (ci <= ri)


def _boundary_rows(g, bs, rev):
    n, w = g.shape
    off = (bs >> 1) - 1 + (1 if rev else 0)
    if bs >= SUBLANES:
        pieces = [jnp.broadcast_to(g[p * bs + off:p * bs + off + 1], (bs, w)) for p in range(n // bs)]
        return jnp.concatenate(pieces, axis=0)
    sub = lax.broadcasted_iota(jnp.int32, (SUBLANES, w), 0)
    pieces = []
    for r0 in range(0, n, SUBLANES):
        acc = None
        for p in range(SUBLANES // bs):
            r = r0 + p * bs + off
            val = jnp.broadcast_to(g[r:r + 1], (SUBLANES, w))
            acc = val if acc is None else jnp.where(sub >= p * bs, val, acc)
        pieces.append(acc)
    return jnp.concatenate(pieces, axis=0)


def _hgrn_chunks(scans):
    row = lax.broadcasted_iota(jnp.int32, (CHUNK, 1), 0)
    ri = lax.broadcasted_iota(jnp.int32, (CHUNK, CHUNK), 0)
    ci = lax.broadcasted_iota(jnp.int32, (CHUNK, CHUNK), 1)
    n_levels = CHUNK.bit_length() - 1
    eye = ri == ci

    units = []
    for q_raw, f_raw, v, low, s_ref, rev in scans:
        q = _silu(q_raw.astype(F32))
        f_raw = f_raw.astype(F32)
        one_m = 1.0 - low
        f = low + one_m * jax.nn.sigmoid(f_raw)
        k = one_m * jax.nn.sigmoid(-f_raw)
        gc = _scan_cumsum(jnp.log(f), rev, 2) * LOG2E
        g_last = gc[0:1] if rev else gc[CHUNK - 1:CHUNK]
        q_dec = (q * jnp.exp2(gc)).astype(BF16)
        k_dec = (k * jnp.exp2(g_last - gc)).astype(BF16)
        mixed, pair = [], []
        for lv in range(1, n_levels + 1):
            upper = ((row >> (lv - 1)) & 1) == 1
            q_rows = jnp.logical_not(upper) if rev else upper
            decay = jnp.exp2(-jnp.abs(gc - _boundary_rows(gc, 1 << lv, rev)))
            mixed.append((jnp.where(q_rows, q, k) * decay).astype(BF16))
            i_upper = ((ri >> (lv - 1)) & 1) == 1
            j_upper = ((ci >> (lv - 1)) & 1) == 1
            halves = jnp.logical_and(j_upper, jnp.logical_not(i_upper)) if rev else \
                jnp.logical_and(i_upper, jnp.logical_not(j_upper))
            pair.append(jnp.logical_and((ri >> lv) == (ci >> lv), halves))
        qk = q * k
        v_b = v.astype(BF16)
        v_f = v.astype(F32)
        for h in range(HEADS):
            sl = slice(h * HEAD_DIM, (h + 1) * HEAD_DIM)
            units.append(dict(s_ref=s_ref, h=h, diag=jnp.sum(qk[:, sl], axis=1, keepdims=True),
                              mixed=[x[:, sl] for x in mixed], pair=pair,
                              q_dec=q_dec[:, sl], k_dec=k_dec[:, sl], v_b=v_b[:, sl], v_f=v_f[:, sl],
                              decay=jnp.exp2(g_last[:, sl])))

    scores = [[_dot_nt(m, m) for m in u["mixed"]] for u in units]
    states = [u["s_ref"][u["h"]] for u in units]
    carried = [_dot_nt(u["q_dec"], s.astype(BF16)) for u, s in zip(units, states)]
    outs = []
    for u, sc, car in zip(units, scores, carried):
        a = jnp.where(eye, u["diag"], 0.0)
        for lv in range(n_levels):
            a = jnp.where(u["pair"][lv], sc[lv], a)
        outs.append(_dot(a.astype(BF16), u["v_b"]) + car)
    for u, s in zip(units, states):
        u["s_ref"][u["h"]] = u["decay"] * s + _dot(u["v_f"].T.astype(BF16), u["k_dec"])
    return [jnp.concatenate(outs[i * HEADS:(i + 1) * HEADS], axis=1) for i in range(len(scans))]


def _hgrn_kernel(lowf_ref, lowb_ref,
                 cqf, cff, cvf, cqb, cfb, cvb,
                 lqf, lff, lvf, lqb, lfb, lvb,
                 of_ref, ob_ref, sf_ref, sb_ref, *, n_ctx, per_step):
    t = pl.program_id(1)
    n_samples = of_ref.shape[0]

    @pl.when(t == 0)
    def _():
        sf_ref[...] = jnp.zeros_like(sf_ref)
        sb_ref[...] = jnp.zeros_like(sb_ref)

    is_ctx = t < n_ctx

    def body(j, carry):
        rows_f = pl.ds(pl.multiple_of(j * CHUNK, CHUNK), CHUNK)
        rows_b = pl.ds(pl.multiple_of((per_step - 1 - j) * CHUNK, CHUNK), CHUNK)

        def pick(c_ref, l_ref, i, rows):
            return jnp.where(is_ctx, c_ref[i, rows, :], l_ref[i, rows, :])

        scans = []
        for i in range(n_samples):
            scans.append((pick(cqf, lqf, i, rows_f), pick(cff, lff, i, rows_f), pick(cvf, lvf, i, rows_f),
                          lowf_ref[...], sf_ref.at[i], False))
            scans.append((pick(cqb, lqb, i, rows_b), pick(cfb, lfb, i, rows_b), pick(cvb, lvb, i, rows_b),
                          lowb_ref[...], sb_ref.at[i], True))
        outs = _hgrn_chunks(scans)
        for i in range(n_samples):
            of_ref[i, rows_f, :] = outs[2 * i].astype(of_ref.dtype)
            ob_ref[i, rows_b, :] = outs[2 * i + 1].astype(ob_ref.dtype)
        return carry

    lax.fori_loop(0, per_step, body, 0)


def _hgrn_scans(p_ctx, p_lat, low_f, low_b):
    b, tc, _ = p_ctx.shape
    t = p_lat.shape[1]
    per_step = max(g for g in (4, 2, 1) if (tc // CHUNK) % g == 0 and (t // CHUNK) % g == 0)
    rows = per_step * CHUNK
    n_ctx, n_lat = tc // rows, t // rows

    def cf(ti): return jnp.minimum(ti, n_ctx - 1)
    def cb(ti): return jnp.maximum(n_ctx - 1 - ti, 0)
    def lf(ti): return jnp.maximum(ti - n_ctx, 0)
    def lb(ti): return n_lat - 1 - jnp.maximum(ti - n_ctx, 0)

    nb = HGRN_SAMPLES_PER_STEP if b % HGRN_SAMPLES_PER_STEP == 0 else 1

    def spec(block_of, col):
        return pl.BlockSpec((nb, rows, COL), lambda bi, ti: (bi, block_of(ti), col))

    low_spec = pl.BlockSpec((1, COL), lambda bi, ti: (0, 0))
    out_f, out_b = pl.pallas_call(
        functools.partial(_hgrn_kernel, n_ctx=n_ctx, per_step=per_step),
        grid=(b // nb, n_ctx + n_lat),
        in_specs=[low_spec, low_spec,
                  spec(cf, 0), spec(cf, 1), spec(cf, 3), spec(cb, 0), spec(cb, 2), spec(cb, 3),
                  spec(lf, 0), spec(lf, 1), spec(lf, 3), spec(lb, 0), spec(lb, 2), spec(lb, 3)],
        out_specs=[pl.BlockSpec((nb, rows, COL), lambda bi, ti: (bi, lf(ti), 0)),
                   pl.BlockSpec((nb, rows, COL), lambda bi, ti: (bi, lb(ti), 0))],
        out_shape=[jax.ShapeDtypeStruct((b, t, COL), BF16)] * 2,
        scratch_shapes=[pltpu.VMEM((nb, HEADS, HEAD_DIM, HEAD_DIM), F32)] * 2,
        compiler_params=_cparams("arbitrary", "arbitrary"),
        name="hgrn_scans",
    )(low_f.reshape(1, COL), low_b.reshape(1, COL), *([p_ctx] * 6), *([p_lat] * 6))
    return out_f, out_b


def _conv_silu_norm(x, prev, nxt, w, idx):
    row = lax.broadcasted_iota(jnp.int32, (CHUNK, 1), 0)
    xm1 = jnp.where(row == 0, prev[7:8], pltpu.roll(x, 1, 0))
    xm2 = jnp.where(row == 0, prev[6:7], jnp.where(row == 1, prev[7:8], pltpu.roll(x, 2, 0)))
    xp1 = jnp.where(row == CHUNK - 1, nxt[0:1], pltpu.roll(x, CHUNK - 1, 0))
    xp2 = jnp.where(row == CHUNK - 1, nxt[1:2],
                    jnp.where(row == CHUNK - 2, nxt[0:1], pltpu.roll(x, CHUNK - 2, 0)))
    return _silu_norm(w[0:1] * xm2 + w[1:2] * xm1 + w[2:3] * x + w[3:4] * xp1 + w[4:5] * xp2, idx)


def _silu_norm(y, idx):
    y = _silu(y)
    if idx == 2:
        return y
    scale = HEAD_DIM ** -0.5 if idx == 0 else 1.0
    pieces = []
    for h in range(HEADS):
        yh = y[:, h * HEAD_DIM:(h + 1) * HEAD_DIM]
        pieces.append(yh * (lax.rsqrt(jnp.sum(yh * yh, axis=1, keepdims=True) + NORM_EPS) * scale))
    return jnp.concatenate(pieces, axis=1)


def _gdn_prep_ctx_kernel(q_ref, k_ref, v_ref, pq_ref, pk_ref, pv_ref, nq_ref, nk_ref, nv_ref, w_ref, o_ref):
    t = pl.program_id(1)
    first = t == 0
    last = t == pl.num_programs(1) - 1
    for idx, (c_ref, p_ref, n_ref) in enumerate(((q_ref, pq_ref, nq_ref), (k_ref, pk_ref, nk_ref),
                                                 (v_ref, pv_ref, nv_ref))):
        prev = jnp.where(first, 0.0, p_ref[...])
        nxt = jnp.where(last, 0.0, n_ref[...])
        y = _conv_silu_norm(c_ref[...], prev, nxt, w_ref[:, idx * COL:(idx + 1) * COL], idx)
        o_ref[:, idx * COL:(idx + 1) * COL] = y.astype(o_ref.dtype)


def _gdn_prep_ctx(qkv, conv_w):
    b, t, _ = qkv.shape
    n_chunks = t // CHUNK
    rb = CHUNK // SUBLANES
    def cur(col): return pl.BlockSpec((None, CHUNK, COL), lambda bi, c: (bi, c, col))
    def prv(col): return pl.BlockSpec((None, SUBLANES, COL), lambda bi, c: (bi, jnp.maximum(c * rb - 1, 0), col))
    def nxt(col): return pl.BlockSpec((None, SUBLANES, COL),
                                      lambda bi, c: (bi, jnp.minimum((c + 1) * rb, n_chunks * rb - 1), col))
    cols = (0, 1, 2)
    return pl.pallas_call(
        _gdn_prep_ctx_kernel,
        grid=(b, n_chunks),
        in_specs=[cur(c) for c in cols] + [prv(c) for c in cols] + [nxt(c) for c in cols]
                 + [pl.BlockSpec((CONV_K, 3 * COL), lambda bi, c: (0, 0))],
        out_specs=pl.BlockSpec((None, None, CHUNK, 3 * COL), lambda bi, c: (bi, c, 0, 0)),
        out_shape=jax.ShapeDtypeStruct((b, n_chunks, CHUNK, 3 * COL), BF16),
        compiler_params=_cparams("arbitrary", "arbitrary"),
        name="gdn_prep_ctx",
    )(*([qkv] * 9), conv_w)


def _gdn_prep_lat_kernel(q_ref, k_ref, v_ref, pq_ref, pk_ref, pv_ref, nq_ref, nk_ref, nv_ref, ab_ref, w_ref,
                         o_ref, oab_ref):
    g = pl.program_id(1)
    first = g == 0
    last = g == pl.num_programs(1) - 1
    tail = CHUNK - SUBLANES
    for idx, (c_ref, p_ref, n_ref) in enumerate(((q_ref, pq_ref, nq_ref), (k_ref, pk_ref, nk_ref),
                                                 (v_ref, pv_ref, nv_ref))):
        w = w_ref[:, idx * COL:(idx + 1) * COL]
        cols = [c_ref[:, j, :] for j in range(GROUP)]
        before = jnp.where(first, 0.0, p_ref[:, GROUP - 1, :])
        after = jnp.where(last, 0.0, n_ref[:, 0, :])
        for j in range(GROUP):
            prev = cols[j - 1][tail:] if j > 0 else before
            nxt = cols[j + 1][:SUBLANES] if j < GROUP - 1 else after
            o_ref[j, :, idx * COL:(idx + 1) * COL] = _conv_silu_norm(cols[j], prev, nxt, w, idx).astype(o_ref.dtype)
    for j in range(GROUP):
        oab_ref[j] = ab_ref[:, j, :]


def _gdn_prep_lat(qkv, ab, conv_w):
    b, t, _ = qkv.shape
    assert t == CHUNK * GRID_W and GRID_W % GROUP == 0
    n_groups = GRID_W // GROUP
    rb = CHUNK // SUBLANES
    q4 = qkv.reshape(b, CHUNK, GRID_W, 3 * COL)
    ab4 = ab.reshape(b, CHUNK, GRID_W, LANES)
    def cur(col): return pl.BlockSpec((None, CHUNK, GROUP, COL), lambda bi, g: (bi, 0, g, col))
    def prv(col): return pl.BlockSpec((None, SUBLANES, GROUP, COL),
                                      lambda bi, g: (bi, rb - 1, jnp.maximum(g - 1, 0), col))
    def nxt(col): return pl.BlockSpec((None, SUBLANES, GROUP, COL),
                                      lambda bi, g: (bi, 0, jnp.minimum(g + 1, n_groups - 1), col))
    cols = (0, 1, 2)
    return pl.pallas_call(
        _gdn_prep_lat_kernel,
        grid=(b, n_groups),
        in_specs=[cur(c) for c in cols] + [prv(c) for c in cols] + [nxt(c) for c in cols]
                 + [pl.BlockSpec((None, CHUNK, GROUP, LANES), lambda bi, g: (bi, 0, g, 0)),
                    pl.BlockSpec((CONV_K, 3 * COL), lambda bi, g: (0, 0))],
        out_specs=[pl.BlockSpec((None, GROUP, CHUNK, 3 * COL), lambda bi, g: (bi, g, 0, 0)),
                   pl.BlockSpec((None, GROUP, CHUNK, LANES), lambda bi, g: (bi, g, 0, 0))],
        out_shape=[jax.ShapeDtypeStruct((b, GRID_W, CHUNK, 3 * COL), BF16),
                   jax.ShapeDtypeStruct((b, GRID_W, CHUNK, LANES), F32)],
        compiler_params=_cparams("arbitrary", "arbitrary"),
        name="gdn_prep_lat",
    )(*([q4] * 9), ab4, conv_w)


def _mm_bf16(a, b):
    return _dot(a.astype(BF16), b.astype(BF16))


def _unit_tri_inverses(ns):
    ri = lax.broadcasted_iota(jnp.int32, (CHUNK, CHUNK), 0)
    ci = lax.broadcasted_iota(jnp.int32, (CHUNK, CHUNK), 1)
    eye = jnp.where(ri == ci, 1.0, 0.0)
    shift = INV_BLOCK.bit_length() - 1
    on_diag = (ri >> shift) == (ci >> shift)

    def nilpotent_inverses(ms, degree):
        xs = [eye - m for m in ms]
        ps = ms
        for _ in range(degree.bit_length() - 2):
            ps = [_mm_bf16(p, p) for p in ps]
            xs = [x + _mm_bf16(x, p) for x, p in zip(xs, ps)]
        return xs

    d_invs = nilpotent_inverses([jnp.where(on_diag, n, 0.0) for n in ns], INV_BLOCK)
    ms = [_mm_bf16(d, jnp.where(on_diag, 0.0, n)) for d, n in zip(d_invs, ns)]
    ys = nilpotent_inverses(ms, CHUNK // INV_BLOCK)
    return [_mm_bf16(y, d) for y, d in zip(ys, d_invs)]


def _gdn_chunks(scans, alog, dtb):
    units = []
    for x, ab, s_ref, rev in scans:
        g_all = -jnp.exp(alog) * jax.nn.softplus(ab + dtb)
        beta_all = jax.nn.sigmoid(ab)
        gc_all = _scan_cumsum(g_all, rev, 3)
        gr_all = gc_all.T
        incl = _tri(rev)
        strict = _tri(rev, strict=True)
        off = HEADS if rev else 0
        for h in range(HEADS):
            gcol = gc_all[:, off + h:off + h + 1]
            grow = gr_all[off + h:off + h + 1, :]
            units.append(dict(
                s_ref=s_ref, h=h, strict=strict, gcol=gcol,
                beta=beta_all[:, 2 * HEADS + off + h:2 * HEADS + off + h + 1],
                g_last=gcol[0:1] if rev else gcol[CHUNK - 1:CHUNK],
                gamma=jnp.where(incl, jnp.exp(jnp.where(incl, gcol - grow, 0.0)), 0.0),
                q=x[:, h * HEAD_DIM:(h + 1) * HEAD_DIM],
                k=x[:, COL + h * HEAD_DIM:COL + (h + 1) * HEAD_DIM],
                v=x[:, 2 * COL + h * HEAD_DIM:2 * COL + (h + 1) * HEAD_DIM]))

    both = [_dot_nt(jnp.concatenate([u["q"], u["k"]], axis=0), u["k"]) for u in units]
    qks = [x[:CHUNK] for x in both]
    ns = [jnp.where(u["strict"], u["beta"] * x[CHUNK:] * u["gamma"], 0.0) for u, x in zip(units, both)]
    invs = _unit_tri_inverses(ns)
    rhss = [jnp.concatenate([u["v"].astype(F32) * u["beta"],
                             u["k"].astype(F32) * (u["beta"] * jnp.exp(u["gcol"]))], axis=1) for u in units]
    sols = [_mm_bf16(a, r) for a, r in zip(invs, rhss)]
    states = [u["s_ref"][u["h"]] for u in units]
    states_b = [s.astype(BF16) for s in states]
    v_news = [sol[:, :HEAD_DIM] - _dot_nt(sol[:, HEAD_DIM:].astype(BF16), s_b) for sol, s_b in zip(sols, states_b)]
    outs = [_dot_nt((u["q"].astype(F32) * jnp.exp(u["gcol"])).astype(BF16), s_b)
            + _mm_bf16(qk * u["gamma"], v_new)
            for u, qk, s_b, v_new in zip(units, qks, states_b, v_news)]
    for u, s, v_new in zip(units, states, v_news):
        k_dec = (u["k"].astype(F32) * jnp.exp(u["g_last"] - u["gcol"])).astype(BF16)
        u["s_ref"][u["h"]] = jnp.exp(u["g_last"]) * s + _dot(v_new.T.astype(BF16), k_dec)
    return [jnp.concatenate(outs[i * HEADS:(i + 1) * HEADS], axis=1) for i in range(len(scans))]


def _gdn_ctx_kernel(alog_ref, dtb_ref, x_ref, ab_ref, sf_ref, sb_ref, *, n_chunks):
    sf_ref[...] = jnp.zeros_like(sf_ref)
    sb_ref[...] = jnp.zeros_like(sb_ref)

    def body(j, carry):
        jb = n_chunks - 1 - j
        _gdn_chunks([(x_ref[j], ab_ref[j], sf_ref, False), (x_ref[jb], ab_ref[jb], sb_ref, True)],
                    alog_ref[...], dtb_ref[...])
        return carry

    lax.fori_loop(0, n_chunks, body, 0)


def _gdn_lat_kernel(alog_ref, dtb_ref, s0f_ref, s0b_ref, xf_ref, abf_ref, xb_ref, abb_ref,
                    of_ref, ob_ref, sf_ref, sb_ref):
    n_samples, per_step = xf_ref.shape[:2]

    @pl.when(pl.program_id(1) == 0)
    def _():
        sf_ref[...] = s0f_ref[...]
        sb_ref[...] = s0b_ref[...]

    def body(j, carry):
        jb = per_step - 1 - j
        scans = []
        for i in range(n_samples):
            scans.append((xf_ref[i, j], abf_ref[i, j], sf_ref.at[i], False))
            scans.append((xb_ref[i, jb], abb_ref[i, jb], sb_ref.at[i], True))
        outs = _gdn_chunks(scans, alog_ref[...], dtb_ref[...])
        for i in range(n_samples):
            of_ref[i, j] = outs[2 * i]
            ob_ref[i, jb] = outs[2 * i + 1]
        return carry

    lax.fori_loop(0, per_step, body, 0)


def _gdn_scans(x_ctx, x_lat, ab_ctx, ab_lat, alog_row, dtb_row):
    b, n_ctx = x_ctx.shape[:2]
    n_lat = x_lat.shape[1]
    state_shape = jax.ShapeDtypeStruct((b, HEADS, HEAD_DIM, HEAD_DIM), F32)
    def state_spec(nidx):
        return pl.BlockSpec((None, HEADS, HEAD_DIM, HEAD_DIM), lambda *i: (i[0], 0, 0, 0))
    s_f, s_b = pl.pallas_call(
        functools.partial(_gdn_ctx_kernel, n_chunks=n_ctx),
        grid=(b,),
        in_specs=[pl.BlockSpec((1, LANES), lambda bi: (0, 0)), pl.BlockSpec((1, LANES), lambda bi: (0, 0)),
                  pl.BlockSpec((None, n_ctx, CHUNK, 3 * COL), lambda bi: (bi, 0, 0, 0)),
                  pl.BlockSpec((None, n_ctx, CHUNK, LANES), lambda bi: (bi, 0, 0, 0))],
        out_specs=[state_spec(1), state_spec(1)],
        out_shape=[state_shape, state_shape],
        compiler_params=_cparams("arbitrary"),
        name="gdn_ctx_scans",
    )(alog_row, dtb_row, x_ctx, ab_ctx)

    nb = 2 if b % 2 == 0 else 1
    per_step = max(g for g in (4, 2, 1) if n_lat % g == 0)
    n_steps = n_lat // per_step
    def gf(ti): return ti
    def gb(ti): return n_steps - 1 - ti
    def lat_state_spec():
        return pl.BlockSpec((nb, HEADS, HEAD_DIM, HEAD_DIM), lambda bi, ti: (bi, 0, 0, 0))
    def xspec(group_of):
        return pl.BlockSpec((nb, per_step, CHUNK, 3 * COL), lambda bi, ti: (bi, group_of(ti), 0, 0))
    def abspec(group_of):
        return pl.BlockSpec((nb, per_step, CHUNK, LANES), lambda bi, ti: (bi, group_of(ti), 0, 0))
    def ospec(group_of):
        return pl.BlockSpec((nb, per_step, CHUNK, COL), lambda bi, ti: (bi, group_of(ti), 0, 0))
    row_spec = pl.BlockSpec((1, LANES), lambda bi, ti: (0, 0))
    return pl.pallas_call(
        _gdn_lat_kernel,
        grid=(b // nb, n_steps),
        in_specs=[row_spec, row_spec, lat_state_spec(), lat_state_spec(),
                  xspec(gf), abspec(gf), xspec(gb), abspec(gb)],
        out_specs=[ospec(gf), ospec(gb)],
        out_shape=[jax.ShapeDtypeStruct((b, n_lat, CHUNK, COL), F32)] * 2,
        scratch_shapes=[pltpu.VMEM((nb, HEADS, HEAD_DIM, HEAD_DIM), F32)] * 2,
        compiler_params=_cparams("arbitrary", "arbitrary"),
        name="gdn_lat_scans",
    )(alog_row, dtb_row, s_f, s_b, x_lat, ab_lat, x_lat, ab_lat)


def _head_norm(x):
    pieces = []
    for h in range(HEADS):
        xh = x[:, h * HEAD_DIM:(h + 1) * HEAD_DIM]
        pieces.append(xh * lax.rsqrt(jnp.mean(xh * xh, axis=1, keepdims=True) + NORM_EPS))
    return jnp.concatenate(pieces, axis=1)


def _merge_kernel(oaf, oab, og, obf, obb, z, ga, gb, hag, gbg, wba, wbb, u_ref):
    o_a = oaf[...].astype(F32) + oab[...].astype(F32)
    y_a = _head_norm(o_a * jax.nn.sigmoid(og[...].astype(F32))) * hag[...]
    o_b = jnp.concatenate([obf[:, r, :] + obb[:, r, :] for r in range(obf.shape[1])], axis=0)
    y_b = _head_norm(o_b) * gbg[...] * _silu(z[...].astype(F32))
    u = jax.nn.sigmoid(ga[...].astype(F32)) * _dot(y_a.astype(BF16), wba[...])
    u = u + jax.nn.sigmoid(gb[...].astype(F32)) * _dot(y_b.astype(BF16), wbb[...])
    u_ref[...] = u.astype(u_ref.dtype)


def _merge(p_lat, oa_f, oa_b, ob_f, ob_b, ha_gain, gb_gain, w_ba, w_bb):
    b, t, _ = p_lat.shape
    d = w_ba.shape[1]
    bm = SUBLANES * GRID_W
    g0 = 6 * COL // d
    def o_spec(): return pl.BlockSpec((None, bm, COL), lambda bi, i: (bi, i, 0))
    def ob_spec(): return pl.BlockSpec((None, GRID_W, SUBLANES, COL), lambda bi, i: (bi, 0, i, 0))
    def p_spec(col): return pl.BlockSpec((None, bm, COL), lambda bi, i: (bi, i, col))
    def gate_spec(k): return pl.BlockSpec((None, bm, d), lambda bi, i: (bi, i, g0 + k))
    def full(shape): return _resident(shape, lambda bi, i: (0,) * len(shape))
    return pl.pallas_call(
        _merge_kernel,
        grid=(b, t // bm),
        in_specs=[o_spec(), o_spec(), p_spec(4), ob_spec(), ob_spec(), p_spec(5),
                  gate_spec(0), gate_spec(1), full((1, COL)), full((1, COL)),
                  full((COL, d)), full((COL, d))],
        out_specs=pl.BlockSpec((None, bm, d), lambda bi, i: (bi, i, 0)),
        out_shape=jax.ShapeDtypeStruct((b, t, d), BF16),
        compiler_params=_cparams("arbitrary", "arbitrary"),
        name="merge",
    )(oa_f, oa_b, p_lat, ob_f, ob_b, p_lat, p_lat, p_lat,
      ha_gain.reshape(1, COL), gb_gain.reshape(1, COL), w_ba, w_bb)


def _outproj_kernel(u_ref, wout_ref, x_ref, g_ref, nrm_ref, sh_ref, sc_ref, rwh_ref, rwl_ref,
                    x1_ref, h2_ref, afft_ref, acc0_ref, *, n_experts):
    acc0_ref[...] = jnp.zeros_like(acc0_ref)
    x1 = x_ref[...] + g_ref[...] * _dot(u_ref[...], wout_ref[...])
    x1_ref[...] = x1
    rs = lax.rsqrt(jnp.mean(x1 * x1, axis=-1, keepdims=True) + NORM_EPS)
    h2 = (x1 * rs) * (nrm_ref[...] * (1.0 + sc_ref[...])) + sh_ref[...]
    h2_ref[...] = h2
    h_hi = h2.astype(BF16)
    h_lo = (h2 - h_hi.astype(F32)).astype(BF16)
    logits = _dot(h_hi, rwh_ref[...]) + _dot(h_hi, rwl_ref[...]) + _dot(h_lo, rwh_ref[...])
    lane = lax.broadcasted_iota(jnp.int32, logits.shape, 1)
    logits = jnp.where(lane < n_experts, logits, -jnp.inf)
    e = jnp.exp(logits - jnp.max(logits, axis=-1, keepdims=True))
    aff = e / jnp.sum(e, axis=-1, keepdims=True)
    afft_ref[...] = aff.T[:afft_ref.shape[0]]


def _outproj_router(u, w_out, x, gate, norm_gain, shift, scale, router_pad, n_experts, bm):
    b, t, d = x.shape
    ep = -(-n_experts // SUBLANES) * SUBLANES
    router_hi = router_pad.astype(BF16)
    router_lo = (router_pad - router_hi.astype(F32)).astype(BF16)
    def row(): return pl.BlockSpec((None, bm, d), lambda bi, i: (bi, i, 0))
    def mod(): return pl.BlockSpec((None, 1, d), lambda bi, i: (bi, 0, 0))
    return pl.pallas_call(
        functools.partial(_outproj_kernel, n_experts=n_experts),
        grid=(b, t // bm),
        in_specs=[row(), _resident((d, d), lambda bi, i: (0, 0)), row(), mod(),
                  pl.BlockSpec((1, d), lambda bi, i: (0, 0)), mod(), mod(),
                  _resident((d, LANES), lambda bi, i: (0, 0)), _resident((d, LANES), lambda bi, i: (0, 0))],
        out_specs=[row(), row(), pl.BlockSpec((None, ep, bm), lambda bi, i: (bi, 0, i)), row()],
        out_shape=[jax.ShapeDtypeStruct((b, t, d), F32), jax.ShapeDtypeStruct((b, t, d), F32),
                   jax.ShapeDtypeStruct((b, ep, t), F32), jax.ShapeDtypeStruct((b, t, d), F32)],
        compiler_params=_cparams("arbitrary", "arbitrary"),
        name="outproj_router",
    )(u, w_out, x, gate, norm_gain.reshape(1, d), shift, scale, router_hi, router_lo)


def _route_kernel(aff_ref, lst_ref, gate_ref, *, cap):
    bits = pltpu.bitcast(aff_ref[...], jnp.int32)
    rows, n_blk, _ = bits.shape
    lane = lax.broadcasted_iota(jnp.int32, bits.shape, 2)
    blk = lax.broadcasted_iota(jnp.int32, bits.shape, 1)

    def count(mask):
        return jnp.sum(mask.astype(jnp.int32), axis=(1, 2), keepdims=True)

    def refine(i, lo):
        cand = lo | lax.shift_left(jnp.int32(1), 30 - i)
        return jnp.where(count(bits >= cand) >= cap, cand, lo)

    cut = lax.fori_loop(0, 31, refine, jnp.zeros((rows, 1, 1), jnp.int32))

    def block_scan(m):
        x = m
        for k in range(LANES.bit_length() - 1):
            s = 1 << k
            x = x + jnp.where(lane >= s, pltpu.roll(x, s, 2), 0)
        return x, jnp.broadcast_to(x[:, :, LANES - 1:LANES], x.shape)

    def exclusive_prefix(m):
        incl, tot = block_scan(m)
        y = tot
        k = 0
        while (1 << k) < n_blk:
            s = 1 << k
            y = y + jnp.where(blk >= s, pltpu.roll(y, s, 1), 0)
            k += 1
        return incl - m + (y - tot)

    above = bits > cut
    at_cut = bits == cut
    need = cap - count(above)
    chosen = jnp.logical_or(above, jnp.logical_and(at_cut, exclusive_prefix(at_cut.astype(jnp.int32)) < need))
    token = blk * LANES + lane
    n_tok = n_blk * LANES

    def shifted(x, s):
        if s < LANES:
            y = pltpu.roll(x, LANES - s, 2)
            y = jnp.where(lane < LANES - s, y, pltpu.roll(y, n_blk - 1, 1)) if n_blk > 1 else y
        else:
            y = pltpu.roll(x, n_blk - s // LANES, 1)
        return jnp.where(token < n_tok - s, y, 0)

    m = chosen.astype(jnp.int32)
    val = jnp.where(chosen, token, 0)
    aff = jnp.where(chosen, bits, 0)
    dist = jnp.where(chosen, token - exclusive_prefix(m), 0)
    valid = m
    for k in range((n_tok - 1).bit_length()):
        s = 1 << k
        moving = valid * ((dist >> k) & 1)
        arrive = shifted(moving, s) == 1
        stay = (valid - moving) == 1
        val = jnp.where(arrive, shifted(val, s), jnp.where(stay, val, 0))
        aff = jnp.where(arrive, shifted(aff, s), jnp.where(stay, aff, 0))
        dist = jnp.where(arrive, shifted(dist, s), jnp.where(stay, dist, 0))
        valid = jnp.logical_or(arrive, stay).astype(jnp.int32)
    lst_ref[...] = val[:, :lst_ref.shape[1], :]
    gate_ref[...] = pltpu.bitcast(aff[:, :gate_ref.shape[1], :], F32)


def _route(aff_t, cap):
    r, t = aff_t.shape
    assert r % SUBLANES == 0 and t % LANES == 0 and cap % LANES == 0
    n_blk = t // LANES
    out_spec = pl.BlockSpec((SUBLANES, cap // LANES, LANES), lambda i: (i, 0, 0))
    return pl.pallas_call(
        functools.partial(_route_kernel, cap=cap),
        grid=(r // SUBLANES,),
        in_specs=[pl.BlockSpec((SUBLANES, n_blk, LANES), lambda i: (i, 0, 0))],
        out_specs=[out_spec, out_spec],
        out_shape=[jax.ShapeDtypeStruct((r, cap // LANES, LANES), jnp.int32),
                   jax.ShapeDtypeStruct((r, cap // LANES, LANES), F32)],
        compiler_params=_cparams("arbitrary"),
        name="route",
    )(aff_t.reshape(r, n_blk, LANES))


def _ffn_kernel(lst_ref, lst_next_ref, gate_ref, h_hbm, acc_in_hbm, wg_hbm, wu_hbm, wd_hbm,
                acc_hbm, x_buf, a_buf, wg_s, wu_s, wd_s, stage, sem, wsem, *, cap, half, n_experts, n_batch):
    e = pl.program_id(0)
    b = pl.program_id(1)
    step = e * n_batch + b
    n_steps = n_experts * n_batch
    slot = step % 2
    other = 1 - slot
    b_next = jnp.where(b + 1 == n_batch, 0, b + 1)
    w_slot = e % 2
    n_tasks, ch_rows, ch_cols = stage.shape

    def task(k, bb):
        m, cc = divmod(k, W_PER_STEP)
        row0 = (bb * W_PER_STEP + cc if m < 2 else bb) * ch_rows
        rows = pl.ds(row0 if isinstance(row0, int) else pl.multiple_of(row0, ch_rows), ch_rows)
        if m < 2:
            return (wg_hbm, wu_hbm)[m], (wg_s, wu_s)[m], (rows,)
        return wd_hbm, wd_s, (rows, pl.ds(cc * ch_cols, ch_cols))

    def task_copy(k, ee, bb):
        hbm, _, idx = task(k, bb)
        return pltpu.make_async_copy(hbm.at[(ee,) + idx], stage.at[k], wsem.at[k])

    def finish_task(k, ee, sl, bb):
        task_copy(k, ee, bb).wait()
        _, scratch, idx = task(k, bb)
        scratch.at[(sl,) + idx][...] = stage[k].astype(BF16)

    @pl.when(step == 0)
    def _():
        for bb in range(n_batch):
            for k in range(n_tasks):
                task_copy(k, 0, bb).start()
            for k in range(n_tasks):
                finish_task(k, 0, 0, bb)

    e_next = jnp.minimum(e + 1, n_experts - 1)
    for k in range(n_tasks):
        task_copy(k, e_next, b).start(priority=1)

    def for_chosen(lst, fn):
        def body(s8, carry):
            for r in range(SUBLANES):
                s = s8 * SUBLANES + r
                fn(s, lst[0, s])
            return carry
        lax.fori_loop(0, cap // SUBLANES, body, 0)

    def start_gather(lst, bb, sl):
        def one(s, n):
            pltpu.make_async_copy(h_hbm.at[bb, pl.ds(n, 1)], x_buf.at[sl, pl.ds(s, 1)], sem.at[0, sl]).start()
            pltpu.make_async_copy(acc_in_hbm.at[bb, pl.ds(n, 1)], a_buf.at[sl, pl.ds(s, 1)],
                                  sem.at[1, sl]).start()
        for_chosen(lst, one)

    def wait_gather(sl):
        pltpu.make_async_copy(h_hbm.at[b, pl.ds(0, cap)], x_buf.at[sl], sem.at[0, sl]).wait()
        pltpu.make_async_copy(acc_in_hbm.at[b, pl.ds(0, cap)], a_buf.at[sl], sem.at[1, sl]).wait()

    def start_scatter(sl):
        def one(s, n):
            pltpu.make_async_copy(a_buf.at[sl, pl.ds(s, 1)], acc_hbm.at[b, pl.ds(n, 1)], sem.at[2, sl]).start()
        for_chosen(lst_ref, one)

    def wait_scatter(sl):
        pltpu.make_async_copy(a_buf.at[sl], acc_hbm.at[b, pl.ds(0, cap)], sem.at[2, sl]).wait()

    @pl.when(step == 0)
    def _():
        start_gather(lst_ref, b, slot)

    wait_gather(slot)
    pending = list(range(n_tasks))

    def convert_one():
        if pending:
            finish_task(pending.pop(0), e_next, 1 - w_slot, b)

    for r0 in range(0, cap, half):
        rows = pl.ds(r0, half)
        x = x_buf[slot, rows, :].astype(BF16)
        a = _dot(x, wg_s[w_slot])
        convert_one()
        u = _dot(x, wu_s[w_slot])
        convert_one()
        y = _dot((_silu(a) * u).astype(BF16), wd_s[w_slot]) * gate_ref[rows, :]
        a_buf[slot, rows, :] = a_buf[slot, rows, :] + y
        convert_one()
    while pending:
        convert_one()

    @pl.when(step > 0)
    def _():
        wait_scatter(other)

    @pl.when(step + 1 < n_steps)
    def _():
        start_gather(lst_next_ref, b_next, other)

    start_scatter(slot)

    @pl.when(step == n_steps - 1)
    def _():
        wait_scatter(slot)


def _expert_ffn(h2, acc0, lst, gates, w_gate, w_up, w_down):
    b, t, d = h2.shape
    e, _, f = w_gate.shape
    ep = lst.shape[0] // b
    cap = lst.shape[1] * LANES
    assert b >= 2
    n_chunks = W_PER_STEP * b
    assert d == 2 * f and W_PER_STEP == 2 and d % (n_chunks * 2 * SUBLANES) == 0
    ch_rows, ch_cols = d // n_chunks, f
    half = cap // 2 if cap % 16 == 0 else cap

    def cur(ei, bi):
        return bi * ep + ei, 0, 0

    def nxt(ei, bi):
        nb = bi + 1
        return (nb % b) * ep + jnp.minimum(ei + nb // b, e - 1), 0, 0

    def smem(shape, index_map):
        return pl.BlockSpec(shape, index_map, memory_space=pltpu.SMEM)

    return pl.pallas_call(
        functools.partial(_ffn_kernel, cap=cap, half=half, n_experts=e, n_batch=b),
        grid=(e, b),
        in_specs=[smem((None, 1, cap), cur), smem((None, 1, cap), nxt),
                  pl.BlockSpec((None, cap, 1), cur),
                  pl.BlockSpec(memory_space=pl.ANY),
                  pl.BlockSpec(memory_space=pl.ANY),
                  pl.BlockSpec(memory_space=pl.ANY),
                  pl.BlockSpec(memory_space=pl.ANY),
                  pl.BlockSpec(memory_space=pl.ANY)],
        out_specs=pl.BlockSpec(memory_space=pl.ANY),
        out_shape=jax.ShapeDtypeStruct((b, t, d), F32),
        scratch_shapes=[pltpu.VMEM((2, cap, d), F32), pltpu.VMEM((2, cap, d), F32),
                        pltpu.VMEM((2, d, f), BF16), pltpu.VMEM((2, d, f), BF16), pltpu.VMEM((2, f, d), BF16),
                        pltpu.VMEM((3 * W_PER_STEP, ch_rows, ch_cols), F32),
                        pltpu.SemaphoreType.DMA((3, 2)), pltpu.SemaphoreType.DMA((3 * W_PER_STEP,))],
        input_output_aliases={4: 0},
        compiler_params=_cparams("arbitrary", "arbitrary"),
        name="expert_ffn",
    )(lst.reshape(b * ep, 1, cap), lst.reshape(b * ep, 1, cap), gates.reshape(b * ep, cap, 1),
      h2, acc0, w_gate, w_up, w_down)


def _final_kernel(x1_ref, acc_ref, g_ref, nrm_ref, o_ref):
    x = x1_ref[...] + g_ref[...] * acc_ref[...]
    o_ref[...] = x * lax.rsqrt(jnp.mean(x * x, axis=-1, keepdims=True) + NORM_EPS) * nrm_ref[...]


def _final(x1, acc, gate, norm_gain, bm):
    b, t, d = x1.shape
    def row(): return pl.BlockSpec((None, bm, d), lambda bi, i: (bi, i, 0))
    return pl.pallas_call(
        _final_kernel,
        grid=(b, t // bm),
        in_specs=[row(), row(), pl.BlockSpec((None, 1, d), lambda bi, i: (bi, 0, 0)),
                  pl.BlockSpec((1, d), lambda bi, i: (0, 0))],
        out_specs=row(),
        out_shape=jax.ShapeDtypeStruct((b, t, d), F32),
        compiler_params=_cparams("arbitrary", "arbitrary"),
        name="final_norm",
    )(x1, acc, gate, norm_gain.reshape(1, d))


def _row_block(t, largest):
    for bm in (1024, 512, 256, 128, 64):
        if bm <= largest and t % bm == 0:
            return bm
    raise ValueError(f"token count {t} is not a multiple of 64")


def kernel(x, c, ctx, c_ctx, ada_w, ada_b, norm_mix, norm_ffn, w_in, gdn_conv, gdn_a_log, gdn_dt_bias,
           hgrn_lb, hgrn_norm, gdn_norm, w_branch_a, w_branch_b, w_out, router_w, w_gate, w_up, w_down,
           final_norm):
    depth = ada_w.shape[0]
    assert depth == 1, "single-layer stack"
    b, t, d = x.shape
    tc = ctx.shape[1]
    n_experts = router_w.shape[-1]
    assert d % COL == 0 and (6 * COL) % d == 0 and t == CHUNK * GRID_W and tc % CHUNK == 0
    assert w_in.shape[-1] == 9 * COL + 4 * HEADS + 2 * d and n_experts <= LANES

    lower = jnp.cumsum(jax.nn.softmax(hgrn_lb.astype(F32), axis=1), axis=1)

    c_rows = jnp.zeros((SUBLANES, d), F32).at[:b].set(c).at[b].set(c_ctx)
    mods = _modulation(c_rows, ada_w[0], ada_b[0])
    ml = [m.reshape(b, 1, d) for m in jnp.split(mods[:b], 6, axis=-1)]
    mc = [m.reshape(1, 1, d) for m in jnp.split(mods[b:b + 1], 6, axis=-1)]

    wi = w_in[0]
    s_ab = 9 * COL
    s_gates = s_ab + 4 * HEADS
    w_main = jnp.concatenate([wi[:, :5 * COL], wi[:, 8 * COL:s_ab], wi[:, s_gates:]], axis=1).astype(BF16)
    w_qkv = wi[:, 5 * COL:8 * COL].astype(BF16)
    w_ab = jnp.pad(wi[:, s_ab:s_gates], ((0, 0), (0, LANES - 4 * HEADS))).astype(BF16)

    bm_lat, bm_ctx = _row_block(t, 1024), _row_block(tc, 1024)
    bn_main = 2 * COL if w_main.shape[1] % (2 * COL) == 0 else COL
    (p_lat,) = _input_projection(x, norm_mix[0], ml[0], ml[1], w_main, None, BF16, bm_lat, bn_main)
    qkv_lat, ab_lat = _input_projection(x, norm_mix[0], ml[0], ml[1], w_qkv, w_ab, F32, bm_lat)
    (p_ctx,) = _input_projection(ctx, norm_mix[0], mc[0], mc[1], w_main[:, :4 * COL], None, BF16, bm_ctx)
    qkv_ctx, ab_ctx = _input_projection(ctx, norm_mix[0], mc[0], mc[1], w_qkv, w_ab, F32, bm_ctx)

    oa_f, oa_b = _hgrn_scans(p_ctx, p_lat, lower[0, 0], lower[1, 0])

    x_ctx = _gdn_prep_ctx(qkv_ctx, gdn_conv[0])
    x_lat, ab_lat_cm = _gdn_prep_lat(qkv_lat, ab_lat, gdn_conv[0])
    pad = LANES - 2 * HEADS
    alog_row = jnp.pad(gdn_a_log[0].reshape(1, 2 * HEADS).astype(F32), ((0, 0), (0, pad)))
    dtb_row = jnp.pad(gdn_dt_bias[0].reshape(1, 2 * HEADS).astype(F32), ((0, 0), (0, pad)))
    ob_f, ob_b = _gdn_scans(x_ctx, x_lat, ab_ctx.reshape(b, tc // CHUNK, CHUNK, LANES), ab_lat_cm,
                            alog_row, dtb_row)

    u = _merge(p_lat, oa_f, oa_b, ob_f, ob_b, hgrn_norm[0], gdn_norm[0],
               w_branch_a[0].astype(BF16), w_branch_b[0].astype(BF16))
    router_pad = jnp.pad(router_w[0].astype(F32), ((0, 0), (0, LANES - n_experts)))
    bm_row = _row_block(t, 256)
    x1, h2, aff_t, acc0 = _outproj_router(u, w_out[0].astype(BF16), x, ml[2], norm_ffn[0], ml[3], ml[4],
                                          router_pad, n_experts, bm_row)

    cap = max(1, (CAPACITY_FACTOR * t) // n_experts)
    lst, gates = _route(aff_t.reshape(-1, t), cap)
    acc = _expert_ffn(h2, acc0, lst, gates, w_gate[0], w_up[0], w_down[0])
    return _final(x1, acc, ml[5], final_norm, bm_row)
```

```python
import functools

import jax
import jax.numpy as jnp
from jax import lax
from jax.experimental import pallas as pl
from jax.experimental.pallas import tpu as pltpu

F32 = jnp.float32
BF16 = jnp.bfloat16

NORM_EPS = 1e-6
LOG2E = 1.4426950408889634
GRID_W = 64
HEADS = 8
HEAD_DIM = 128
CHUNK = 64
CONV_K = 5
CAPACITY_FACTOR = 2
COL = HEADS * HEAD_DIM
LANES = 128
SUBLANES = 8
GROUP = SUBLANES
INV_BLOCK = 8
HGRN_SAMPLES_PER_STEP = 2
W_PER_STEP = 2
VMEM_LIMIT_BYTES = 56 * 1024 * 1024


def _cparams(*sem):
    return pltpu.CompilerParams(dimension_semantics=sem, vmem_limit_bytes=VMEM_LIMIT_BYTES)


def _dot(a, b):
    return jnp.dot(a, b, preferred_element_type=F32)


def _dot_nt(a, b):
    return lax.dot_general(a, b, (((1,), (1,)), ((), ())), preferred_element_type=F32)


def _silu(x):
    return x * jax.nn.sigmoid(x)


def _scan_cumsum(g, rev, terms):
    tri = jnp.where(_tri(rev), 1.0, 0.0).astype(BF16)
    total = None
    rest = g
    for _ in range(terms):
        part = rest.astype(BF16)
        rest = rest - part.astype(F32)
        total = _dot(tri, part) if total is None else total + _dot(tri, part)
    return total


def _resident(shape, index_map):
    return pl.BlockSpec(shape, index_map, pipeline_mode=pl.Buffered(1))


def _mod_kernel(c_ref, w_ref, b_ref, o_ref):
    o_ref[...] = _dot(_silu(c_ref[...]), w_ref[...]) + b_ref[...]


def _modulation(c_rows, ada_w, ada_b):
    rows, d = c_rows.shape
    n = ada_w.shape[1]
    bn = 1024
    return pl.pallas_call(
        _mod_kernel,
        grid=(n // bn,),
        in_specs=[pl.BlockSpec((rows, d), lambda j: (0, 0)),
                  pl.BlockSpec((d, bn), lambda j: (0, j)),
                  pl.BlockSpec((1, bn), lambda j: (0, j))],
        out_specs=pl.BlockSpec((rows, bn), lambda j: (0, j)),
        out_shape=jax.ShapeDtypeStruct((rows, n), F32),
        compiler_params=_cparams("arbitrary"),
        name="modulation",
    )(c_rows, ada_w, ada_b.reshape(1, n))


def _proj_kernel(x_ref, gain_ref, shift_ref, scale_ref, w_ref, *rest, with_ab):
    if with_ab:
        wab_ref, o_ref, oab_ref, h_ref = rest
    else:
        o_ref, h_ref = rest

    @pl.when(pl.program_id(2) == 0)
    def _():
        x = x_ref[...]
        rs = lax.rsqrt(jnp.mean(x * x, axis=-1, keepdims=True) + NORM_EPS)
        h = ((x * rs) * (gain_ref[...] * (1.0 + scale_ref[...])) + shift_ref[...]).astype(BF16)
        h_ref[...] = h
        if with_ab:
            oab_ref[...] = _dot(h, wab_ref[...])

    o_ref[...] = _dot(h_ref[...], w_ref[...]).astype(o_ref.dtype)


def _input_projection(x, gain, shift, scale, w, w_ab, out_dtype, bm, bn=COL):
    b, t, d = x.shape
    n = w.shape[1]
    assert n % bn == 0
    per_batch = shift.shape[0] > 1
    mod_map = (lambda bi, i, j: (bi, 0, 0)) if per_batch else (lambda bi, i, j: (0, 0, 0))
    with_ab = w_ab is not None
    in_specs = [pl.BlockSpec((None, bm, d), lambda bi, i, j: (bi, i, 0)),
                pl.BlockSpec((1, d), lambda bi, i, j: (0, 0)),
                pl.BlockSpec((None, 1, d), mod_map),
                pl.BlockSpec((None, 1, d), mod_map),
                pl.BlockSpec((d, bn), lambda bi, i, j: (0, j))]
    out_specs = [pl.BlockSpec((None, bm, bn), lambda bi, i, j: (bi, i, j))]
    out_shape = [jax.ShapeDtypeStruct((b, t, n), out_dtype)]
    args = [x, gain.reshape(1, d), shift, scale, w]
    if with_ab:
        in_specs.append(pl.BlockSpec((d, LANES), lambda bi, i, j: (0, 0)))
        out_specs.append(pl.BlockSpec((None, bm, LANES), lambda bi, i, j: (bi, i, 0)))
        out_shape.append(jax.ShapeDtypeStruct((b, t, LANES), F32))
        args.append(w_ab)
    return pl.pallas_call(
        functools.partial(_proj_kernel, with_ab=with_ab),
        grid=(b, t // bm, n // bn),
        in_specs=in_specs, out_specs=out_specs, out_shape=out_shape,
        scratch_shapes=[pltpu.VMEM((bm, d), BF16)],
        compiler_params=_cparams("arbitrary", "arbitrary", "arbitrary"),
        name="input_projection",
    )(*args)


def _tri(rev, strict=False):
    ri = lax.broadcasted_iota(jnp.int32, (CHUNK, CHUNK), 0)
    ci = lax.broadcasted_iota(jnp.int32, (CHUNK, CHUNK), 1)
    if rev:
        return (ci > ri) if strict else (ci >= ri)
    return (ci < ri) if strict else (ci <= ri)


def _boundary_rows(g, bs, rev):
    n, w = g.shape
    off = (bs >> 1) - 1 + (1 if rev else 0)
    if bs >= SUBLANES:
        pieces = [jnp.broadcast_to(g[p * bs + off:p * bs + off + 1], (bs, w)) for p in range(n // bs)]
        return jnp.concatenate(pieces, axis=0)
    sub = lax.broadcasted_iota(jnp.int32, (SUBLANES, w), 0)
    pieces = []
    for r0 in range(0, n, SUBLANES):
        acc = None
        for p in range(SUBLANES // bs):
            r = r0 + p * bs + off
            val = jnp.broadcast_to(g[r:r + 1], (SUBLANES, w))
            acc = val if acc is None else jnp.where(sub >= p * bs, val, acc)
        pieces.append(acc)
    return jnp.concatenate(pieces, axis=0)


def _hgrn_chunks(scans):
    row = lax.broadcasted_iota(jnp.int32, (CHUNK, 1), 0)
    ri = lax.broadcasted_iota(jnp.int32, (CHUNK, CHUNK), 0)
    ci = lax.broadcasted_iota(jnp.int32, (CHUNK, CHUNK), 1)
    n_levels = CHUNK.bit_length() - 1
    eye = ri == ci

    units = []
    for q_raw, f_raw, v, low, s_ref, rev in scans:
        q = _silu(q_raw.astype(F32))
        f_raw = f_raw.astype(F32)
        one_m = 1.0 - low
        f = low + one_m * jax.nn.sigmoid(f_raw)
        k = one_m * jax.nn.sigmoid(-f_raw)
        gc = _scan_cumsum(jnp.log(f), rev, 2) * LOG2E
        g_last = gc[0:1] if rev else gc[CHUNK - 1:CHUNK]
        q_dec = (q * jnp.exp2(gc)).astype(BF16)
        k_dec = (k * jnp.exp2(g_last - gc)).astype(BF16)
        mixed, pair = [], []
        for lv in range(1, n_levels + 1):
            upper = ((row >> (lv - 1)) & 1) == 1
            q_rows = jnp.logical_not(upper) if rev else upper
            decay = jnp.exp2(-jnp.abs(gc - _boundary_rows(gc, 1 << lv, rev)))
            mixed.append((jnp.where(q_rows, q, k) * decay).astype(BF16))
            i_upper = ((ri >> (lv - 1)) & 1) == 1
            j_upper = ((ci >> (lv - 1)) & 1) == 1
            halves = jnp.logical_and(j_upper, jnp.logical_not(i_upper)) if rev else \
                jnp.logical_and(i_upper, jnp.logical_not(j_upper))
            pair.append(jnp.logical_and((ri >> lv) == (ci >> lv), halves))
        qk = q * k
        v_b = v.astype(BF16)
        v_f = v.astype(F32)
        for h in range(HEADS):
            sl = slice(h * HEAD_DIM, (h + 1) * HEAD_DIM)
            units.append(dict(s_ref=s_ref, h=h, diag=jnp.sum(qk[:, sl], axis=1, keepdims=True),
                              mixed=[x[:, sl] for x in mixed], pair=pair,
                              q_dec=q_dec[:, sl], k_dec=k_dec[:, sl], v_b=v_b[:, sl], v_f=v_f[:, sl],
                              decay=jnp.exp2(g_last[:, sl])))

    scores = [[_dot_nt(m, m) for m in u["mixed"]] for u in units]
    states = [u["s_ref"][u["h"]] for u in units]
    carried = [_dot_nt(u["q_dec"], s.astype(BF16)) for u, s in zip(units, states)]
    outs = []
    for u, sc, car in zip(units, scores, carried):
        a = jnp.where(eye, u["diag"], 0.0)
        for lv in range(n_levels):
            a = jnp.where(u["pair"][lv], sc[lv], a)
        outs.append(_dot(a.astype(BF16), u["v_b"]) + car)
    for u, s in zip(units, states):
        u["s_ref"][u["h"]] = u["decay"] * s + _dot(u["v_f"].T.astype(BF16), u["k_dec"])
    return [jnp.concatenate(outs[i * HEADS:(i + 1) * HEADS], axis=1) for i in range(len(scans))]


def _hgrn_kernel(lowf_ref, lowb_ref,
                 cqf, cff, cvf, cqb, cfb, cvb,
                 lqf, lff, lvf, lqb, lfb, lvb,
                 of_ref, ob_ref, sf_ref, sb_ref, *, n_ctx, per_step):
    t = pl.program_id(1)
    n_samples = of_ref.shape[0]

    @pl.when(t == 0)
    def _():
        sf_ref[...] = jnp.zeros_like(sf_ref)
        sb_ref[...] = jnp.zeros_like(sb_ref)

    is_ctx = t < n_ctx

    def body(j, carry):
        rows_f = pl.ds(pl.multiple_of(j * CHUNK, CHUNK), CHUNK)
        rows_b = pl.ds(pl.multiple_of((per_step - 1 - j) * CHUNK, CHUNK), CHUNK)

        def pick(c_ref, l_ref, i, rows):
            return jnp.where(is_ctx, c_ref[i, rows, :], l_ref[i, rows, :])

        scans = []
        for i in range(n_samples):
            scans.append((pick(cqf, lqf, i, rows_f), pick(cff, lff, i, rows_f), pick(cvf, lvf, i, rows_f),
                          lowf_ref[...], sf_ref.at[i], False))
            scans.append((pick(cqb, lqb, i, rows_b), pick(cfb, lfb, i, rows_b), pick(cvb, lvb, i, rows_b),
                          lowb_ref[...], sb_ref.at[i], True))
        outs = _hgrn_chunks(scans)
        for i in range(n_samples):
            of_ref[i, rows_f, :] = outs[2 * i].astype(of_ref.dtype)
            ob_ref[i, rows_b, :] = outs[2 * i + 1].astype(ob_ref.dtype)
        return carry

    lax.fori_loop(0, per_step, body, 0)


def _hgrn_scans(p_ctx, p_lat, low_f, low_b):
    b, tc, _ = p_ctx.shape
    t = p_lat.shape[1]
    per_step = max(g for g in (4, 2, 1) if (tc // CHUNK) % g == 0 and (t // CHUNK) % g == 0)
    rows = per_step * CHUNK
    n_ctx, n_lat = tc // rows, t // rows

    def cf(ti): return jnp.minimum(ti, n_ctx - 1)
    def cb(ti): return jnp.maximum(n_ctx - 1 - ti, 0)
    def lf(ti): return jnp.maximum(ti - n_ctx, 0)
    def lb(ti): return n_lat - 1 - jnp.maximum(ti - n_ctx, 0)

    nb = HGRN_SAMPLES_PER_STEP if b % HGRN_SAMPLES_PER_STEP == 0 else 1

    def spec(block_of, col):
        return pl.BlockSpec((nb, rows, COL), lambda bi, ti: (bi, block_of(ti), col))

    low_spec = pl.BlockSpec((1, COL), lambda bi, ti: (0, 0))
    out_f, out_b = pl.pallas_call(
        functools.partial(_hgrn_kernel, n_ctx=n_ctx, per_step=per_step),
        grid=(b // nb, n_ctx + n_lat),
        in_specs=[low_spec, low_spec,
                  spec(cf, 0), spec(cf, 1), spec(cf, 3), spec(cb, 0), spec(cb, 2), spec(cb, 3),
                  spec(lf, 0), spec(lf, 1), spec(lf, 3), spec(lb, 0), spec(lb, 2), spec(lb, 3)],
        out_specs=[pl.BlockSpec((nb, rows, COL), lambda bi, ti: (bi, lf(ti), 0)),
                   pl.BlockSpec((nb, rows, COL), lambda bi, ti: (bi, lb(ti), 0))],
        out_shape=[jax.ShapeDtypeStruct((b, t, COL), BF16)] * 2,
        scratch_shapes=[pltpu.VMEM((nb, HEADS, HEAD_DIM, HEAD_DIM), F32)] * 2,
        compiler_params=_cparams("arbitrary", "arbitrary"),
        name="hgrn_scans",
    )(low_f.reshape(1, COL), low_b.reshape(1, COL), *([p_ctx] * 6), *([p_lat] * 6))
    return out_f, out_b


def _conv_silu_norm(x, prev, nxt, w, idx):
    row = lax.broadcasted_iota(jnp.int32, (CHUNK, 1), 0)
    xm1 = jnp.where(row == 0, prev[7:8], pltpu.roll(x, 1, 0))
    xm2 = jnp.where(row == 0, prev[6:7], jnp.where(row == 1, prev[7:8], pltpu.roll(x, 2, 0)))
    xp1 = jnp.where(row == CHUNK - 1, nxt[0:1], pltpu.roll(x, CHUNK - 1, 0))
    xp2 = jnp.where(row == CHUNK - 1, nxt[1:2],
                    jnp.where(row == CHUNK - 2, nxt[0:1], pltpu.roll(x, CHUNK - 2, 0)))
    return _silu_norm(w[0:1] * xm2 + w[1:2] * xm1 + w[2:3] * x + w[3:4] * xp1 + w[4:5] * xp2, idx)


def _silu_norm(y, idx):
    y = _silu(y)
    if idx == 2:
        return y
    scale = HEAD_DIM ** -0.5 if idx == 0 else 1.0
    pieces = []
    for h in range(HEADS):
        yh = y[:, h * HEAD_DIM:(h + 1) * HEAD_DIM]
        pieces.append(yh * (lax.rsqrt(jnp.sum(yh * yh, axis=1, keepdims=True) + NORM_EPS) * scale))
    return jnp.concatenate(pieces, axis=1)


def _gdn_prep_ctx_kernel(q_ref, k_ref, v_ref, pq_ref, pk_ref, pv_ref, nq_ref, nk_ref, nv_ref, w_ref, o_ref):
    t = pl.program_id(1)
    first = t == 0
    last = t == pl.num_programs(1) - 1
    for idx, (c_ref, p_ref, n_ref) in enumerate(((q_ref, pq_ref, nq_ref), (k_ref, pk_ref, nk_ref),
                                                 (v_ref, pv_ref, nv_ref))):
        prev = jnp.where(first, 0.0, p_ref[...])
        nxt = jnp.where(last, 0.0, n_ref[...])
        y = _conv_silu_norm(c_ref[...], prev, nxt, w_ref[:, idx * COL:(idx + 1) * COL], idx)
        o_ref[:, idx * COL:(idx + 1) * COL] = y.astype(o_ref.dtype)


def _gdn_prep_ctx(qkv, conv_w):
    b, t, _ = qkv.shape
    n_chunks = t // CHUNK
    rb = CHUNK // SUBLANES
    def cur(col): return pl.BlockSpec((None, CHUNK, COL), lambda bi, c: (bi, c, col))
    def prv(col): return pl.BlockSpec((None, SUBLANES, COL), lambda bi, c: (bi, jnp.maximum(c * rb - 1, 0), col))
    def nxt(col): return pl.BlockSpec((None, SUBLANES, COL),
                                      lambda bi, c: (bi, jnp.minimum((c + 1) * rb, n_chunks * rb - 1), col))
    cols = (0, 1, 2)
    return pl.pallas_call(
        _gdn_prep_ctx_kernel,
        grid=(b, n_chunks),
        in_specs=[cur(c) for c in cols] + [prv(c) for c in cols] + [nxt(c) for c in cols]
                 + [pl.BlockSpec((CONV_K, 3 * COL), lambda bi, c: (0, 0))],
        out_specs=pl.BlockSpec((None, None, CHUNK, 3 * COL), lambda bi, c: (bi, c, 0, 0)),
        out_shape=jax.ShapeDtypeStruct((b, n_chunks, CHUNK, 3 * COL), BF16),
        compiler_params=_cparams("arbitrary", "arbitrary"),
        name="gdn_prep_ctx",
    )(*([qkv] * 9), conv_w)


def _gdn_prep_lat_kernel(q_ref, k_ref, v_ref, pq_ref, pk_ref, pv_ref, nq_ref, nk_ref, nv_ref, ab_ref, w_ref,
                         o_ref, oab_ref):
    g = pl.program_id(1)
    first = g == 0
    last = g == pl.num_programs(1) - 1
    tail = CHUNK - SUBLANES
    for idx, (c_ref, p_ref, n_ref) in enumerate(((q_ref, pq_ref, nq_ref), (k_ref, pk_ref, nk_ref),
                                                 (v_ref, pv_ref, nv_ref))):
        w = w_ref[:, idx * COL:(idx + 1) * COL]
        cols = [c_ref[:, j, :] for j in range(GROUP)]
        before = jnp.where(first, 0.0, p_ref[:, GROUP - 1, :])
        after = jnp.where(last, 0.0, n_ref[:, 0, :])
        for j in range(GROUP):
            prev = cols[j - 1][tail:] if j > 0 else before
            nxt = cols[j + 1][:SUBLANES] if j < GROUP - 1 else after
            o_ref[j, :, idx * COL:(idx + 1) * COL] = _conv_silu_norm(cols[j], prev, nxt, w, idx).astype(o_ref.dtype)
    for j in range(GROUP):
        oab_ref[j] = ab_ref[:, j, :]


def _gdn_prep_lat(qkv, ab, conv_w):
    b, t, _ = qkv.shape
    assert t == CHUNK * GRID_W and GRID_W % GROUP == 0
    n_groups = GRID_W // GROUP
    rb = CHUNK // SUBLANES
    q4 = qkv.reshape(b, CHUNK, GRID_W, 3 * COL)
    ab4 = ab.reshape(b, CHUNK, GRID_W, LANES)
    def cur(col): return pl.BlockSpec((None, CHUNK, GROUP, COL), lambda bi, g: (bi, 0, g, col))
    def prv(col): return pl.BlockSpec((None, SUBLANES, GROUP, COL),
                                      lambda bi, g: (bi, rb - 1, jnp.maximum(g - 1, 0), col))
    def nxt(col): return pl.BlockSpec((None, SUBLANES, GROUP, COL),
                                      lambda bi, g: (bi, 0, jnp.minimum(g + 1, n_groups - 1), col))
    cols = (0, 1, 2)
    return pl.pallas_call(
        _gdn_prep_lat_kernel,
        grid=(b, n_groups),
        in_specs=[cur(c) for c in cols] + [prv(c) for c in cols] + [nxt(c) for c in cols]
                 + [pl.BlockSpec((None, CHUNK, GROUP, LANES), lambda bi, g: (bi, 0, g, 0)),
                    pl.BlockSpec((CONV_K, 3 * COL), lambda bi, g: (0, 0))],
        out_specs=[pl.BlockSpec((None, GROUP, CHUNK, 3 * COL), lambda bi, g: (bi, g, 0, 0)),
                   pl.BlockSpec((None, GROUP, CHUNK, LANES), lambda bi, g: (bi, g, 0, 0))],
        out_shape=[jax.ShapeDtypeStruct((b, GRID_W, CHUNK, 3 * COL), BF16),
                   jax.ShapeDtypeStruct((b, GRID_W, CHUNK, LANES), F32)],
        compiler_params=_cparams("arbitrary", "arbitrary"),
        name="gdn_prep_lat",
    )(*([q4] * 9), ab4, conv_w)


def _mm_bf16(a, b):
    return _dot(a.astype(BF16), b.astype(BF16))


def _unit_tri_inverses(ns):
    ri = lax.broadcasted_iota(jnp.int32, (CHUNK, CHUNK), 0)
    ci = lax.broadcasted_iota(jnp.int32, (CHUNK, CHUNK), 1)
    eye = jnp.where(ri == ci, 1.0, 0.0)
    shift = INV_BLOCK.bit_length() - 1
    on_diag = (ri >> shift) == (ci >> shift)

    def nilpotent_inverses(ms, degree):
        xs = [eye - m for m in ms]
        ps = ms
        for _ in range(degree.bit_length() - 2):
            ps = [_mm_bf16(p, p) for p in ps]
            xs = [x + _mm_bf16(x, p) for x, p in zip(xs, ps)]
        return xs

    d_invs = nilpotent_inverses([jnp.where(on_diag, n, 0.0) for n in ns], INV_BLOCK)
    ms = [_mm_bf16(d, jnp.where(on_diag, 0.0, n)) for d, n in zip(d_invs, ns)]
    ys = nilpotent_inverses(ms, CHUNK // INV_BLOCK)
    return [_mm_bf16(y, d) for y, d in zip(ys, d_invs)]


def _gdn_chunks(scans, alog, dtb):
    units = []
    for x, ab, s_ref, rev in scans:
        g_all = -jnp.exp(alog) * jax.nn.softplus(ab + dtb)
        beta_all = jax.nn.sigmoid(ab)
        gc_all = _scan_cumsum(g_all, rev, 3)
        gr_all = gc_all.T
        incl = _tri(rev)
        strict = _tri(rev, strict=True)
        off = HEADS if rev else 0
        for h in range(HEADS):
            gcol = gc_all[:, off + h:off + h + 1]
            grow = gr_all[off + h:off + h + 1, :]
            units.append(dict(
                s_ref=s_ref, h=h, strict=strict, gcol=gcol,
                beta=beta_all[:, 2 * HEADS + off + h:2 * HEADS + off + h + 1],
                g_last=gcol[0:1] if rev else gcol[CHUNK - 1:CHUNK],
                gamma=jnp.where(incl, jnp.exp(jnp.where(incl, gcol - grow, 0.0)), 0.0),
                q=x[:, h * HEAD_DIM:(h + 1) * HEAD_DIM],
                k=x[:, COL + h * HEAD_DIM:COL + (h + 1) * HEAD_DIM],
                v=x[:, 2 * COL + h * HEAD_DIM:2 * COL + (h + 1) * HEAD_DIM]))

    both = [_dot_nt(jnp.concatenate([u["q"], u["k"]], axis=0), u["k"]) for u in units]
    qks = [x[:CHUNK] for x in both]
    ns = [jnp.where(u["strict"], u["beta"] * x[CHUNK:] * u["gamma"], 0.0) for u, x in zip(units, both)]
    invs = _unit_tri_inverses(ns)
    rhss = [jnp.concatenate([u["v"].astype(F32) * u["beta"],
                             u["k"].astype(F32) * (u["beta"] * jnp.exp(u["gcol"]))], axis=1) for u in units]
    sols = [_mm_bf16(a, r) for a, r in zip(invs, rhss)]
    states = [u["s_ref"][u["h"]] for u in units]
    states_b = [s.astype(BF16) for s in states]
    v_news = [sol[:, :HEAD_DIM] - _dot_nt(sol[:, HEAD_DIM:].astype(BF16), s_b) for sol, s_b in zip(sols, states_b)]
    outs = [_dot_nt((u["q"].astype(F32) * jnp.exp(u["gcol"])).astype(BF16), s_b)
            + _mm_bf16(qk * u["gamma"], v_new)
            for u, qk, s_b, v_new in zip(units, qks, states_b, v_news)]
    for u, s, v_new in zip(units, states, v_news):
        k_dec = (u["k"].astype(F32) * jnp.exp(u["g_last"] - u["gcol"])).astype(BF16)
        u["s_ref"][u["h"]] = jnp.exp(u["g_last"]) * s + _dot(v_new.T.astype(BF16), k_dec)
    return [jnp.concatenate(outs[i * HEADS:(i + 1) * HEADS], axis=1) for i in range(len(scans))]


def _gdn_ctx_kernel(alog_ref, dtb_ref, x_ref, ab_ref, sf_ref, sb_ref, *, n_chunks):
    sf_ref[...] = jnp.zeros_like(sf_ref)
    sb_ref[...] = jnp.zeros_like(sb_ref)

    def body(j, carry):
        jb = n_chunks - 1 - j
        _gdn_chunks([(x_ref[j], ab_ref[j], sf_ref, False), (x_ref[jb], ab_ref[jb], sb_ref, True)],
                    alog_ref[...], dtb_ref[...])
        return carry

    lax.fori_loop(0, n_chunks, body, 0)


def _gdn_lat_kernel(alog_ref, dtb_ref, s0f_ref, s0b_ref, xf_ref, abf_ref, xb_ref, abb_ref,
                    of_ref, ob_ref, sf_ref, sb_ref):
    n_samples, per_step = xf_ref.shape[:2]

    @pl.when(pl.program_id(1) == 0)
    def _():
        sf_ref[...] = s0f_ref[...]
        sb_ref[...] = s0b_ref[...]

    def body(j, carry):
        jb = per_step - 1 - j
        scans = []
        for i in range(n_samples):
            scans.append((xf_ref[i, j], abf_ref[i, j], sf_ref.at[i], False))
            scans.append((xb_ref[i, jb], abb_ref[i, jb], sb_ref.at[i], True))
        outs = _gdn_chunks(scans, alog_ref[...], dtb_ref[...])
        for i in range(n_samples):
            of_ref[i, j] = outs[2 * i]
            ob_ref[i, jb] = outs[2 * i + 1]
        return carry

    lax.fori_loop(0, per_step, body, 0)


def _gdn_scans(x_ctx, x_lat, ab_ctx, ab_lat, alog_row, dtb_row):
    b, n_ctx = x_ctx.shape[:2]
    n_lat = x_lat.shape[1]
    state_shape = jax.ShapeDtypeStruct((b, HEADS, HEAD_DIM, HEAD_DIM), F32)
    def state_spec(nidx):
        return pl.BlockSpec((None, HEADS, HEAD_DIM, HEAD_DIM), lambda *i: (i[0], 0, 0, 0))
    s_f, s_b = pl.pallas_call(
        functools.partial(_gdn_ctx_kernel, n_chunks=n_ctx),
        grid=(b,),
        in_specs=[pl.BlockSpec((1, LANES), lambda bi: (0, 0)), pl.BlockSpec((1, LANES), lambda bi: (0, 0)),
                  pl.BlockSpec((None, n_ctx, CHUNK, 3 * COL), lambda bi: (bi, 0, 0, 0)),
                  pl.BlockSpec((None, n_ctx, CHUNK, LANES), lambda bi: (bi, 0, 0, 0))],
        out_specs=[state_spec(1), state_spec(1)],
        out_shape=[state_shape, state_shape],
        compiler_params=_cparams("arbitrary"),
        name="gdn_ctx_scans",
    )(alog_row, dtb_row, x_ctx, ab_ctx)

    nb = 2 if b % 2 == 0 else 1
    per_step = max(g for g in (4, 2, 1) if n_lat % g == 0)
    n_steps = n_lat // per_step
    def gf(ti): return ti
    def gb(ti): return n_steps - 1 - ti
    def lat_state_spec():
        return pl.BlockSpec((nb, HEADS, HEAD_DIM, HEAD_DIM), lambda bi, ti: (bi, 0, 0, 0))
    def xspec(group_of):
        return pl.BlockSpec((nb, per_step, CHUNK, 3 * COL), lambda bi, ti: (bi, group_of(ti), 0, 0))
    def abspec(group_of):
        return pl.BlockSpec((nb, per_step, CHUNK, LANES), lambda bi, ti: (bi, group_of(ti), 0, 0))
    def ospec(group_of):
        return pl.BlockSpec((nb, per_step, CHUNK, COL), lambda bi, ti: (bi, group_of(ti), 0, 0))
    row_spec = pl.BlockSpec((1, LANES), lambda bi, ti: (0, 0))
    return pl.pallas_call(
        _gdn_lat_kernel,
        grid=(b // nb, n_steps),
        in_specs=[row_spec, row_spec, lat_state_spec(), lat_state_spec(),
                  xspec(gf), abspec(gf), xspec(gb), abspec(gb)],
        out_specs=[ospec(gf), ospec(gb)],
        out_shape=[jax.ShapeDtypeStruct((b, n_lat, CHUNK, COL), F32)] * 2,
        scratch_shapes=[pltpu.VMEM((nb, HEADS, HEAD_DIM, HEAD_DIM), F32)] * 2,
        compiler_params=_cparams("arbitrary", "arbitrary"),
        name="gdn_lat_scans",
    )(alog_row, dtb_row, s_f, s_b, x_lat, ab_lat, x_lat, ab_lat)


def _head_norm(x):
    pieces = []
    for h in range(HEADS):
        xh = x[:, h * HEAD_DIM:(h + 1) * HEAD_DIM]
        pieces.append(xh * lax.rsqrt(jnp.mean(xh * xh, axis=1, keepdims=True) + NORM_EPS))
    return jnp.concatenate(pieces, axis=1)


def _merge_kernel(oaf, oab, og, obf, obb, z, ga, gb, hag, gbg, wba, wbb, u_ref):
    o_a = oaf[...].astype(F32) + oab[...].astype(F32)
    y_a = _head_norm(o_a * jax.nn.sigmoid(og[...].astype(F32))) * hag[...]
    o_b = jnp.concatenate([obf[:, r, :] + obb[:, r, :] for r in range(obf.shape[1])], axis=0)
    y_b = _head_norm(o_b) * gbg[...] * _silu(z[...].astype(F32))
    u = jax.nn.sigmoid(ga[...].astype(F32)) * _dot(y_a.astype(BF16), wba[...])
    u = u + jax.nn.sigmoid(gb[...].astype(F32)) * _dot(y_b.astype(BF16), wbb[...])
    u_ref[...] = u.astype(u_ref.dtype)


def _merge(p_lat, oa_f, oa_b, ob_f, ob_b, ha_gain, gb_gain, w_ba, w_bb):
    b, t, _ = p_lat.shape
    d = w_ba.shape[1]
    bm = SUBLANES * GRID_W
    g0 = 6 * COL // d
    def o_spec(): return pl.BlockSpec((None, bm, COL), lambda bi, i: (bi, i, 0))
    def ob_spec(): return pl.BlockSpec((None, GRID_W, SUBLANES, COL), lambda bi, i: (bi, 0, i, 0))
    def p_spec(col): return pl.BlockSpec((None, bm, COL), lambda bi, i: (bi, i, col))
    def gate_spec(k): return pl.BlockSpec((None, bm, d), lambda bi, i: (bi, i, g0 + k))
    def full(shape): return _resident(shape, lambda bi, i: (0,) * len(shape))
    return pl.pallas_call(
        _merge_kernel,
        grid=(b, t // bm),
        in_specs=[o_spec(), o_spec(), p_spec(4), ob_spec(), ob_spec(), p_spec(5),
                  gate_spec(0), gate_spec(1), full((1, COL)), full((1, COL)),
                  full((COL, d)), full((COL, d))],
        out_specs=pl.BlockSpec((None, bm, d), lambda bi, i: (bi, i, 0)),
        out_shape=jax.ShapeDtypeStruct((b, t, d), BF16),
        compiler_params=_cparams("arbitrary", "arbitrary"),
        name="merge",
    )(oa_f, oa_b, p_lat, ob_f, ob_b, p_lat, p_lat, p_lat,
      ha_gain.reshape(1, COL), gb_gain.reshape(1, COL), w_ba, w_bb)


def _outproj_kernel(u_ref, wout_ref, x_ref, g_ref, nrm_ref, sh_ref, sc_ref, rwh_ref, rwl_ref,
                    x1_ref, h2_ref, afft_ref, acc0_ref, *, n_experts):
    acc0_ref[...] = jnp.zeros_like(acc0_ref)
    x1 = x_ref[...] + g_ref[...] * _dot(u_ref[...], wout_ref[...])
    x1_ref[...] = x1
    rs = lax.rsqrt(jnp.mean(x1 * x1, axis=-1, keepdims=True) + NORM_EPS)
    h2 = (x1 * rs) * (nrm_ref[...] * (1.0 + sc_ref[...])) + sh_ref[...]
    h2_ref[...] = h2
    h_hi = h2.astype(BF16)
    h_lo = (h2 - h_hi.astype(F32)).astype(BF16)
    logits = _dot(h_hi, rwh_ref[...]) + _dot(h_hi, rwl_ref[...]) + _dot(h_lo, rwh_ref[...])
    lane = lax.broadcasted_iota(jnp.int32, logits.shape, 1)
    logits = jnp.where(lane < n_experts, logits, -jnp.inf)
    e = jnp.exp(logits - jnp.max(logits, axis=-1, keepdims=True))
    aff = e / jnp.sum(e, axis=-1, keepdims=True)
    afft_ref[...] = aff.T[:afft_ref.shape[0]]


def _outproj_router(u, w_out, x, gate, norm_gain, shift, scale, router_pad, n_experts, bm):
    b, t, d = x.shape
    ep = -(-n_experts // SUBLANES) * SUBLANES
    router_hi = router_pad.astype(BF16)
    router_lo = (router_pad - router_hi.astype(F32)).astype(BF16)
    def row(): return pl.BlockSpec((None, bm, d), lambda bi, i: (bi, i, 0))
    def mod(): return pl.BlockSpec((None, 1, d), lambda bi, i: (bi, 0, 0))
    return pl.pallas_call(
        functools.partial(_outproj_kernel, n_experts=n_experts),
        grid=(b, t // bm),
        in_specs=[row(), _resident((d, d), lambda bi, i: (0, 0)), row(), mod(),
                  pl.BlockSpec((1, d), lambda bi, i: (0, 0)), mod(), mod(),
                  _resident((d, LANES), lambda bi, i: (0, 0)), _resident((d, LANES), lambda bi, i: (0, 0))],
        out_specs=[row(), row(), pl.BlockSpec((None, ep, bm), lambda bi, i: (bi, 0, i)), row()],
        out_shape=[jax.ShapeDtypeStruct((b, t, d), F32), jax.ShapeDtypeStruct((b, t, d), F32),
                   jax.ShapeDtypeStruct((b, ep, t), F32), jax.ShapeDtypeStruct((b, t, d), F32)],
        compiler_params=_cparams("arbitrary", "arbitrary"),
        name="outproj_router",
    )(u, w_out, x, gate, norm_gain.reshape(1, d), shift, scale, router_hi, router_lo)


def _route_kernel(aff_ref, lst_ref, gate_ref, *, cap):
    a = aff_ref[...]
    rows, n_blk, _ = a.shape
    lane = lax.broadcasted_iota(jnp.int32, a.shape, 2)
    blk = lax.broadcasted_iota(jnp.int32, a.shape, 1)

    def count(mask):
        return jnp.sum(mask.astype(jnp.int32), axis=(1, 2), keepdims=True)

    def as_value(pattern):
        return lax.bitcast_convert_type(pattern, F32)

    def refine(i, lo):
        cand = lo | lax.shift_left(jnp.int32(1), 30 - i)
        return jnp.where(count(a >= as_value(cand)) >= cap, cand, lo)

    cut = as_value(lax.fori_loop(0, 31, refine, jnp.zeros((rows, 1, 1), jnp.int32)))

    def block_scan(m):
        x = m
        for k in range(LANES.bit_length() - 1):
            s = 1 << k
            x = x + jnp.where(lane >= s, pltpu.roll(x, s, 2), 0)
        return x, jnp.broadcast_to(x[:, :, LANES - 1:LANES], x.shape)

    def exclusive_prefix(m):
        incl, tot = block_scan(m)
        y = tot
        k = 0
        while (1 << k) < n_blk:
            s = 1 << k
            y = y + jnp.where(blk >= s, pltpu.roll(y, s, 1), 0)
            k += 1
        return incl - m + (y - tot)

    above = a > cut
    at_cut = a == cut
    need = cap - count(above)
    chosen = jnp.logical_or(above, jnp.logical_and(at_cut, exclusive_prefix(at_cut.astype(jnp.int32)) < need))
    token = blk * LANES + lane
    n_tok = n_blk * LANES

    def shifted(x, s):
        if s < LANES:
            y = pltpu.roll(x, LANES - s, 2)
            y = jnp.where(lane < LANES - s, y, pltpu.roll(y, n_blk - 1, 1)) if n_blk > 1 else y
        else:
            y = pltpu.roll(x, n_blk - s // LANES, 1)
        return jnp.where(token < n_tok - s, y, 0)

    m = chosen.astype(jnp.int32)
    val = jnp.where(chosen, token, 0)
    aff = jnp.where(chosen, a, 0.0)
    dist = jnp.where(chosen, token - exclusive_prefix(m), 0)
    valid = m
    for k in range((n_tok - 1).bit_length()):
        s = 1 << k
        moving = valid * ((dist >> k) & 1)
        arrive = shifted(moving, s) == 1
        stay = (valid - moving) == 1
        val = jnp.where(arrive, shifted(val, s), jnp.where(stay, val, 0))
        aff = jnp.where(arrive, shifted(aff, s), jnp.where(stay, aff, 0.0))
        dist = jnp.where(arrive, shifted(dist, s), jnp.where(stay, dist, 0))
        valid = jnp.logical_or(arrive, stay).astype(jnp.int32)
    lst_ref[...] = val[:, :lst_ref.shape[1], :]
    gate_ref[...] = aff[:, :gate_ref.shape[1], :]


def _route(aff_t, cap):
    r, t = aff_t.shape
    assert r % SUBLANES == 0 and t % LANES == 0 and cap % LANES == 0
    n_blk = t // LANES
    out_spec = pl.BlockSpec((SUBLANES, cap // LANES, LANES), lambda i: (i, 0, 0))
    return pl.pallas_call(
        functools.partial(_route_kernel, cap=cap),
        grid=(r // SUBLANES,),
        in_specs=[pl.BlockSpec((SUBLANES, n_blk, LANES), lambda i: (i, 0, 0))],
        out_specs=[out_spec, out_spec],
        out_shape=[jax.ShapeDtypeStruct((r, cap // LANES, LANES), jnp.int32),
                   jax.ShapeDtypeStruct((r, cap // LANES, LANES), F32)],
        compiler_params=_cparams("arbitrary"),
        name="route",
    )(aff_t.reshape(r, n_blk, LANES))


def _ffn_kernel(lst_ref, lst_next_ref, gate_ref, h_hbm, acc_in_hbm, wg_hbm, wu_hbm, wd_hbm,
                acc_hbm, x_buf, a_buf, wg_s, wu_s, wd_s, stage, sem, wsem, *, cap, half, n_experts, n_batch):
    e = pl.program_id(0)
    b = pl.program_id(1)
    step = e * n_batch + b
    n_steps = n_experts * n_batch
    slot = step % 2
    other = 1 - slot
    b_next = jnp.where(b + 1 == n_batch, 0, b + 1)
    w_slot = e % 2
    n_tasks, ch_rows, ch_cols = stage.shape

    def task(k, bb):
        m, cc = divmod(k, W_PER_STEP)
        row0 = (bb * W_PER_STEP + cc if m < 2 else bb) * ch_rows
        rows = pl.ds(row0 if isinstance(row0, int) else pl.multiple_of(row0, ch_rows), ch_rows)
        if m < 2:
            return (wg_hbm, wu_hbm)[m], (wg_s, wu_s)[m], (rows,)
        return wd_hbm, wd_s, (rows, pl.ds(cc * ch_cols, ch_cols))

    def task_copy(k, ee, bb):
        hbm, _, idx = task(k, bb)
        return pltpu.make_async_copy(hbm.at[(ee,) + idx], stage.at[k], wsem.at[k])

    def finish_task(k, ee, sl, bb):
        task_copy(k, ee, bb).wait()
        _, scratch, idx = task(k, bb)
        scratch.at[(sl,) + idx][...] = stage[k].astype(BF16)

    @pl.when(step == 0)
    def _():
        for bb in range(n_batch):
            for k in range(n_tasks):
                task_copy(k, 0, bb).start()
            for k in range(n_tasks):
                finish_task(k, 0, 0, bb)

    e_next = jnp.minimum(e + 1, n_experts - 1)
    for k in range(n_tasks):
        task_copy(k, e_next, b).start()

    def for_chosen(lst, fn):
        def body(s8, carry):
            for r in range(SUBLANES):
                s = s8 * SUBLANES + r
                fn(s, lst[0, s])
            return carry
        lax.fori_loop(0, cap // SUBLANES, body, 0)

    def start_gather(lst, bb, sl):
        def one(s, n):
            pltpu.make_async_copy(h_hbm.at[bb, pl.ds(n, 1)], x_buf.at[sl, pl.ds(s, 1)], sem.at[0, sl]).start()
            pltpu.make_async_copy(acc_in_hbm.at[bb, pl.ds(n, 1)], a_buf.at[sl, pl.ds(s, 1)],
                                  sem.at[1, sl]).start()
        for_chosen(lst, one)

    def wait_gather(sl):
        pltpu.make_async_copy(h_hbm.at[b, pl.ds(0, cap)], x_buf.at[sl], sem.at[0, sl]).wait()
        pltpu.make_async_copy(acc_in_hbm.at[b, pl.ds(0, cap)], a_buf.at[sl], sem.at[1, sl]).wait()

    def start_scatter(sl):
        def one(s, n):
            pltpu.make_async_copy(a_buf.at[sl, pl.ds(s, 1)], acc_hbm.at[b, pl.ds(n, 1)], sem.at[2, sl]).start()
        for_chosen(lst_ref, one)

    def wait_scatter(sl):
        pltpu.make_async_copy(a_buf.at[sl], acc_hbm.at[b, pl.ds(0, cap)], sem.at[2, sl]).wait()

    @pl.when(step == 0)
    def _():
        start_gather(lst_ref, b, slot)

    wait_gather(slot)
    pending = list(range(n_tasks))

    def convert_one():
        if pending:
            finish_task(pending.pop(0), e_next, 1 - w_slot, b)

    for r0 in range(0, cap, half):
        rows = pl.ds(r0, half)
        x = x_buf[slot, rows, :].astype(BF16)
        a = _dot(x, wg_s[w_slot])
        convert_one()
        u = _dot(x, wu_s[w_slot])
        convert_one()
        y = _dot((_silu(a) * u).astype(BF16), wd_s[w_slot]) * gate_ref[rows, :]
        a_buf[slot, rows, :] = a_buf[slot, rows, :] + y
        convert_one()
    while pending:
        convert_one()

    @pl.when(step > 0)
    def _():
        wait_scatter(other)

    @pl.when(step + 1 < n_steps)
    def _():
        start_gather(lst_next_ref, b_next, other)

    start_scatter(slot)

    @pl.when(step == n_steps - 1)
    def _():
        wait_scatter(slot)


def _expert_ffn(h2, acc0, lst, gates, w_gate, w_up, w_down):
    b, t, d = h2.shape
    e, _, f = w_gate.shape
    ep = lst.shape[0] // b
    cap = lst.shape[1] * LANES
    assert b >= 2
    n_chunks = W_PER_STEP * b
    assert d == 2 * f and W_PER_STEP == 2 and d % (n_chunks * 2 * SUBLANES) == 0
    ch_rows, ch_cols = d // n_chunks, f
    half = cap // 2 if cap % 16 == 0 else cap

    def cur(ei, bi):
        return bi * ep + ei, 0, 0

    def nxt(ei, bi):
        nb = bi + 1
        return (nb % b) * ep + jnp.minimum(ei + nb // b, e - 1), 0, 0

    def smem(shape, index_map):
        return pl.BlockSpec(shape, index_map, memory_space=pltpu.SMEM)

    return pl.pallas_call(
        functools.partial(_ffn_kernel, cap=cap, half=half, n_experts=e, n_batch=b),
        grid=(e, b),
        in_specs=[smem((None, 1, cap), cur), smem((None, 1, cap), nxt),
                  pl.BlockSpec((None, cap, 1), cur),
                  pl.BlockSpec(memory_space=pl.ANY),
                  pl.BlockSpec(memory_space=pl.ANY),
                  pl.BlockSpec(memory_space=pl.ANY),
                  pl.BlockSpec(memory_space=pl.ANY),
                  pl.BlockSpec(memory_space=pl.ANY)],
        out_specs=pl.BlockSpec(memory_space=pl.ANY),
        out_shape=jax.ShapeDtypeStruct((b, t, d), F32),
        scratch_shapes=[pltpu.VMEM((2, cap, d), F32), pltpu.VMEM((2, cap, d), F32),
                        pltpu.VMEM((2, d, f), BF16), pltpu.VMEM((2, d, f), BF16), pltpu.VMEM((2, f, d), BF16),
                        pltpu.VMEM((3 * W_PER_STEP, ch_rows, ch_cols), F32),
                        pltpu.SemaphoreType.DMA((3, 2)), pltpu.SemaphoreType.DMA((3 * W_PER_STEP,))],
        input_output_aliases={4: 0},
        compiler_params=_cparams("arbitrary", "arbitrary"),
        name="expert_ffn",
    )(lst.reshape(b * ep, 1, cap), lst.reshape(b * ep, 1, cap), gates.reshape(b * ep, cap, 1),
      h2, acc0, w_gate, w_up, w_down)


def _final_kernel(x1_ref, acc_ref, g_ref, nrm_ref, o_ref):
    x = x1_ref[...] + g_ref[...] * acc_ref[...]
    o_ref[...] = x * lax.rsqrt(jnp.mean(x * x, axis=-1, keepdims=True) + NORM_EPS) * nrm_ref[...]


def _final(x1, acc, gate, norm_gain, bm):
    b, t, d = x1.shape
    def row(): return pl.BlockSpec((None, bm, d), lambda bi, i: (bi, i, 0))
    return pl.pallas_call(
        _final_kernel,
        grid=(b, t // bm),
        in_specs=[row(), row(), pl.BlockSpec((None, 1, d), lambda bi, i: (bi, 0, 0)),
                  pl.BlockSpec((1, d), lambda bi, i: (0, 0))],
        out_specs=row(),
        out_shape=jax.ShapeDtypeStruct((b, t, d), F32),
        compiler_params=_cparams("arbitrary", "arbitrary"),
        name="final_norm",
    )(x1, acc, gate, norm_gain.reshape(1, d))


def _row_block(t, largest):
    for bm in (1024, 512, 256, 128, 64):
        if bm <= largest and t % bm == 0:
            return bm
    raise ValueError(f"token count {t} is not a multiple of 64")


def kernel(x, c, ctx, c_ctx, ada_w, ada_b, norm_mix, norm_ffn, w_in, gdn_conv, gdn_a_log, gdn_dt_bias,
           hgrn_lb, hgrn_norm, gdn_norm, w_branch_a, w_branch_b, w_out, router_w, w_gate, w_up, w_down,
           final_norm):
    depth = ada_w.shape[0]
    assert depth == 1, "single-layer stack"
    b, t, d = x.shape
    tc = ctx.shape[1]
    n_experts = router_w.shape[-1]
    assert d % COL == 0 and (6 * COL) % d == 0 and t == CHUNK * GRID_W and tc % CHUNK == 0
    assert w_in.shape[-1] == 9 * COL + 4 * HEADS + 2 * d and n_experts <= LANES

    lower = jnp.cumsum(jax.nn.softmax(hgrn_lb.astype(F32), axis=1), axis=1)

    c_rows = jnp.zeros((SUBLANES, d), F32).at[:b].set(c).at[b].set(c_ctx)
    mods = _modulation(c_rows, ada_w[0], ada_b[0])
    ml = [m.reshape(b, 1, d) for m in jnp.split(mods[:b], 6, axis=-1)]
    mc = [m.reshape(1, 1, d) for m in jnp.split(mods[b:b + 1], 6, axis=-1)]

    wi = w_in[0]
    s_ab = 9 * COL
    s_gates = s_ab + 4 * HEADS
    w_main = jnp.concatenate([wi[:, :5 * COL], wi[:, 8 * COL:s_ab], wi[:, s_gates:]], axis=1).astype(BF16)
    w_qkv = wi[:, 5 * COL:8 * COL].astype(BF16)
    w_ab = jnp.pad(wi[:, s_ab:s_gates], ((0, 0), (0, LANES - 4 * HEADS))).astype(BF16)

    bm_lat, bm_ctx = _row_block(t, 1024), _row_block(tc, 1024)
    bn_main = 2 * COL if w_main.shape[1] % (2 * COL) == 0 else COL
    (p_lat,) = _input_projection(x, norm_mix[0], ml[0], ml[1], w_main, None, BF16, bm_lat, bn_main)
    qkv_lat, ab_lat = _input_projection(x, norm_mix[0], ml[0], ml[1], w_qkv, w_ab, F32, bm_lat)
    (p_ctx,) = _input_projection(ctx, norm_mix[0], mc[0], mc[1], w_main[:, :4 * COL], None, BF16, bm_ctx)
    qkv_ctx, ab_ctx = _input_projection(ctx, norm_mix[0], mc[0], mc[1], w_qkv, w_ab, F32, bm_ctx)

    oa_f, oa_b = _hgrn_scans(p_ctx, p_lat, lower[0, 0], lower[1, 0])

    x_ctx = _gdn_prep_ctx(qkv_ctx, gdn_conv[0])
    x_lat, ab_lat_cm = _gdn_prep_lat(qkv_lat, ab_lat, gdn_conv[0])
    pad = LANES - 2 * HEADS
    alog_row = jnp.pad(gdn_a_log[0].reshape(1, 2 * HEADS).astype(F32), ((0, 0), (0, pad)))
    dtb_row = jnp.pad(gdn_dt_bias[0].reshape(1, 2 * HEADS).astype(F32), ((0, 0), (0, pad)))
    ob_f, ob_b = _gdn_scans(x_ctx, x_lat, ab_ctx.reshape(b, tc // CHUNK, CHUNK, LANES), ab_lat_cm,
                            alog_row, dtb_row)

    u = _merge(p_lat, oa_f, oa_b, ob_f, ob_b, hgrn_norm[0], gdn_norm[0],
               w_branch_a[0].astype(BF16), w_branch_b[0].astype(BF16))
    router_pad = jnp.pad(router_w[0].astype(F32), ((0, 0), (0, LANES - n_experts)))
    bm_row = _row_block(t, 256)
    x1, h2, aff_t, acc0 = _outproj_router(u, w_out[0].astype(BF16), x, ml[2], norm_ffn[0], ml[3], ml[4],
                                          router_pad, n_experts, bm_row)

    cap = max(1, (CAPACITY_FACTOR * t) // n_experts)
    lst, gates = _route(aff_t.reshape(-1, t), cap)
    acc = _expert_ffn(h2, acc0, lst, gates, w_gate[0], w_up[0], w_down[0])
    return _final(x1, acc, ml[5], final_norm, bm_row)
```
